```python
import jax
import jax.numpy as jnp
from jax import lax
import numpy as np

D_MODEL = 4096
BATCH = 4
SEQ = 2048
DEPTH = 2
DEC_BATCH = 32
DEC_SEQ = 1
PAST_LEN = 16384
PAGE_SIZE = 128

N_EVEN = (DEPTH + 1) // 2
N_ODD = DEPTH // 2
CONV_CH = D_MODEL // 2
CONV_WIDTH = 31
RET_HEADS = 8
RET_HEAD_DIM = (D_MODEL // 2) // RET_HEADS
RET_WIDTH = RET_HEADS * RET_HEAD_DIM
RET_CHUNK = 128
EVEN_IN = 2 * CONV_CH + 4 * RET_WIDTH
ATT_HEAD_DIM = 64
N_HEADS = D_MODEL // ATT_HEAD_DIM
N_KV_HEADS = 8
GROUP = N_HEADS // N_KV_HEADS
WINDOW = 128
ODD_IN = (N_HEADS + 2 * N_KV_HEADS) * ATT_HEAD_DIM
ROPE_THETA = 10000.0
N_EXPERTS = 128
TOP_K = 8
N_GROUPS = 8
TOPK_GROUPS = 4
EXPERTS_PER_GROUP = N_EXPERTS // N_GROUPS
EXPERT_FF = 512
SHARED_FF = 512
ROUTED_SCALE = 2.5
MOE_BLOCK = 128
DN_ALPHA = (2.0 * DEPTH) ** 0.25
DN_BETA = (8.0 * DEPTH) ** -0.25
LN_EPS = 1e-5

kernel_name = 'hybrid_conv_retention_swa_moe_step'


def layer_norm(x, g, b):
    xf = x.astype(jnp.float32)
    mu = xf.mean(-1, keepdims=True)
    var = jnp.square(xf - mu).mean(-1, keepdims=True)
    return ((xf - mu) * lax.rsqrt(var + LN_EPS) * g + b).astype(x.dtype)


def rotary(x, pos):
    half = x.shape[-1] // 2
    inv_freq = ROPE_THETA ** (-jnp.arange(half, dtype=jnp.float32) / half)
    ang = pos.astype(jnp.float32)[:, None] * inv_freq[None, :]
    cos = jnp.cos(ang)[None, :, None, :]
    sin = jnp.sin(ang)[None, :, None, :]
    xf = x.astype(jnp.float32)
    x1, x2 = xf[..., :half], xf[..., half:]
    return jnp.concatenate([x1 * cos - x2 * sin, x2 * cos + x1 * sin], axis=-1).astype(x.dtype)


def conformer_conv(a, gate, buf, w_dw, b_dw, ln_g, ln_b):
    u = a * jax.nn.sigmoid(gate)
    full = jnp.concatenate([buf.astype(u.dtype), u], axis=1)
    y = lax.conv_general_dilated(full, w_dw[:, None, :].astype(u.dtype), window_strides=(1,), padding='VALID',
                                 dimension_numbers=('NWC', 'WIO', 'NWC'), feature_group_count=u.shape[-1])
    y = jax.nn.silu(layer_norm(y + b_dw, ln_g, ln_b))
    return y, full[:, -(CONV_WIDTH - 1):]


def retention(q, k, v, state):
    B, L, H, d = q.shape
    c = RET_CHUNK if L % RET_CHUNK == 0 else L
    n = L // c
    log_g = jnp.log1p(-jnp.exp2(-5.0 - jnp.arange(H, dtype=jnp.float32)))
    idx = jnp.arange(c, dtype=jnp.float32)
    rel = idx[:, None] - idx[None, :]
    intra = jnp.where(rel[None] >= 0, jnp.exp(log_g[:, None, None] * jnp.maximum(rel, 0.0)[None]), 0.0)
    q_decay = jnp.exp(log_g[:, None] * (idx + 1.0)[None, :])
    k_decay = jnp.exp(log_g[:, None] * (c - 1.0 - idx)[None, :])
    chunk_decay = jnp.exp(log_g * c)[None, :, None, None]
    qc = q.reshape(B, n, c, H, d)
    kc = k.reshape(B, n, c, H, d)
    vc = v.reshape(B, n, c, H, d)
    scores = jnp.einsum('bnihd,bnjhd->bnhij', qc, kc) * intra[None, None]
    o_intra = jnp.einsum('bnhij,bnjhe->bnihe', scores, vc)
    kv = jnp.einsum('bnjhd,hj,bnjhe->nbhde', kc, k_decay, vc)

    def step(s, kv_n):
        return chunk_decay * s + kv_n, s

    s_final, s_prev = lax.scan(step, state, kv)
    o_cross = jnp.einsum('bnihd,hi,nbhde->bnihe', qc, q_decay, s_prev)
    return (o_intra + o_cross).reshape(B, L, H, d), s_final


def band_mask(lq):
    i = jnp.arange(lq)[:, None]
    j = jnp.arange(WINDOW + lq)[None, :]
    return (j >= i) & (j <= WINDOW + i)


def sink_attend(q, k, v, mask, sinks):
    s = jnp.einsum('bnqhgd,bnshd->bnhgqs', q, k).astype(jnp.float32) * (ATT_HEAD_DIM ** -0.5)
    s = jnp.where(mask[None, :, None, None], s, -jnp.inf)
    sink = sinks.astype(jnp.float32)[None, None, :, :, None, None]
    m = jnp.maximum(s.max(-1, keepdims=True), sink)
    p = jnp.exp(s - m)
    p = p / (p.sum(-1, keepdims=True) + jnp.exp(sink - m))
    return jnp.einsum('bnhgqs,bnshd->bnqhgd', p.astype(v.dtype), v)


def even_mixer(x, pos, conv_buf, ret_state, p, i):
    B, L, _ = x.shape
    proj = jnp.einsum('bld,de->ble', x, p['w_in_even'][i])
    o1 = CONV_CH
    o2 = 2 * CONV_CH
    o3 = o2 + RET_WIDTH
    o4 = o3 + RET_WIDTH
    o5 = o4 + RET_WIDTH
    a, gate = proj[..., :o1], proj[..., o1:o2]
    q, k, v, g = proj[..., o2:o3], proj[..., o3:o4], proj[..., o4:o5], proj[..., o5:]
    if conv_buf is None:
        conv_buf = jnp.zeros((B, CONV_WIDTH - 1, CONV_CH), x.dtype)
    if ret_state is None:
        ret_state = jnp.zeros((B, RET_HEADS, RET_HEAD_DIM, RET_HEAD_DIM), jnp.float32)
    conv_out, conv_new = conformer_conv(a, gate, conv_buf, p['conv_w'][i], p['conv_b'][i],
                                        p['conv_ln_g'][i], p['conv_ln_b'][i])
    qh = rotary(q.reshape(B, L, RET_HEADS, RET_HEAD_DIM), pos).astype(jnp.float32)
    kh = rotary(k.reshape(B, L, RET_HEADS, RET_HEAD_DIM), pos).astype(jnp.float32) * (RET_HEAD_DIM ** -0.5)
    vh = v.reshape(B, L, RET_HEADS, RET_HEAD_DIM).astype(jnp.float32)
    o, s_new = retention(qh, kh, vh, ret_state.astype(jnp.float32))
    mu = o.mean(-1, keepdims=True)
    var = jnp.square(o - mu).mean(-1, keepdims=True)
    o = (o - mu) * lax.rsqrt(var + LN_EPS) * p['ret_gn_g'][i].reshape(RET_HEADS, RET_HEAD_DIM) \
        + p['ret_gn_b'][i].reshape(RET_HEADS, RET_HEAD_DIM)
    o = (o.reshape(B, L, RET_WIDTH) * jax.nn.silu(g.astype(jnp.float32))).astype(x.dtype)
    mixed = jnp.concatenate([conv_out, o], axis=-1)
    out = jnp.einsum('ble,ed->bld', mixed, p['w_out_even'][i])
    return out, conv_new, s_new.astype(x.dtype)


def odd_mixer(x, pos, k_buf, v_buf, p, i):
    B, L, _ = x.shape
    proj = jnp.einsum('bld,de->ble', x, p['w_qkv_odd'][i]) + p['b_qkv_odd'][i]
    qw = N_HEADS * ATT_HEAD_DIM
    kw = N_KV_HEADS * ATT_HEAD_DIM
    q = rotary(proj[..., :qw].reshape(B, L, N_HEADS, ATT_HEAD_DIM), pos)
    q = q.reshape(B, L, N_KV_HEADS, GROUP, ATT_HEAD_DIM)
    k = rotary(proj[..., qw:qw + kw].reshape(B, L, N_KV_HEADS, ATT_HEAD_DIM), pos)
    v = proj[..., qw + kw:].reshape(B, L, N_KV_HEADS, ATT_HEAD_DIM)
    sinks = p['sinks'][i].reshape(N_KV_HEADS, GROUP)
    if k_buf is None:
        nb = L // WINDOW
        qb = q.reshape(B, nb, WINDOW, N_KV_HEADS, GROUP, ATT_HEAD_DIM)
        kb = k.reshape(B, nb, WINDOW, N_KV_HEADS, ATT_HEAD_DIM)
        vb = v.reshape(B, nb, WINDOW, N_KV_HEADS, ATT_HEAD_DIM)
        pad = ((0, 0), (1, 0), (0, 0), (0, 0), (0, 0))
        k_band = jnp.concatenate([jnp.pad(kb[:, :-1], pad), kb], axis=2)
        v_band = jnp.concatenate([jnp.pad(vb[:, :-1], pad), vb], axis=2)
        has_prev = (jnp.arange(nb) > 0)[:, None, None] | (jnp.arange(2 * WINDOW) >= WINDOW)[None, None, :]
        mask = band_mask(WINDOW)[None] & has_prev
        o = sink_attend(qb, k_band, v_band, mask, sinks)
        k_new, v_new = k[:, -WINDOW:], v[:, -WINDOW:]
    else:
        k_all = jnp.concatenate([k_buf.astype(k.dtype), k], axis=1)
        v_all = jnp.concatenate([v_buf.astype(v.dtype), v], axis=1)
        o = sink_attend(q[:, None], k_all[:, None], v_all[:, None], band_mask(L)[None], sinks)
        k_new, v_new = k_all[:, -WINDOW:], v_all[:, -WINDOW:]
    o = o.reshape(B, L, N_HEADS * ATT_HEAD_DIM)
    out = jnp.einsum('ble,ed->bld', o, p['w_out_odd'][i]) + p['b_out_odd'][i]
    return out, k_new, v_new


def route(x, w_router, e_bias):
    T = x.shape[0]
    scores = jax.nn.sigmoid(jnp.einsum('td,de->te', x, w_router).astype(jnp.float32))
    biased = scores + e_bias.astype(jnp.float32)
    grp_score = lax.top_k(biased.reshape(T, N_GROUPS, EXPERTS_PER_GROUP), 2)[0].sum(-1)
    _, gidx = lax.top_k(grp_score, TOPK_GROUPS)
    gmask = (gidx[..., None] == jnp.arange(N_GROUPS)[None, None, :]).any(axis=1)
    emask = jnp.repeat(gmask, EXPERTS_PER_GROUP, axis=1)
    _, idx = lax.top_k(jnp.where(emask, biased, -jnp.inf), TOP_K)
    w = jnp.take_along_axis(scores, idx, axis=1)
    w = w / w.sum(-1, keepdims=True) * ROUTED_SCALE
    return idx, w


def routed_experts(x, idx, wts, w_gate, w_up, w_down):
    T, D = x.shape
    A = T * TOP_K
    blk = min(MOE_BLOCK, max(8, A // N_EXPERTS))
    n_blocks = -(-A // blk) + N_EXPERTS
    flat_e = idx.reshape(A)
    order = jnp.argsort(flat_e)
    sorted_e = flat_e[order]
    counts = jnp.bincount(flat_e, length=N_EXPERTS)
    padded = (counts + blk - 1) // blk * blk
    pad_end = jnp.cumsum(padded)
    pad_start = pad_end - padded
    start = jnp.cumsum(counts) - counts
    dest_sorted = (pad_start[sorted_e] + jnp.arange(A) - start[sorted_e]).astype(jnp.int32)
    dest = jnp.zeros((A,), jnp.int32).at[order].set(dest_sorted)
    slot_token = jnp.full((n_blocks * blk,), T, jnp.int32).at[dest].set(jnp.arange(A, dtype=jnp.int32) // TOP_K)
    block_e = jnp.minimum(jnp.searchsorted(pad_end, jnp.arange(n_blocks) * blk, side='right'), N_EXPERTS - 1)
    x_pad = jnp.concatenate([x, jnp.zeros((1, D), x.dtype)], axis=0)

    def one_block(args):
        tok, e = args
        xb = x_pad[tok]
        h = jax.nn.silu(xb @ w_gate[e]) * (xb @ w_up[e])
        return h @ w_down[e]

    yb = lax.map(one_block, (slot_token.reshape(n_blocks, blk), block_e)).reshape(n_blocks * blk, D)
    y = yb[dest].reshape(T, TOP_K, D)
    return jnp.einsum('tkd,tk->td', y, wts.astype(y.dtype))


def moe(x, p, layer):
    B, L, D = x.shape
    xt = x.reshape(B * L, D)
    idx, wts = route(xt, p['router_w'][layer], p['router_bias'][layer])
    routed = routed_experts(xt, idx, wts, p['exp_w_gate'][layer], p['exp_w_up'][layer], p['exp_w_down'][layer])
    shared = (jax.nn.silu(xt @ p['shared_w_gate'][layer]) * (xt @ p['shared_w_up'][layer])) @ p['shared_w_down'][layer]
    return (routed + shared).reshape(B, L, D)


def run_trunk(x, pos, st_conv, st_ret, st_wk, st_wv, p):
    fresh = st_conv is None
    conv_out, ret_out, wk_out, wv_out = [], [], [], []
    for layer in range(DEPTH):
        i = layer // 2
        if layer % 2 == 0:
            h, c_new, r_new = even_mixer(x, pos, None if fresh else st_conv[i], None if fresh else st_ret[i], p, i)
            conv_out.append(c_new)
            ret_out.append(r_new)
        else:
            h, k_new, v_new = odd_mixer(x, pos, None if fresh else st_wk[i], None if fresh else st_wv[i], p, i)
            wk_out.append(k_new)
            wv_out.append(v_new)
        x = layer_norm(DN_ALPHA * x + h, p['ln_g'][layer, 0], p['ln_b'][layer, 0])
        x = layer_norm(DN_ALPHA * x + moe(x, p, layer), p['ln_g'][layer, 1], p['ln_b'][layer, 1])
    return x, jnp.stack(conv_out), jnp.stack(ret_out), jnp.stack(wk_out), jnp.stack(wv_out)


def setup_inputs(seed: int = 0) -> dict:
    key = jax.random.key(seed)
    ks = jax.random.split(key, 32)

    def nrm(k, shape, scale):
        return jax.random.normal(k, shape, jnp.float32) * scale

    return {
        'x_prompt': nrm(ks[0], (BATCH, SEQ, D_MODEL), 1.0),
        'x_sample': nrm(ks[1], (DEC_BATCH, DEC_SEQ, D_MODEL), 1.0),
        'state_conv': nrm(ks[2], (N_EVEN, DEC_BATCH, CONV_WIDTH - 1, CONV_CH), 0.5),
        'state_ret': nrm(ks[3], (N_EVEN, DEC_BATCH, RET_HEADS, RET_HEAD_DIM, RET_HEAD_DIM), 0.5),
        'state_win_k': nrm(ks[4], (N_ODD, DEC_BATCH, WINDOW, N_KV_HEADS, ATT_HEAD_DIM), 1.0),
        'state_win_v': nrm(ks[5], (N_ODD, DEC_BATCH, WINDOW, N_KV_HEADS, ATT_HEAD_DIM), 1.0),
        'w_in_even': nrm(ks[6], (N_EVEN, D_MODEL, EVEN_IN), D_MODEL ** -0.5),
        'conv_w': nrm(ks[7], (N_EVEN, CONV_WIDTH, CONV_CH), CONV_WIDTH ** -0.5),
        'conv_b': nrm(ks[8], (N_EVEN, CONV_CH), 0.02),
        'conv_ln_g': 1.0 + nrm(ks[9], (N_EVEN, CONV_CH), 0.02),
        'conv_ln_b': nrm(ks[10], (N_EVEN, CONV_CH), 0.02),
        'ret_gn_g': 1.0 + nrm(ks[11], (N_EVEN, RET_WIDTH), 0.02),
        'ret_gn_b': nrm(ks[12], (N_EVEN, RET_WIDTH), 0.02),
        'w_out_even': nrm(ks[13], (N_EVEN, CONV_CH + RET_WIDTH, D_MODEL), DN_BETA * (CONV_CH + RET_WIDTH) ** -0.5),
        'w_qkv_odd': nrm(ks[14], (N_ODD, D_MODEL, ODD_IN), D_MODEL ** -0.5),
        'b_qkv_odd': nrm(ks[15], (N_ODD, ODD_IN), 0.02),
        'sinks': nrm(ks[16], (N_ODD, N_HEADS), 0.5),
        'w_out_odd': nrm(ks[17], (N_ODD, N_HEADS * ATT_HEAD_DIM, D_MODEL), DN_BETA * (N_HEADS * ATT_HEAD_DIM) ** -0.5),
        'b_out_odd': nrm(ks[18], (N_ODD, D_MODEL), 0.02),
        'ln_g': 1.0 + nrm(ks[19], (DEPTH, 2, D_MODEL), 0.02),
        'ln_b': nrm(ks[20], (DEPTH, 2, D_MODEL), 0.02),
        'router_w': nrm(ks[21], (DEPTH, D_MODEL, N_EXPERTS), D_MODEL ** -0.5),
        'router_bias': nrm(ks[22], (DEPTH, N_EXPERTS), 0.01),
        'exp_w_gate': nrm(ks[23], (DEPTH, N_EXPERTS, D_MODEL, EXPERT_FF), D_MODEL ** -0.5),
        'exp_w_up': nrm(ks[24], (DEPTH, N_EXPERTS, D_MODEL, EXPERT_FF), D_MODEL ** -0.5),
        'exp_w_down': nrm(ks[25], (DEPTH, N_EXPERTS, EXPERT_FF, D_MODEL), DN_BETA * EXPERT_FF ** -0.5),
        'shared_w_gate': nrm(ks[26], (DEPTH, D_MODEL, SHARED_FF), D_MODEL ** -0.5),
        'shared_w_up': nrm(ks[27], (DEPTH, D_MODEL, SHARED_FF), D_MODEL ** -0.5),
        'shared_w_down': nrm(ks[28], (DEPTH, SHARED_FF, D_MODEL), DN_BETA * SHARED_FF ** -0.5),
    }


def reference(x_prompt, x_sample, state_conv, state_ret, state_win_k, state_win_v,
              w_in_even, conv_w, conv_b, conv_ln_g, conv_ln_b, ret_gn_g, ret_gn_b, w_out_even,
              w_qkv_odd, b_qkv_odd, sinks, w_out_odd, b_out_odd, ln_g, ln_b,
              router_w, router_bias, exp_w_gate, exp_w_up, exp_w_down,
              shared_w_gate, shared_w_up, shared_w_down):
    p = dict(w_in_even=w_in_even, conv_w=conv_w, conv_b=conv_b, conv_ln_g=conv_ln_g, conv_ln_b=conv_ln_b,
             ret_gn_g=ret_gn_g, ret_gn_b=ret_gn_b, w_out_even=w_out_even,
             w_qkv_odd=w_qkv_odd, b_qkv_odd=b_qkv_odd, sinks=sinks, w_out_odd=w_out_odd, b_out_odd=b_out_odd,
             ln_g=ln_g, ln_b=ln_b, router_w=router_w, router_bias=router_bias,
             exp_w_gate=exp_w_gate, exp_w_up=exp_w_up, exp_w_down=exp_w_down,
             shared_w_gate=shared_w_gate, shared_w_up=shared_w_up, shared_w_down=shared_w_down)
    pos_prompt = jnp.arange(x_prompt.shape[1], dtype=jnp.int32)
    pos_sample = PAST_LEN + jnp.arange(x_sample.shape[1], dtype=jnp.int32)
    y_prompt, conv_p, ret_p, wk_p, wv_p = run_trunk(x_prompt, pos_prompt, None, None, None, None, p)
    y_sample, conv_s, ret_s, wk_s, wv_s = run_trunk(x_sample, pos_sample, state_conv, state_ret,
                                                    state_win_k, state_win_v, p)
    return (y_prompt, y_sample, conv_p, conv_s, ret_p, ret_s, wk_p, wk_s, wv_p, wv_s)
```

```python
import functools

import jax
import jax.numpy as jnp
import numpy as np
from jax import lax
from jax.experimental import pallas as pl
from jax.experimental.pallas import tpu as pltpu

F32 = jnp.float32
BF16 = jnp.bfloat16

D_MODEL = 4096
DEPTH = 2
PAST_LEN = 16384
CONV_CH = D_MODEL // 2
CONV_WIDTH = 31
CONV_HALO = 32
RET_HEADS = 8
RET_HEAD_DIM = 256
RET_WIDTH = RET_HEADS * RET_HEAD_DIM
RET_CHUNK = 128
ATT_HEAD_DIM = 64
N_HEADS = 64
N_KV_HEADS = 8
GROUP = N_HEADS // N_KV_HEADS
WINDOW = 128
Q_WIDTH = N_HEADS * ATT_HEAD_DIM
KV_WIDTH = N_KV_HEADS * ATT_HEAD_DIM
ROPE_THETA = 10000.0
N_EXPERTS = 128
TOP_K = 8
N_GROUPS = 8
TOPK_GROUPS = 4
EXPERTS_PER_GROUP = N_EXPERTS // N_GROUPS
EXPERT_FF = 512
ROUTED_SCALE = 2.5
DN_ALPHA = (2.0 * DEPTH) ** 0.25
LN_EPS = 1e-5

LANES = 128
V7X_VMEM_LIMIT = 52 * 1024 * 1024
MOE_TM = 256


def _params(*sem):
    return pltpu.CompilerParams(dimension_semantics=sem, vmem_limit_bytes=V7X_VMEM_LIMIT)


def _silu(x):
    return x * jax.nn.sigmoid(x)


def _rot_half_64(x):
    n = x.shape[-1]
    lane = lax.broadcasted_iota(jnp.int32, x.shape, x.ndim - 1)
    fwd = pltpu.roll(x, n - ATT_HEAD_DIM // 2, x.ndim - 1)
    bwd = pltpu.roll(x, ATT_HEAD_DIM // 2, x.ndim - 1)
    return jnp.where(lane % ATT_HEAD_DIM < ATT_HEAD_DIM // 2, -fwd, bwd)


def _mm_body(*refs, has_bias, rope_tiles):
    x_ref, w_ref = refs[0], refs[1]
    pos = 2
    b_ref = None
    if has_bias:
        b_ref = refs[pos]
        pos += 1
    if rope_tiles:
        cos_ref, sin_ref = refs[pos], refs[pos + 1]
        pos += 2
    o_ref = refs[pos]
    acc = jnp.dot(x_ref[...].astype(BF16), w_ref[...].astype(BF16), preferred_element_type=F32)
    if has_bias:
        acc = acc + b_ref[...]
    if rope_tiles:
        reps = acc.shape[1] // LANES
        cos = jnp.concatenate([cos_ref[...]] * reps, axis=1)
        sin = jnp.concatenate([sin_ref[...]] * reps, axis=1)
        @pl.when(pl.program_id(1) < rope_tiles)
        def _():
            o_ref[...] = (acc * cos + _rot_half_64(acc) * sin).astype(o_ref.dtype)

        @pl.when(pl.program_id(1) >= rope_tiles)
        def _():
            o_ref[...] = acc.astype(o_ref.dtype)
    else:
        o_ref[...] = acc.astype(o_ref.dtype)


def matmul(x, w, bias=None, rope=None, out_dtype=F32, tm=1024, tn=512):
    M, K = x.shape
    N = w.shape[1]
    tm = min(tm, M)
    tn = min(tn, N)
    assert M % tm == 0 and N % tn == 0
    in_specs = [pl.BlockSpec((tm, K), lambda i, j: (i, 0)),
                pl.BlockSpec((K, tn), lambda i, j: (0, j))]
    args = [x, w]
    if bias is not None:
        in_specs.append(pl.BlockSpec((1, tn), lambda i, j: (0, j)))
        args.append(bias.reshape(1, N))
    rope_tiles = 0
    if rope is not None:
        cos, sin, n_cols = rope
        assert n_cols % tn == 0 and cos.shape[0] % tm == 0
        rope_tiles = n_cols // tn
        nrb = cos.shape[0] // tm
        in_specs += [pl.BlockSpec((tm, LANES), lambda i, j: (i % nrb, 0))] * 2
        args += [cos, sin]
    return pl.pallas_call(
        functools.partial(_mm_body, has_bias=bias is not None, rope_tiles=rope_tiles),
        grid=(M // tm, N // tn),
        in_specs=in_specs,
        out_specs=pl.BlockSpec((tm, tn), lambda i, j: (i, j)),
        out_shape=jax.ShapeDtypeStruct((M, N), out_dtype),
        compiler_params=_params("parallel", "arbitrary"),
    )(*args)


def _split_bf16(x):
    hi = x.astype(BF16)
    lo = (x - hi.astype(F32)).astype(BF16)
    return hi, lo


def _router_body(x_ref, w_ref, o_ref):
    xh, xl = _split_bf16(x_ref[...])
    wh, wl = _split_bf16(w_ref[...])
    acc = jnp.dot(xh, wh, preferred_element_type=F32)
    acc = acc + jnp.dot(xl, wh, preferred_element_type=F32)
    acc = acc + jnp.dot(xh, wl, preferred_element_type=F32)
    o_ref[...] = acc


def router_logits(x, w, tm=512):
    M, K = x.shape
    N = w.shape[1]
    tm = min(tm, M)
    return pl.pallas_call(
        _router_body,
        grid=(M // tm,),
        in_specs=[pl.BlockSpec((tm, K), lambda i: (i, 0)), pl.BlockSpec((K, N), lambda i: (0, 0))],
        out_specs=pl.BlockSpec((tm, N), lambda i: (i, 0)),
        out_shape=jax.ShapeDtypeStruct((M, N), F32),
        compiler_params=_params("parallel"),
    )(x, w)


def _layer_norm_rows(z, g, b):
    mu = jnp.mean(z, axis=-1, keepdims=True)
    zc = z - mu
    var = jnp.mean(zc * zc, axis=-1, keepdims=True)
    return zc * lax.rsqrt(var + LN_EPS) * g + b


def _ln_body(*refs, n_h):
    x_ref = refs[0]
    h_refs = refs[1:1 + n_h]
    g_ref, b_ref, o_ref, ob_ref = refs[1 + n_h:]
    z = DN_ALPHA * x_ref[...]
    for h_ref in h_refs:
        z = z + h_ref[...]
    y = _layer_norm_rows(z, g_ref[...], b_ref[...])
    o_ref[...] = y
    ob_ref[...] = y.astype(BF16)


def deepnorm(x, hs, g, b, tm=256):
    M, D = x.shape
    tm = min(tm, M)
    assert M % tm == 0
    in_specs = [pl.BlockSpec((tm, D), lambda i: (i, 0))]
    args = [x]
    for h, off in hs:
        assert off % tm == 0
        in_specs.append(pl.BlockSpec((tm, D), functools.partial(lambda i, o: (i + o, 0), o=off // tm)))
        args.append(h)
    in_specs += [pl.BlockSpec((1, D), lambda i: (0, 0))] * 2
    args += [g.reshape(1, D), b.reshape(1, D)]
    return pl.pallas_call(
        functools.partial(_ln_body, n_h=len(hs)),
        grid=(M // tm,),
        in_specs=in_specs,
        out_specs=[pl.BlockSpec((tm, D), lambda i: (i, 0))] * 2,
        out_shape=[jax.ShapeDtypeStruct((M, D), F32), jax.ShapeDtypeStruct((M, D), BF16)],
        compiler_params=_params("parallel"),
    )(*args)


def _conv_prompt_body(a_ref, g_ref, pa_ref, pg_ref, w_ref, bdw_ref, lng_ref, lnb_ref,
                      o_ref, st_ref, full_ref, y_ref, *, tl):
    i = pl.program_id(1)
    u = a_ref[0] * jax.nn.sigmoid(g_ref[0])
    up = pa_ref[0] * jax.nn.sigmoid(pg_ref[0])
    up = jnp.where(i > 0, up, 0.0)
    full_ref[0:CONV_HALO, :] = up
    full_ref[CONV_HALO:CONV_HALO + tl, :] = u
    base = CONV_HALO - (CONV_WIDTH - 1)
    n_col = u.shape[1] // LANES

    def col_step(c, carry):
        off = pl.multiple_of(c * LANES, LANES)
        acc = jnp.zeros((tl, LANES), F32)
        for j in range(CONV_WIDTH):
            acc = acc + w_ref[j:j + 1, pl.ds(off, LANES)] * full_ref[base + j:base + j + tl, pl.ds(off, LANES)]
        y_ref[:, pl.ds(off, LANES)] = acc
        return carry

    lax.fori_loop(0, n_col, col_step, 0)
    y = _layer_norm_rows(y_ref[...] + bdw_ref[...], lng_ref[...], lnb_ref[...])
    o_ref[0] = _silu(y).astype(o_ref.dtype)

    @pl.when(i == pl.num_programs(1) - 1)
    def _():
        st_ref[0] = full_ref[CONV_HALO + tl - (CONV_WIDTH - 1):CONV_HALO + tl, :]


def conv_prompt(proj3, w_dw, b_dw, ln_g, ln_b, tl=128):
    B, L, _ = proj3.shape
    C = CONV_CH
    nh = tl // CONV_HALO
    vec = lambda v: v.reshape(1, C)
    return pl.pallas_call(
        functools.partial(_conv_prompt_body, tl=tl),
        grid=(B, L // tl),
        in_specs=[
            pl.BlockSpec((1, tl, C), lambda b, i: (b, i, 0)),
            pl.BlockSpec((1, tl, C), lambda b, i: (b, i, 1)),
            pl.BlockSpec((1, CONV_HALO, C), lambda b, i: (b, jnp.maximum(i * nh - 1, 0), 0)),
            pl.BlockSpec((1, CONV_HALO, C), lambda b, i: (b, jnp.maximum(i * nh - 1, 0), 1)),
            pl.BlockSpec((CONV_WIDTH, C), lambda b, i: (0, 0)),
            pl.BlockSpec((1, C), lambda b, i: (0, 0)),
            pl.BlockSpec((1, C), lambda b, i: (0, 0)),
            pl.BlockSpec((1, C), lambda b, i: (0, 0)),
        ],
        out_specs=[
            pl.BlockSpec((1, tl, C), lambda b, i: (b, i, 0)),
            pl.BlockSpec((1, CONV_WIDTH - 1, C), lambda b, i: (b, 0, 0)),
        ],
        out_shape=[jax.ShapeDtypeStruct((B, L, C), BF16),
                   jax.ShapeDtypeStruct((B, CONV_WIDTH - 1, C), F32)],
        scratch_shapes=[pltpu.VMEM((CONV_HALO + tl, C), F32), pltpu.VMEM((tl, C), F32)],
        compiler_params=_params("parallel", "arbitrary"),
    )(proj3, proj3, proj3, proj3, w_dw, vec(b_dw), vec(ln_g), vec(ln_b))


def _conv_sample_body(a_ref, g_ref, buf_ref, w_ref, bdw_ref, lng_ref, lnb_ref, o_ref, st_ref):
    u = a_ref[...] * jax.nn.sigmoid(g_ref[...])
    nb = CONV_WIDTH - 1
    acc = w_ref[nb:nb + 1, :] * u
    for j in range(nb):
        acc = acc + w_ref[j:j + 1, :] * buf_ref[:, j, :]
    y = _layer_norm_rows(acc + bdw_ref[...], lng_ref[...], lnb_ref[...])
    o_ref[...] = _silu(y).astype(o_ref.dtype)
    st_ref[:, 0:nb - 1, :] = buf_ref[:, 1:nb, :]
    st_ref[:, nb - 1, :] = u


def conv_sample(proj, buf, w_dw, b_dw, ln_g, ln_b, tb=8):
    B = proj.shape[0]
    C = CONV_CH
    nb = CONV_WIDTH - 1
    vec = lambda v: v.reshape(1, C)
    return pl.pallas_call(
        _conv_sample_body,
        grid=(B // tb,),
        in_specs=[
            pl.BlockSpec((tb, C), lambda b: (b, 0)),
            pl.BlockSpec((tb, C), lambda b: (b, 1)),
            pl.BlockSpec((tb, nb, C), lambda b: (b, 0, 0)),
            pl.BlockSpec((CONV_WIDTH, C), lambda b: (0, 0)),
            pl.BlockSpec((1, C), lambda b: (0, 0)),
            pl.BlockSpec((1, C), lambda b: (0, 0)),
            pl.BlockSpec((1, C), lambda b: (0, 0)),
        ],
        out_specs=[pl.BlockSpec((tb, C), lambda b: (b, 0)),
                   pl.BlockSpec((tb, nb, C), lambda b: (b, 0, 0))],
        out_shape=[jax.ShapeDtypeStruct((B, C), BF16), jax.ShapeDtypeStruct((B, nb, C), F32)],
        compiler_params=_params("parallel"),
    )(proj, proj, buf, w_dw, vec(b_dw), vec(ln_g), vec(ln_b))


def _ret_log_gamma():
    return np.log1p(-np.exp2(-5.0 - np.arange(RET_HEADS, dtype=np.float64)))


def _group_norm_gate(o, gate, gn_g, gn_b):
    mu = jnp.mean(o, axis=-1, keepdims=True)
    oc = o - mu
    var = jnp.mean(oc * oc, axis=-1, keepdims=True)
    return (oc * lax.rsqrt(var + LN_EPS) * gn_g + gn_b) * _silu(gate)


def _rope_256(x, cos, sin):
    half = RET_HEAD_DIM // 2
    x1, x2 = x[..., :half], x[..., half:]
    return jnp.concatenate([x1 * cos - x2 * sin, x2 * cos + x1 * sin], axis=-1)


def _ret_prompt_body(q_ref, k_ref, v_ref, g_ref, cos_ref, sin_ref, intra_ref, qd_ref, kd_ref, cd_ref,
                     gng_ref, gnb_ref, o_ref, s_ref):
    c = pl.program_id(2)

    @pl.when(c == 0)
    def _():
        s_ref[...] = jnp.zeros_like(s_ref)

    cos, sin = cos_ref[...], sin_ref[...]
    q = _rope_256(q_ref[0], cos, sin)
    k = _rope_256(k_ref[0], cos, sin) * (RET_HEAD_DIM ** -0.5)
    vb = v_ref[0].astype(BF16)
    s_prev = s_ref[0, 0]
    scores = lax.dot_general(q.astype(BF16), k.astype(BF16), (((1,), (1,)), ((), ())),
                             preferred_element_type=F32) * intra_ref[0]
    o = jnp.dot(scores.astype(BF16), vb, preferred_element_type=F32)
    o = o + jnp.dot((q * qd_ref[0]).astype(BF16), s_prev.astype(BF16), preferred_element_type=F32)
    kdt = (k * kd_ref[0]).T.astype(BF16)
    kv = jnp.dot(kdt, vb, preferred_element_type=F32)
    s_ref[0, 0] = cd_ref[0] * s_prev + kv
    o_ref[0] = _group_norm_gate(o, g_ref[0], gng_ref[...], gnb_ref[...]).astype(o_ref.dtype)


def retention_prompt(proj3, cos, sin, gn_g, gn_b):
    B, L, _ = proj3.shape
    H, d, c = RET_HEADS, RET_HEAD_DIM, RET_CHUNK
    lg = _ret_log_gamma()
    idx = np.arange(c, dtype=np.float64)
    rel = idx[:, None] - idx[None, :]
    intra = np.where(rel[None] >= 0, np.exp(lg[:, None, None] * np.maximum(rel, 0.0)[None]), 0.0)
    ones = np.ones((1, 1, d))
    qd = np.exp(lg[:, None] * (idx + 1.0)[None, :])[:, :, None] * ones
    kd = np.exp(lg[:, None] * (c - 1.0 - idx)[None, :])[:, :, None] * ones
    cd = np.exp(lg * c)[:, None, None] * ones
    col0 = 2 * CONV_CH // d
    head_spec = lambda off: pl.BlockSpec((1, c, d), lambda b, h, n: (b, n, col0 + off * H + h))
    tab_spec = pl.BlockSpec((1, c, d), lambda b, h, n: (h, 0, 0))
    return pl.pallas_call(
        _ret_prompt_body,
        grid=(B, H, L // c),
        in_specs=[
            head_spec(0), head_spec(1), head_spec(2), head_spec(3),
            pl.BlockSpec((c, d // 2), lambda b, h, n: (n, 0)),
            pl.BlockSpec((c, d // 2), lambda b, h, n: (n, 0)),
            pl.BlockSpec((1, c, c), lambda b, h, n: (h, 0, 0)),
            tab_spec, tab_spec,
            pl.BlockSpec((1, 1, d), lambda b, h, n: (h, 0, 0)),
            pl.BlockSpec((1, d), lambda b, h, n: (0, h)),
            pl.BlockSpec((1, d), lambda b, h, n: (0, h)),
        ],
        out_specs=[
            pl.BlockSpec((1, c, d), lambda b, h, n: (b, n, h)),
            pl.BlockSpec((1, 1, d, d), lambda b, h, n: (b, h, 0, 0)),
        ],
        out_shape=[jax.ShapeDtypeStruct((B, L, H * d), BF16),
                   jax.ShapeDtypeStruct((B, H, d, d), F32)],
        compiler_params=_params("parallel", "parallel", "arbitrary"),
    )(proj3, proj3, proj3, proj3, cos, sin,
      jnp.asarray(intra, F32), jnp.asarray(qd, F32), jnp.asarray(kd, F32), jnp.asarray(cd, F32),
      gn_g.reshape(1, H * d), gn_b.reshape(1, H * d))


def _ret_sample_body(qc_ref, kc_ref, v_ref, g_ref, s_ref, cosc_ref, sinc_ref, gam_ref, gng_ref, gnb_ref,
                     o_ref, so_ref):
    half = RET_HEAD_DIM // 2
    cos, sin = cosc_ref[...], sinc_ref[...]

    def rope_col(x):
        x1, x2 = x[:half], x[half:]
        return jnp.concatenate([x1 * cos - x2 * sin, x2 * cos + x1 * sin], axis=0)

    for h in range(RET_HEADS):
        lo, hi = h * RET_HEAD_DIM, (h + 1) * RET_HEAD_DIM
        q = rope_col(qc_ref[0, h])
        k = rope_col(kc_ref[0, h]) * (RET_HEAD_DIM ** -0.5)
        v = v_ref[0, :, lo:hi]
        s_new = gam_ref[h] * s_ref[0, h] + k * v
        so_ref[0, h] = s_new
        o = jnp.sum(q * s_new, axis=0, keepdims=True)
        o_ref[0, :, lo:hi] = _group_norm_gate(o, g_ref[0, :, lo:hi], gng_ref[:, lo:hi],
                                              gnb_ref[:, lo:hi]).astype(o_ref.dtype)


def retention_sample(q_col, k_col, v_row, g_row, state, cos_col, sin_col, gn_g, gn_b):
    B = state.shape[0]
    H, d = RET_HEADS, RET_HEAD_DIM
    gam = jnp.asarray(np.exp(_ret_log_gamma()), F32)
    col_spec = pl.BlockSpec((1, H, d, 1), lambda b: (b, 0, 0, 0))
    row_spec = pl.BlockSpec((1, 1, H * d), lambda b: (b, 0, 0))
    return pl.pallas_call(
        _ret_sample_body,
        grid=(B,),
        in_specs=[
            col_spec, col_spec, row_spec, row_spec,
            pl.BlockSpec((1, H, d, d), lambda b: (b, 0, 0, 0)),
            pl.BlockSpec((d // 2, 1), lambda b: (0, 0)),
            pl.BlockSpec((d // 2, 1), lambda b: (0, 0)),
            pl.BlockSpec(memory_space=pltpu.SMEM),
            pl.BlockSpec((1, H * d), lambda b: (0, 0)),
            pl.BlockSpec((1, H * d), lambda b: (0, 0)),
        ],
        out_specs=[row_spec, pl.BlockSpec((1, H, d, d), lambda b: (b, 0, 0, 0))],
        out_shape=[jax.ShapeDtypeStruct((B, 1, H * d), BF16), jax.ShapeDtypeStruct((B, H, d, d), F32)],
        compiler_params=_params("parallel"),
    )(q_col, k_col, v_row, g_row, state, cos_col, sin_col, gam, gn_g.reshape(1, H * d), gn_b.reshape(1, H * d))


def _pair_operand(x2, h, lane):
    col = x2[:, (h // 2) * LANES:(h // 2 + 1) * LANES]
    swapped = pltpu.roll(col, ATT_HEAD_DIM, 1)
    first, second = (col, swapped) if h % 2 == 0 else (swapped, col)
    return jnp.concatenate([jnp.where(lane < ATT_HEAD_DIM, first, 0.0),
                            jnp.where(lane >= ATT_HEAD_DIM, second, 0.0)], axis=0).astype(BF16)


def _swa_prompt_body(sink_ref, q_ref, kc_ref, kp_ref, vc_ref, vp_ref, o_ref):
    j = pl.program_id(1)
    W = WINDOW
    k2 = jnp.concatenate([kp_ref[0], kc_ref[0]], axis=0)
    v2 = jnp.concatenate([vp_ref[0], vc_ref[0]], axis=0)
    row = lax.broadcasted_iota(jnp.int32, (W, 2 * W), 0)
    col = lax.broadcasted_iota(jnp.int32, (W, 2 * W), 1)
    first_key = jnp.where(j > 0, 0, W)
    mask = (col >= jnp.maximum(row, first_key)) & (col <= W + row)
    lane_kv = lax.broadcasted_iota(jnp.int32, (2 * W, LANES), 1)
    lane_o = lax.broadcasted_iota(jnp.int32, (W, LANES), 1)
    scale = ATT_HEAD_DIM ** -0.5
    for h in range(N_KV_HEADS):
        kk = _pair_operand(k2, h, lane_kv)
        vv = _pair_operand(v2, h, lane_kv)
        for p in range(GROUP // 2):
            c0 = h * GROUP * ATT_HEAD_DIM + p * LANES
            qp = q_ref[0, :, c0:c0 + LANES].astype(BF16)
            s = lax.dot_general(qp, kk, (((1,), (1,)), ((), ())), preferred_element_type=F32) * scale
            probs, inv = [], []
            for t in range(2):
                sink = sink_ref[h * GROUP + 2 * p + t]
                st = jnp.where(mask, s[:, t * 2 * W:(t + 1) * 2 * W], -jnp.inf)
                m = jnp.maximum(jnp.max(st, axis=-1, keepdims=True), sink)
                e = jnp.exp(st - m)
                denom = jnp.sum(e, axis=-1, keepdims=True) + jnp.exp(sink - m)
                probs.append(e.astype(BF16))
                inv.append(1.0 / denom)
            o2 = jnp.dot(jnp.concatenate(probs, axis=1), vv, preferred_element_type=F32)
            o2 = o2 * jnp.where(lane_o < ATT_HEAD_DIM, inv[0], inv[1])
            o_ref[0, :, c0:c0 + LANES] = o2.astype(o_ref.dtype)


def swa_prompt(qkv3, sinks):
    B, L, _ = qkv3.shape
    W = WINDOW
    kcol = Q_WIDTH // KV_WIDTH
    return pl.pallas_call(
        _swa_prompt_body,
        grid_spec=pltpu.PrefetchScalarGridSpec(
            num_scalar_prefetch=0,
            grid=(B, L // W),
            in_specs=[
                pl.BlockSpec(memory_space=pltpu.SMEM),
                pl.BlockSpec((1, W, Q_WIDTH), lambda b, j: (b, j, 0)),
                pl.BlockSpec((1, W, KV_WIDTH), lambda b, j: (b, j, kcol)),
                pl.BlockSpec((1, W, KV_WIDTH), lambda b, j: (b, jnp.maximum(j - 1, 0), kcol)),
                pl.BlockSpec((1, W, KV_WIDTH), lambda b, j: (b, j, kcol + 1)),
                pl.BlockSpec((1, W, KV_WIDTH), lambda b, j: (b, jnp.maximum(j - 1, 0), kcol + 1)),
            ],
            out_specs=pl.BlockSpec((1, W, Q_WIDTH), lambda b, j: (b, j, 0)),
        ),
        out_shape=jax.ShapeDtypeStruct((B, L, Q_WIDTH), BF16),
        compiler_params=_params("parallel", "parallel"),
    )(sinks, qkv3, qkv3, qkv3, qkv3, qkv3)


def _swa_sample_body(q_ref, kn_ref, vn_ref, knf_ref, vnf_ref, kb_ref, vb_ref, sink_ref, o_ref, ko_ref, vo_ref):
    W = WINDOW
    scale = ATT_HEAD_DIM ** -0.5
    kb = kb_ref[0]
    vb = vb_ref[0]
    for h in range(N_KV_HEADS):
        lo, hi = h * ATT_HEAD_DIM, (h + 1) * ATT_HEAD_DIM
        qh = q_ref[0, h * GROUP:(h + 1) * GROUP, :]
        kn = kn_ref[0, h:h + 1, :]
        vn = vn_ref[0, h:h + 1, :]
        sink = sink_ref[h * GROUP:(h + 1) * GROUP, :]
        s_buf = lax.dot_general(qh.astype(BF16), kb[:, lo:hi].astype(BF16), (((1,), (1,)), ((), ())),
                                preferred_element_type=F32) * scale
        s_new = jnp.sum(qh * kn, axis=-1, keepdims=True) * scale
        m = jnp.maximum(jnp.maximum(jnp.max(s_buf, axis=-1, keepdims=True), s_new), sink)
        e_buf = jnp.exp(s_buf - m)
        e_new = jnp.exp(s_new - m)
        denom = jnp.sum(e_buf, axis=-1, keepdims=True) + e_new + jnp.exp(sink - m)
        o = jnp.dot(e_buf.astype(BF16), vb[:, lo:hi].astype(BF16), preferred_element_type=F32)
        o = o + e_new * vn
        o_ref[0, h * GROUP:(h + 1) * GROUP, :] = (o / denom).astype(o_ref.dtype)
    ko_ref[0, 0:W - 1, :] = kb[1:W, :]
    ko_ref[0, W - 1:W, :] = knf_ref[0]
    vo_ref[0, 0:W - 1, :] = vb[1:W, :]
    vo_ref[0, W - 1:W, :] = vnf_ref[0]


def swa_sample(q3, kn3, vn3, k_buf, v_buf, sinks):
    B = q3.shape[0]
    W, d = WINDOW, ATT_HEAD_DIM
    head_spec = lambda n: pl.BlockSpec((1, n, d), lambda b: (b, 0, 0))
    flat_spec = pl.BlockSpec((1, 1, KV_WIDTH), lambda b: (b, 0, 0))
    buf_spec = pl.BlockSpec((1, W, KV_WIDTH), lambda b: (b, 0, 0))
    return pl.pallas_call(
        _swa_sample_body,
        grid=(B,),
        in_specs=[head_spec(N_HEADS), head_spec(N_KV_HEADS), head_spec(N_KV_HEADS), flat_spec, flat_spec,
                  buf_spec, buf_spec, pl.BlockSpec((N_HEADS, 1), lambda b: (0, 0))],
        out_specs=[head_spec(N_HEADS), buf_spec, buf_spec],
        out_shape=[jax.ShapeDtypeStruct((B, N_HEADS, d), BF16),
                   jax.ShapeDtypeStruct((B, W, KV_WIDTH), F32),
                   jax.ShapeDtypeStruct((B, W, KV_WIDTH), F32)],
        compiler_params=_params("parallel"),
    )(q3, kn3, vn3, kn3.reshape(B, 1, KV_WIDTH), vn3.reshape(B, 1, KV_WIDTH), k_buf, v_buf,
      sinks.reshape(N_HEADS, 1))


def _ffn_up_body(be_ref, na_ref, x_ref, wg_ref, wu_ref, h_ref):
    @pl.when(pl.program_id(0) < na_ref[0])
    def _():
        x = x_ref[...].astype(BF16)
        g = jnp.dot(x, wg_ref[0].astype(BF16), preferred_element_type=F32)
        u = jnp.dot(x, wu_ref[0].astype(BF16), preferred_element_type=F32)
        h_ref[...] = (_silu(g) * u).astype(h_ref.dtype)


def _ffn_down_body(be_ref, na_ref, h_ref, wd_ref, y_ref):
    @pl.when(pl.program_id(0) < na_ref[0])
    def _():
        y_ref[...] = jnp.dot(h_ref[...], wd_ref[0].astype(BF16), preferred_element_type=F32)


def grouped_ffn(xs, w_gate, w_up, w_down, block_e, n_active, tm):
    NP, D = xs.shape
    NB = NP // tm
    FF = w_gate.shape[-1]
    row = lambda i, be, na: (jnp.minimum(i, na[0] - 1), 0)
    wsel = lambda i, be, na: (be[i], 0, 0)
    h = pl.pallas_call(
        _ffn_up_body,
        grid_spec=pltpu.PrefetchScalarGridSpec(
            num_scalar_prefetch=2, grid=(NB,),
            in_specs=[pl.BlockSpec((tm, D), row),
                      pl.BlockSpec((1, D, FF), wsel), pl.BlockSpec((1, D, FF), wsel)],
            out_specs=pl.BlockSpec((tm, FF), row)),
        out_shape=jax.ShapeDtypeStruct((NP, FF), BF16),
        compiler_params=_params("arbitrary"),
    )(block_e, n_active, xs, w_gate, w_up)
    return pl.pallas_call(
        _ffn_down_body,
        grid_spec=pltpu.PrefetchScalarGridSpec(
            num_scalar_prefetch=2, grid=(NB,),
            in_specs=[pl.BlockSpec((tm, FF), row), pl.BlockSpec((1, FF, D), wsel)],
            out_specs=pl.BlockSpec((tm, D), row)),
        out_shape=jax.ShapeDtypeStruct((NP, D), F32),
        compiler_params=_params("arbitrary"),
    )(block_e, n_active, h, w_down)


def _route(logits, e_bias):
    T = logits.shape[0]
    scores = jax.nn.sigmoid(logits)
    biased = scores + e_bias.astype(F32)
    grp_score = lax.top_k(biased.reshape(T, N_GROUPS, EXPERTS_PER_GROUP), 2)[0].sum(-1)
    _, gidx = lax.top_k(grp_score, TOPK_GROUPS)
    gmask = (gidx[..., None] == jnp.arange(N_GROUPS)[None, None, :]).any(axis=1)
    emask = jnp.repeat(gmask, EXPERTS_PER_GROUP, axis=1)
    _, idx = lax.top_k(jnp.where(emask, biased, -jnp.inf), TOP_K)
    w = jnp.take_along_axis(scores, idx, axis=1)
    w = w / w.sum(-1, keepdims=True) * ROUTED_SCALE
    return idx, w


def _dispatch_plan(idx, tm):
    T = idx.shape[0]
    A = T * TOP_K
    NB = -(-A // tm) + N_EXPERTS
    flat_e = idx.reshape(A).astype(jnp.int32)
    order = jnp.argsort(flat_e, stable=True).astype(jnp.int32)
    counts = jnp.sum(flat_e[:, None] == jnp.arange(N_EXPERTS, dtype=jnp.int32)[None, :], axis=0, dtype=jnp.int32)
    padded = (counts + tm - 1) // tm * tm
    pad_end = jnp.cumsum(padded)
    pad_start = pad_end - padded
    start = jnp.cumsum(counts) - counts
    n_active = (pad_end[-1] // tm).astype(jnp.int32)
    blk = jnp.arange(NB, dtype=jnp.int32)
    block_e = jnp.minimum(jnp.searchsorted(pad_end, jnp.minimum(blk, n_active - 1) * tm, side='right'),
                          N_EXPERTS - 1).astype(jnp.int32)
    slot = jnp.arange(NB * tm, dtype=jnp.int32)
    slot_e = jnp.repeat(block_e, tm)
    r = slot - pad_start[slot_e]
    valid = (r < counts[slot_e]) & (slot < n_active * tm)
    src = jnp.clip(start[slot_e] + r, 0, A - 1)
    slot_token = jnp.where(valid, order[src] // TOP_K, 0)
    sorted_e = flat_e[order]
    dest_sorted = pad_start[sorted_e] + jnp.arange(A, dtype=jnp.int32) - start[sorted_e]
    dest = jnp.zeros((A,), jnp.int32).at[order].set(dest_sorted)
    return slot_token, dest, block_e, n_active.reshape(1)


def moe_block(x_parts, xb_parts, p, layer):
    T = sum(x.shape[0] for x in x_parts)
    logits = jnp.concatenate([router_logits(x, p['router_w'][layer]) for x in x_parts], axis=0)
    idx, wts = _route(logits, p['router_bias'][layer])
    slot_token, dest, block_e, n_active = _dispatch_plan(idx, MOE_TM)
    xb_all = jnp.concatenate(xb_parts, axis=0)
    xs = xb_all[slot_token]
    ys = grouped_ffn(xs, p['exp_w_gate'][layer], p['exp_w_up'][layer], p['exp_w_down'][layer],
                     block_e, n_active, MOE_TM)
    routed = jnp.einsum('tkd,tk->td', ys[dest].reshape(T, TOP_K, D_MODEL), wts)
    shared = []
    for xb in xb_parts:
        rows = xb.shape[0]
        tm = min(rows, 512)
        nb = rows // tm
        shared.append(grouped_ffn(xb, p['shared_w_gate'][layer][None], p['shared_w_up'][layer][None],
                                  p['shared_w_down'][layer][None], jnp.zeros((nb,), jnp.int32),
                                  jnp.full((1,), nb, jnp.int32), tm))
    return routed, shared


def _rope_tables(pos, half):
    inv_freq = ROPE_THETA ** (-jnp.arange(half, dtype=F32) / half)
    ang = pos.astype(F32)[:, None] * inv_freq[None, :]
    return jnp.cos(ang), jnp.sin(ang)


def _even_prompt(x_bf, B, L, p, i):
    proj = matmul(x_bf, p['w_in_even'][i].astype(BF16))
    proj3 = proj.reshape(B, L, -1)
    cos, sin = _rope_tables(jnp.arange(L, dtype=jnp.int32), RET_HEAD_DIM // 2)
    conv_out, conv_new = conv_prompt(proj3, p['conv_w'][i], p['conv_b'][i], p['conv_ln_g'][i], p['conv_ln_b'][i])
    ret_out, s_new = retention_prompt(proj3, cos, sin, p['ret_gn_g'][i], p['ret_gn_b'][i])
    mixed = jnp.concatenate([conv_out, ret_out], axis=-1).reshape(B * L, -1)
    return matmul(mixed, p['w_out_even'][i].astype(BF16)), conv_new, s_new


def _even_sample(x_bf, conv_buf, ret_state, p, i):
    B = x_bf.shape[0]
    H, d = RET_HEADS, RET_HEAD_DIM
    proj = matmul(x_bf, p['w_in_even'][i].astype(BF16))
    conv_out, conv_new = conv_sample(proj, conv_buf, p['conv_w'][i], p['conv_b'][i],
                                     p['conv_ln_g'][i], p['conv_ln_b'][i])
    o2 = 2 * CONV_CH
    q_col = proj[:, o2:o2 + RET_WIDTH].reshape(B, H, d, 1)
    k_col = proj[:, o2 + RET_WIDTH:o2 + 2 * RET_WIDTH].reshape(B, H, d, 1)
    v_row = proj[:, o2 + 2 * RET_WIDTH:o2 + 3 * RET_WIDTH].reshape(B, 1, RET_WIDTH)
    g_row = proj[:, o2 + 3 * RET_WIDTH:].reshape(B, 1, RET_WIDTH)
    cos, sin = _rope_tables(jnp.full((1,), PAST_LEN, jnp.int32), d // 2)
    ret_out, s_new = retention_sample(q_col, k_col, v_row, g_row, ret_state, cos.reshape(d // 2, 1),
                                      sin.reshape(d // 2, 1), p['ret_gn_g'][i], p['ret_gn_b'][i])
    mixed = jnp.concatenate([conv_out, ret_out.reshape(B, RET_WIDTH)], axis=-1)
    return matmul(mixed, p['w_out_even'][i].astype(BF16)), conv_new, s_new


def _att_rope_tables(pos, rows):
    cos, sin = _rope_tables(pos, ATT_HEAD_DIM // 2)
    cos = jnp.tile(cos, (rows // cos.shape[0], LANES // cos.shape[1]))
    sin = jnp.tile(sin, (rows // sin.shape[0], LANES // sin.shape[1]))
    return cos, sin


def _odd_prompt(x_bf, B, L, p, i):
    cos, sin = _att_rope_tables(jnp.arange(L, dtype=jnp.int32), L)
    qkv = matmul(x_bf, p['w_qkv_odd'][i].astype(BF16), bias=p['b_qkv_odd'][i],
                 rope=(cos, sin, Q_WIDTH + KV_WIDTH))
    qkv3 = qkv.reshape(B, L, -1)
    o = swa_prompt(qkv3, p['sinks'][i])
    out = matmul(o.reshape(B * L, Q_WIDTH), p['w_out_odd'][i].astype(BF16), bias=p['b_out_odd'][i])
    k_new = qkv3[:, L - WINDOW:, Q_WIDTH:Q_WIDTH + KV_WIDTH].reshape(B, WINDOW, N_KV_HEADS, ATT_HEAD_DIM)
    v_new = qkv3[:, L - WINDOW:, Q_WIDTH + KV_WIDTH:].reshape(B, WINDOW, N_KV_HEADS, ATT_HEAD_DIM)
    return out, k_new, v_new


def _odd_sample(x_bf, k_buf, v_buf, p, i):
    B = x_bf.shape[0]
    cos, sin = _att_rope_tables(jnp.full((1,), PAST_LEN, jnp.int32), B)
    qkv = matmul(x_bf, p['w_qkv_odd'][i].astype(BF16), bias=p['b_qkv_odd'][i],
                 rope=(cos, sin, Q_WIDTH + KV_WIDTH))
    q3 = qkv[:, :Q_WIDTH].reshape(B, N_HEADS, ATT_HEAD_DIM)
    kn3 = qkv[:, Q_WIDTH:Q_WIDTH + KV_WIDTH].reshape(B, N_KV_HEADS, ATT_HEAD_DIM)
    vn3 = qkv[:, Q_WIDTH + KV_WIDTH:].reshape(B, N_KV_HEADS, ATT_HEAD_DIM)
    o, k_new, v_new = swa_sample(q3, kn3, vn3, k_buf.reshape(B, WINDOW, KV_WIDTH),
                                 v_buf.reshape(B, WINDOW, KV_WIDTH), p['sinks'][i])
    out = matmul(o.reshape(B, Q_WIDTH), p['w_out_odd'][i].astype(BF16), bias=p['b_out_odd'][i])
    shape = (B, WINDOW, N_KV_HEADS, ATT_HEAD_DIM)
    return out, k_new.reshape(shape), v_new.reshape(shape)


def kernel(x_prompt, x_sample, state_conv, state_ret, state_win_k, state_win_v,
           w_in_even, conv_w, conv_b, conv_ln_g, conv_ln_b, ret_gn_g, ret_gn_b, w_out_even,
           w_qkv_odd, b_qkv_odd, sinks, w_out_odd, b_out_odd, ln_g, ln_b,
           router_w, router_bias, exp_w_gate, exp_w_up, exp_w_down,
           shared_w_gate, shared_w_up, shared_w_down):
    p = dict(w_in_even=w_in_even, conv_w=conv_w, conv_b=conv_b, conv_ln_g=conv_ln_g, conv_ln_b=conv_ln_b,
             ret_gn_g=ret_gn_g, ret_gn_b=ret_gn_b, w_out_even=w_out_even,
             w_qkv_odd=w_qkv_odd, b_qkv_odd=b_qkv_odd, sinks=sinks, w_out_odd=w_out_odd, b_out_odd=b_out_odd,
             ln_g=ln_g, ln_b=ln_b, router_w=router_w, router_bias=router_bias,
             exp_w_gate=exp_w_gate, exp_w_up=exp_w_up, exp_w_down=exp_w_down,
             shared_w_gate=shared_w_gate, shared_w_up=shared_w_up, shared_w_down=shared_w_down)
    B, L, D = x_prompt.shape
    Bs = x_sample.shape[0]
    Tp = B * L
    xp = x_prompt.reshape(Tp, D)
    xs = x_sample.reshape(Bs, D)
    xp_bf, xs_bf = xp.astype(BF16), xs.astype(BF16)
    conv_p, conv_s, ret_p, ret_s, wk_p, wk_s, wv_p, wv_s = [], [], [], [], [], [], [], []
    for layer in range(DEPTH):
        i = layer // 2
        if layer % 2 == 0:
            hp, c_new, r_new = _even_prompt(xp_bf, B, L, p, i)
            conv_p.append(c_new)
            ret_p.append(r_new)
            hs, c_new, r_new = _even_sample(xs_bf, state_conv[i], state_ret[i], p, i)
            conv_s.append(c_new)
            ret_s.append(r_new)
        else:
            hp, k_new, v_new = _odd_prompt(xp_bf, B, L, p, i)
            wk_p.append(k_new)
            wv_p.append(v_new)
            hs, k_new, v_new = _odd_sample(xs_bf, state_win_k[i], state_win_v[i], p, i)
            wk_s.append(k_new)
            wv_s.append(v_new)
        xp, xp_bf = deepnorm(xp, [(hp, 0)], ln_g[layer, 0], ln_b[layer, 0])
        xs, xs_bf = deepnorm(xs, [(hs, 0)], ln_g[layer, 0], ln_b[layer, 0])
        routed, shared = moe_block([xp, xs], [xp_bf, xs_bf], p, layer)
        xp, xp_bf = deepnorm(xp, [(routed, 0), (shared[0], 0)], ln_g[layer, 1], ln_b[layer, 1])
        xs, xs_bf = deepnorm(xs, [(routed, Tp), (shared[1], 0)], ln_g[layer, 1], ln_b[layer, 1])
    return (xp.reshape(B, L, D), xs.reshape(Bs, 1, D),
            jnp.stack(conv_p), jnp.stack(conv_s), jnp.stack(ret_p), jnp.stack(ret_s),
            jnp.stack(wk_p), jnp.stack(wk_s), jnp.stack(wv_p), jnp.stack(wv_s))
```

```python
import functools

import jax
import jax.numpy as jnp
import numpy as np
from jax import lax
from jax.experimental import pallas as pl
from jax.experimental.pallas import tpu as pltpu

F32 = jnp.float32
BF16 = jnp.bfloat16

D_MODEL = 4096
DEPTH = 2
PAST_LEN = 16384
CONV_CH = D_MODEL // 2
CONV_WIDTH = 31
CONV_HALO = 32
RET_HEADS = 8
RET_HEAD_DIM = 256
RET_WIDTH = RET_HEADS * RET_HEAD_DIM
RET_CHUNK = 128
ATT_HEAD_DIM = 64
N_HEADS = 64
N_KV_HEADS = 8
GROUP = N_HEADS // N_KV_HEADS
WINDOW = 128
Q_WIDTH = N_HEADS * ATT_HEAD_DIM
KV_WIDTH = N_KV_HEADS * ATT_HEAD_DIM
ROPE_THETA = 10000.0
N_EXPERTS = 128
TOP_K = 8
N_GROUPS = 8
TOPK_GROUPS = 4
EXPERTS_PER_GROUP = N_EXPERTS // N_GROUPS
EXPERT_FF = 512
ROUTED_SCALE = 2.5
DN_ALPHA = (2.0 * DEPTH) ** 0.25
LN_EPS = 1e-5

LANES = 128
V7X_VMEM_LIMIT = 52 * 1024 * 1024
V7X_VMEM_ROUTED = 58 * 1024 * 1024
MOE_TM = 256


def _params(*sem):
    return pltpu.CompilerParams(dimension_semantics=sem, vmem_limit_bytes=V7X_VMEM_LIMIT)


def _silu(x):
    return x * jax.nn.sigmoid(x)


def _rot_half_64(x):
    n = x.shape[-1]
    lane = lax.broadcasted_iota(jnp.int32, x.shape, x.ndim - 1)
    fwd = pltpu.roll(x, n - ATT_HEAD_DIM // 2, x.ndim - 1)
    bwd = pltpu.roll(x, ATT_HEAD_DIM // 2, x.ndim - 1)
    return jnp.where(lane % ATT_HEAD_DIM < ATT_HEAD_DIM // 2, -fwd, bwd)


def _mm_body(*refs, has_bias, rope_tiles):
    x_ref, w_ref = refs[0], refs[1]
    pos = 2
    b_ref = None
    if has_bias:
        b_ref = refs[pos]
        pos += 1
    if rope_tiles:
        cos_ref, sin_ref = refs[pos], refs[pos + 1]
        pos += 2
    o_ref = refs[pos]
    acc = jnp.dot(x_ref[...].astype(BF16), w_ref[...].astype(BF16), preferred_element_type=F32)
    if has_bias:
        acc = acc + b_ref[...]
    if rope_tiles:
        reps = acc.shape[1] // LANES
        cos = jnp.concatenate([cos_ref[...]] * reps, axis=1)
        sin = jnp.concatenate([sin_ref[...]] * reps, axis=1)
        @pl.when(pl.program_id(1) < rope_tiles)
        def _():
            o_ref[...] = (acc * cos + _rot_half_64(acc) * sin).astype(o_ref.dtype)

        @pl.when(pl.program_id(1) >= rope_tiles)
        def _():
            o_ref[...] = acc.astype(o_ref.dtype)
    else:
        o_ref[...] = acc.astype(o_ref.dtype)


def matmul(x, w, bias=None, rope=None, out_dtype=F32, tm=1024, tn=512, name="matmul"):
    M, K = x.shape
    N = w.shape[1]
    tm = min(tm, M)
    tn = min(tn, N)
    assert M % tm == 0 and N % tn == 0
    in_specs = [pl.BlockSpec((tm, K), lambda i, j: (i, 0)),
                pl.BlockSpec((K, tn), lambda i, j: (0, j))]
    args = [x, w]
    if bias is not None:
        in_specs.append(pl.BlockSpec((1, tn), lambda i, j: (0, j)))
        args.append(bias.reshape(1, N))
    rope_tiles = 0
    if rope is not None:
        cos, sin, n_cols = rope
        assert n_cols % tn == 0 and cos.shape[0] % tm == 0
        rope_tiles = n_cols // tn
        nrb = cos.shape[0] // tm
        in_specs += [pl.BlockSpec((tm, LANES), lambda i, j: (i % nrb, 0))] * 2
        args += [cos, sin]
    return pl.pallas_call(
        functools.partial(_mm_body, has_bias=bias is not None, rope_tiles=rope_tiles),
        grid=(M // tm, N // tn),
        in_specs=in_specs,
        out_specs=pl.BlockSpec((tm, tn), lambda i, j: (i, j)),
        out_shape=jax.ShapeDtypeStruct((M, N), out_dtype),
        compiler_params=_params("parallel", "arbitrary"),
        name=name,
    )(*args)


def _split_bf16(x):
    hi = x.astype(BF16)
    lo = (x - hi.astype(F32)).astype(BF16)
    return hi, lo


def _router_body(x_ref, w_ref, o_ref):
    xh, xl = _split_bf16(x_ref[...])
    wh, wl = _split_bf16(w_ref[0])
    acc = jnp.dot(xh, wh, preferred_element_type=F32)
    acc = acc + jnp.dot(xl, wh, preferred_element_type=F32)
    acc = acc + jnp.dot(xh, wl, preferred_element_type=F32)
    o_ref[...] = acc


def router_logits(x, w, layer, tm=512):
    M, K = x.shape
    N = w.shape[-1]
    tm = min(tm, M)
    return pl.pallas_call(
        _router_body,
        grid=(M // tm,),
        in_specs=[pl.BlockSpec((tm, K), lambda i: (i, 0)), pl.BlockSpec((1, K, N), lambda i: (layer, 0, 0))],
        out_specs=pl.BlockSpec((tm, N), lambda i: (i, 0)),
        out_shape=jax.ShapeDtypeStruct((M, N), F32),
        compiler_params=_params("parallel"),
        name="router",
    )(x, w)


def _layer_norm_rows(z, g, b):
    mu = jnp.mean(z, axis=-1, keepdims=True)
    zc = z - mu
    var = jnp.mean(zc * zc, axis=-1, keepdims=True)
    return zc * lax.rsqrt(var + LN_EPS) * g + b


def _unpack_pair(w):
    lo = lax.bitcast_convert_type(w << 16, F32)
    hi = lax.bitcast_convert_type(w & jnp.uint32(0xFFFF0000), F32)
    return lo, hi


def _pack_pair(lo, hi):
    lo_bits = lax.bitcast_convert_type(lo.astype(BF16).astype(F32), jnp.uint32) >> 16
    hi_bits = lax.bitcast_convert_type(hi.astype(BF16).astype(F32), jnp.uint32) & jnp.uint32(0xFFFF0000)
    return hi_bits | lo_bits


def _ln_body(*refs, n_h, combine):
    x_ref = refs[0]
    h_refs = refs[1:1 + n_h]
    pos = 1 + n_h
    if combine:
        y_ref, w_ref = refs[pos], refs[pos + 1]
        pos += 2
    g_ref, b_ref, o_ref, ob_ref, op_ref = refs[pos:]
    half = x_ref.shape[1] // 2
    z = DN_ALPHA * x_ref[...]
    for h_ref in h_refs:
        z = z + h_ref[...]
    if combine:
        zl, zh = z[:, :half], z[:, half:]
        for k in range(TOP_K):
            lo, hi = _unpack_pair(y_ref[:, k * half:(k + 1) * half])
            wk = w_ref[:, k:k + 1]
            zl = zl + wk * lo
            zh = zh + wk * hi
        z = jnp.concatenate([zl, zh], axis=1)
    y = _layer_norm_rows(z, g_ref[...], b_ref[...])
    o_ref[...] = y
    ob_ref[...] = y.astype(BF16)
    op_ref[...] = _pack_pair(y[:, :half], y[:, half:])


def deepnorm(x, hs, g, b, combine=None, tm=256, name="deepnorm"):
    M, D = x.shape
    tm = min(tm, M)
    assert M % tm == 0
    row_spec = lambda width, off: pl.BlockSpec((tm, width), functools.partial(lambda i, o: (i + o, 0), o=off // tm))
    in_specs = [row_spec(D, 0)]
    args = [x]
    for h, off in hs:
        assert off % tm == 0
        in_specs.append(row_spec(D, off))
        args.append(h)
    if combine is not None:
        y, wts, off = combine
        assert off % tm == 0
        in_specs += [row_spec(y.shape[1], off), row_spec(TOP_K, off)]
        args += [y, wts]
    in_specs += [pl.BlockSpec((1, D), lambda i: (0, 0))] * 2
    args += [g.reshape(1, D), b.reshape(1, D)]
    return pl.pallas_call(
        functools.partial(_ln_body, n_h=len(hs), combine=combine is not None),
        grid=(M // tm,),
        in_specs=in_specs,
        out_specs=[row_spec(D, 0), row_spec(D, 0), row_spec(D // 2, 0)],
        out_shape=[jax.ShapeDtypeStruct((M, D), F32), jax.ShapeDtypeStruct((M, D), BF16),
                   jax.ShapeDtypeStruct((M, D // 2), jnp.uint32)],
        compiler_params=_params("parallel"),
        name=name,
    )(*args)


def _conv_prompt_body(a_ref, g_ref, pa_ref, pg_ref, w_ref, bdw_ref, lng_ref, lnb_ref,
                      o_ref, st_ref, full_ref, y_ref, *, tl):
    i = pl.program_id(1)
    u = a_ref[0] * jax.nn.sigmoid(g_ref[0])
    up = pa_ref[0] * jax.nn.sigmoid(pg_ref[0])
    up = jnp.where(i > 0, up, 0.0)
    full_ref[0:CONV_HALO, :] = up
    full_ref[CONV_HALO:CONV_HALO + tl, :] = u
    base = CONV_HALO - (CONV_WIDTH - 1)
    n_col = u.shape[1] // LANES

    def col_step(c, carry):
        off = pl.multiple_of(c * LANES, LANES)
        acc = jnp.zeros((tl, LANES), F32)
        for j in range(CONV_WIDTH):
            acc = acc + w_ref[j:j + 1, pl.ds(off, LANES)] * full_ref[base + j:base + j + tl, pl.ds(off, LANES)]
        y_ref[:, pl.ds(off, LANES)] = acc
        return carry

    lax.fori_loop(0, n_col, col_step, 0)
    y = _layer_norm_rows(y_ref[...] + bdw_ref[...], lng_ref[...], lnb_ref[...])
    o_ref[0] = _silu(y).astype(o_ref.dtype)

    @pl.when(i == pl.num_programs(1) - 1)
    def _():
        st_ref[0] = full_ref[CONV_HALO + tl - (CONV_WIDTH - 1):CONV_HALO + tl, :]


def conv_prompt(proj3, w_dw, b_dw, ln_g, ln_b, tl=128):
    B, L, _ = proj3.shape
    C = CONV_CH
    nh = tl // CONV_HALO
    vec = lambda v: v.reshape(1, C)
    return pl.pallas_call(
        functools.partial(_conv_prompt_body, tl=tl),
        grid=(B, L // tl),
        in_specs=[
            pl.BlockSpec((1, tl, C), lambda b, i: (b, i, 0)),
            pl.BlockSpec((1, tl, C), lambda b, i: (b, i, 1)),
            pl.BlockSpec((1, CONV_HALO, C), lambda b, i: (b, jnp.maximum(i * nh - 1, 0), 0)),
            pl.BlockSpec((1, CONV_HALO, C), lambda b, i: (b, jnp.maximum(i * nh - 1, 0), 1)),
            pl.BlockSpec((CONV_WIDTH, C), lambda b, i: (0, 0)),
            pl.BlockSpec((1, C), lambda b, i: (0, 0)),
            pl.BlockSpec((1, C), lambda b, i: (0, 0)),
            pl.BlockSpec((1, C), lambda b, i: (0, 0)),
        ],
        out_specs=[
            pl.BlockSpec((1, tl, C), lambda b, i: (b, i, 0)),
            pl.BlockSpec((1, CONV_WIDTH - 1, C), lambda b, i: (b, 0, 0)),
        ],
        out_shape=[jax.ShapeDtypeStruct((B, L, C), BF16),
                   jax.ShapeDtypeStruct((B, CONV_WIDTH - 1, C), F32)],
        scratch_shapes=[pltpu.VMEM((CONV_HALO + tl, C), F32), pltpu.VMEM((tl, C), F32)],
        compiler_params=_params("parallel", "arbitrary"),
        name="conv_prompt",
    )(proj3, proj3, proj3, proj3, w_dw, vec(b_dw), vec(ln_g), vec(ln_b))


def _conv_sample_body(a_ref, g_ref, buf_ref, w_ref, bdw_ref, lng_ref, lnb_ref, o_ref, st_ref):
    u = a_ref[...] * jax.nn.sigmoid(g_ref[...])
    nb = CONV_WIDTH - 1
    acc = w_ref[nb:nb + 1, :] * u
    for j in range(nb):
        acc = acc + w_ref[j:j + 1, :] * buf_ref[:, j, :]
    y = _layer_norm_rows(acc + bdw_ref[...], lng_ref[...], lnb_ref[...])
    o_ref[...] = _silu(y).astype(o_ref.dtype)
    st_ref[:, 0:nb - 1, :] = buf_ref[:, 1:nb, :]
    st_ref[:, nb - 1, :] = u


def conv_sample(proj, buf, w_dw, b_dw, ln_g, ln_b, tb=8):
    B = proj.shape[0]
    C = CONV_CH
    nb = CONV_WIDTH - 1
    vec = lambda v: v.reshape(1, C)
    return pl.pallas_call(
        _conv_sample_body,
        grid=(B // tb,),
        in_specs=[
            pl.BlockSpec((tb, C), lambda b: (b, 0)),
            pl.BlockSpec((tb, C), lambda b: (b, 1)),
            pl.BlockSpec((tb, nb, C), lambda b: (b, 0, 0)),
            pl.BlockSpec((CONV_WIDTH, C), lambda b: (0, 0)),
            pl.BlockSpec((1, C), lambda b: (0, 0)),
            pl.BlockSpec((1, C), lambda b: (0, 0)),
            pl.BlockSpec((1, C), lambda b: (0, 0)),
        ],
        out_specs=[pl.BlockSpec((tb, C), lambda b: (b, 0)),
                   pl.BlockSpec((tb, nb, C), lambda b: (b, 0, 0))],
        out_shape=[jax.ShapeDtypeStruct((B, C), BF16), jax.ShapeDtypeStruct((B, nb, C), F32)],
        compiler_params=_params("parallel"),
        name="conv_sample",
    )(proj, proj, buf, w_dw, vec(b_dw), vec(ln_g), vec(ln_b))


def _ret_log_gamma():
    return np.log1p(-np.exp2(-5.0 - np.arange(RET_HEADS, dtype=np.float64)))


def _group_norm_gate(o, gate, gn_g, gn_b):
    mu = jnp.mean(o, axis=-1, keepdims=True)
    oc = o - mu
    var = jnp.mean(oc * oc, axis=-1, keepdims=True)
    return (oc * lax.rsqrt(var + LN_EPS) * gn_g + gn_b) * _silu(gate)


def _rope_256(x, cos, sin):
    half = RET_HEAD_DIM // 2
    x1, x2 = x[..., :half], x[..., half:]
    return jnp.concatenate([x1 * cos - x2 * sin, x2 * cos + x1 * sin], axis=-1)


def _ret_prompt_body(q_ref, k_ref, v_ref, g_ref, cos_ref, sin_ref, intra_ref, qd_ref, kd_ref, cd_ref,
                     gng_ref, gnb_ref, o_ref, s_ref):
    c = pl.program_id(2)

    @pl.when(c == 0)
    def _():
        s_ref[...] = jnp.zeros_like(s_ref)

    cos, sin = cos_ref[...], sin_ref[...]
    q = _rope_256(q_ref[0], cos, sin)
    k = _rope_256(k_ref[0], cos, sin) * (RET_HEAD_DIM ** -0.5)
    vb = v_ref[0].astype(BF16)
    s_prev = s_ref[0, 0]
    scores = lax.dot_general(q.astype(BF16), k.astype(BF16), (((1,), (1,)), ((), ())),
                             preferred_element_type=F32) * intra_ref[0]
    o = jnp.dot(scores.astype(BF16), vb, preferred_element_type=F32)
    o = o + jnp.dot((q * qd_ref[0]).astype(BF16), s_prev.astype(BF16), preferred_element_type=F32)
    kdt = (k * kd_ref[0]).T.astype(BF16)
    kv = jnp.dot(kdt, vb, preferred_element_type=F32)
    s_ref[0, 0] = cd_ref[0] * s_prev + kv
    o_ref[0] = _group_norm_gate(o, g_ref[0], gng_ref[...], gnb_ref[...]).astype(o_ref.dtype)


def retention_prompt(proj3, cos, sin, gn_g, gn_b):
    B, L, _ = proj3.shape
    H, d, c = RET_HEADS, RET_HEAD_DIM, RET_CHUNK
    lg = _ret_log_gamma()
    idx = np.arange(c, dtype=np.float64)
    rel = idx[:, None] - idx[None, :]
    intra = np.where(rel[None] >= 0, np.exp(lg[:, None, None] * np.maximum(rel, 0.0)[None]), 0.0)
    ones = np.ones((1, 1, d))
    qd = np.exp(lg[:, None] * (idx + 1.0)[None, :])[:, :, None] * ones
    kd = np.exp(lg[:, None] * (c - 1.0 - idx)[None, :])[:, :, None] * ones
    cd = np.exp(lg * c)[:, None, None] * ones
    col0 = 2 * CONV_CH // d
    head_spec = lambda off: pl.BlockSpec((1, c, d), lambda b, h, n: (b, n, col0 + off * H + h))
    tab_spec = pl.BlockSpec((1, c, d), lambda b, h, n: (h, 0, 0))
    return pl.pallas_call(
        _ret_prompt_body,
        grid=(B, H, L // c),
        in_specs=[
            head_spec(0), head_spec(1), head_spec(2), head_spec(3),
            pl.BlockSpec((c, d // 2), lambda b, h, n: (n, 0)),
            pl.BlockSpec((c, d // 2), lambda b, h, n: (n, 0)),
            pl.BlockSpec((1, c, c), lambda b, h, n: (h, 0, 0)),
            tab_spec, tab_spec,
            pl.BlockSpec((1, 1, d), lambda b, h, n: (h, 0, 0)),
            pl.BlockSpec((1, d), lambda b, h, n: (0, h)),
            pl.BlockSpec((1, d), lambda b, h, n: (0, h)),
        ],
        out_specs=[
            pl.BlockSpec((1, c, d), lambda b, h, n: (b, n, h)),
            pl.BlockSpec((1, 1, d, d), lambda b, h, n: (b, h, 0, 0)),
        ],
        out_shape=[jax.ShapeDtypeStruct((B, L, H * d), BF16),
                   jax.ShapeDtypeStruct((B, H, d, d), F32)],
        compiler_params=_params("parallel", "parallel", "arbitrary"),
        name="retention_prompt",
    )(proj3, proj3, proj3, proj3, cos, sin,
      jnp.asarray(intra, F32), jnp.asarray(qd, F32), jnp.asarray(kd, F32), jnp.asarray(cd, F32),
      gn_g.reshape(1, H * d), gn_b.reshape(1, H * d))


def _ret_sample_body(qc_ref, kc_ref, v_ref, g_ref, s_ref, cosc_ref, sinc_ref, gam_ref, gng_ref, gnb_ref,
                     o_ref, so_ref):
    half = RET_HEAD_DIM // 2
    cos, sin = cosc_ref[...], sinc_ref[...]

    def rope_col(x):
        x1, x2 = x[:half], x[half:]
        return jnp.concatenate([x1 * cos - x2 * sin, x2 * cos + x1 * sin], axis=0)

    for h in range(RET_HEADS):
        lo, hi = h * RET_HEAD_DIM, (h + 1) * RET_HEAD_DIM
        q = rope_col(qc_ref[0, h])
        k = rope_col(kc_ref[0, h]) * (RET_HEAD_DIM ** -0.5)
        v = v_ref[0, :, lo:hi]
        s_new = gam_ref[h] * s_ref[0, h] + k * v
        so_ref[0, h] = s_new
        o = jnp.sum(q * s_new, axis=0, keepdims=True)
        o_ref[0, :, lo:hi] = _group_norm_gate(o, g_ref[0, :, lo:hi], gng_ref[:, lo:hi],
                                              gnb_ref[:, lo:hi]).astype(o_ref.dtype)


def retention_sample(q_col, k_col, v_row, g_row, state, cos_col, sin_col, gn_g, gn_b):
    B = state.shape[0]
    H, d = RET_HEADS, RET_HEAD_DIM
    gam = jnp.asarray(np.exp(_ret_log_gamma()), F32)
    col_spec = pl.BlockSpec((1, H, d, 1), lambda b: (b, 0, 0, 0))
    row_spec = pl.BlockSpec((1, 1, H * d), lambda b: (b, 0, 0))
    return pl.pallas_call(
        _ret_sample_body,
        grid=(B,),
        in_specs=[
            col_spec, col_spec, row_spec, row_spec,
            pl.BlockSpec((1, H, d, d), lambda b: (b, 0, 0, 0)),
            pl.BlockSpec((d // 2, 1), lambda b: (0, 0)),
            pl.BlockSpec((d // 2, 1), lambda b: (0, 0)),
            pl.BlockSpec(memory_space=pltpu.SMEM),
            pl.BlockSpec((1, H * d), lambda b: (0, 0)),
            pl.BlockSpec((1, H * d), lambda b: (0, 0)),
        ],
        out_specs=[row_spec, pl.BlockSpec((1, H, d, d), lambda b: (b, 0, 0, 0))],
        out_shape=[jax.ShapeDtypeStruct((B, 1, H * d), BF16), jax.ShapeDtypeStruct((B, H, d, d), F32)],
        compiler_params=_params("parallel"),
        name="retention_sample",
    )(q_col, k_col, v_row, g_row, state, cos_col, sin_col, gam, gn_g.reshape(1, H * d), gn_b.reshape(1, H * d))


def _pair_operand(x2, h, lane):
    col = x2[:, (h // 2) * LANES:(h // 2 + 1) * LANES]
    swapped = pltpu.roll(col, ATT_HEAD_DIM, 1)
    first, second = (col, swapped) if h % 2 == 0 else (swapped, col)
    return jnp.concatenate([jnp.where(lane < ATT_HEAD_DIM, first, 0.0),
                            jnp.where(lane >= ATT_HEAD_DIM, second, 0.0)], axis=0).astype(BF16)


def _swa_prompt_body(sink_ref, q_ref, kc_ref, kp_ref, vc_ref, vp_ref, o_ref):
    j = pl.program_id(1)
    W = WINDOW
    k2 = jnp.concatenate([kp_ref[0], kc_ref[0]], axis=0)
    v2 = jnp.concatenate([vp_ref[0], vc_ref[0]], axis=0)
    row = lax.broadcasted_iota(jnp.int32, (W, 2 * W), 0)
    col = lax.broadcasted_iota(jnp.int32, (W, 2 * W), 1)
    first_key = jnp.where(j > 0, 0, W)
    mask = (col >= jnp.maximum(row, first_key)) & (col <= W + row)
    lane_kv = lax.broadcasted_iota(jnp.int32, (2 * W, LANES), 1)
    lane_o = lax.broadcasted_iota(jnp.int32, (W, LANES), 1)
    scale = ATT_HEAD_DIM ** -0.5
    for h in range(N_KV_HEADS):
        kk = _pair_operand(k2, h, lane_kv)
        vv = _pair_operand(v2, h, lane_kv)
        for p in range(GROUP // 2):
            c0 = h * GROUP * ATT_HEAD_DIM + p * LANES
            qp = q_ref[0, :, c0:c0 + LANES].astype(BF16)
            s = lax.dot_general(qp, kk, (((1,), (1,)), ((), ())), preferred_element_type=F32) * scale
            probs, inv = [], []
            for t in range(2):
                sink = sink_ref[h * GROUP + 2 * p + t]
                st = jnp.where(mask, s[:, t * 2 * W:(t + 1) * 2 * W], -jnp.inf)
                m = jnp.maximum(jnp.max(st, axis=-1, keepdims=True), sink)
                e = jnp.exp(st - m)
                denom = jnp.sum(e, axis=-1, keepdims=True) + jnp.exp(sink - m)
                probs.append(e.astype(BF16))
                inv.append(1.0 / denom)
            o2 = jnp.dot(jnp.concatenate(probs, axis=1), vv, preferred_element_type=F32)
            o2 = o2 * jnp.where(lane_o < ATT_HEAD_DIM, inv[0], inv[1])
            o_ref[0, :, c0:c0 + LANES] = o2.astype(o_ref.dtype)


def swa_prompt(qkv3, sinks):
    B, L, _ = qkv3.shape
    W = WINDOW
    kcol = Q_WIDTH // KV_WIDTH
    return pl.pallas_call(
        _swa_prompt_body,
        grid_spec=pltpu.PrefetchScalarGridSpec(
            num_scalar_prefetch=0,
            grid=(B, L // W),
            in_specs=[
                pl.BlockSpec(memory_space=pltpu.SMEM),
                pl.BlockSpec((1, W, Q_WIDTH), lambda b, j: (b, j, 0)),
                pl.BlockSpec((1, W, KV_WIDTH), lambda b, j: (b, j, kcol)),
                pl.BlockSpec((1, W, KV_WIDTH), lambda b, j: (b, jnp.maximum(j - 1, 0), kcol)),
                pl.BlockSpec((1, W, KV_WIDTH), lambda b, j: (b, j, kcol + 1)),
                pl.BlockSpec((1, W, KV_WIDTH), lambda b, j: (b, jnp.maximum(j - 1, 0), kcol + 1)),
            ],
            out_specs=pl.BlockSpec((1, W, Q_WIDTH), lambda b, j: (b, j, 0)),
        ),
        out_shape=jax.ShapeDtypeStruct((B, L, Q_WIDTH), BF16),
        compiler_params=_params("parallel", "parallel"),
        name="swa_prompt",
    )(sinks, qkv3, qkv3, qkv3, qkv3, qkv3)


def _swa_sample_body(q_ref, kn_ref, vn_ref, knf_ref, vnf_ref, kb_ref, vb_ref, sink_ref, o_ref, ko_ref, vo_ref):
    W = WINDOW
    scale = ATT_HEAD_DIM ** -0.5
    kb = kb_ref[0]
    vb = vb_ref[0]
    for h in range(N_KV_HEADS):
        lo, hi = h * ATT_HEAD_DIM, (h + 1) * ATT_HEAD_DIM
        qh = q_ref[0, h * GROUP:(h + 1) * GROUP, :]
        kn = kn_ref[0, h:h + 1, :]
        vn = vn_ref[0, h:h + 1, :]
        sink = sink_ref[h * GROUP:(h + 1) * GROUP, :]
        s_buf = lax.dot_general(qh.astype(BF16), kb[:, lo:hi].astype(BF16), (((1,), (1,)), ((), ())),
                                preferred_element_type=F32) * scale
        s_new = jnp.sum(qh * kn, axis=-1, keepdims=True) * scale
        m = jnp.maximum(jnp.maximum(jnp.max(s_buf, axis=-1, keepdims=True), s_new), sink)
        e_buf = jnp.exp(s_buf - m)
        e_new = jnp.exp(s_new - m)
        denom = jnp.sum(e_buf, axis=-1, keepdims=True) + e_new + jnp.exp(sink - m)
        o = jnp.dot(e_buf.astype(BF16), vb[:, lo:hi].astype(BF16), preferred_element_type=F32)
        o = o + e_new * vn
        o_ref[0, h * GROUP:(h + 1) * GROUP, :] = (o / denom).astype(o_ref.dtype)
    ko_ref[0, 0:W - 1, :] = kb[1:W, :]
    ko_ref[0, W - 1:W, :] = knf_ref[0]
    vo_ref[0, 0:W - 1, :] = vb[1:W, :]
    vo_ref[0, W - 1:W, :] = vnf_ref[0]


def swa_sample(q3, kn3, vn3, k_buf, v_buf, sinks):
    B = q3.shape[0]
    W, d = WINDOW, ATT_HEAD_DIM
    head_spec = lambda n: pl.BlockSpec((1, n, d), lambda b: (b, 0, 0))
    flat_spec = pl.BlockSpec((1, 1, KV_WIDTH), lambda b: (b, 0, 0))
    buf_spec = pl.BlockSpec((1, W, KV_WIDTH), lambda b: (b, 0, 0))
    return pl.pallas_call(
        _swa_sample_body,
        grid=(B,),
        in_specs=[head_spec(N_HEADS), head_spec(N_KV_HEADS), head_spec(N_KV_HEADS), flat_spec, flat_spec,
                  buf_spec, buf_spec, pl.BlockSpec((N_HEADS, 1), lambda b: (0, 0))],
        out_specs=[head_spec(N_HEADS), buf_spec, buf_spec],
        out_shape=[jax.ShapeDtypeStruct((B, N_HEADS, d), BF16),
                   jax.ShapeDtypeStruct((B, W, KV_WIDTH), F32),
                   jax.ShapeDtypeStruct((B, W, KV_WIDTH), F32)],
        compiler_params=_params("parallel"),
        name="swa_sample",
    )(q3, kn3, vn3, kn3.reshape(B, 1, KV_WIDTH), vn3.reshape(B, 1, KV_WIDTH), k_buf, v_buf,
      sinks.reshape(N_HEADS, 1))


def _shared_up_body(x_ref, wg_ref, wu_ref, h_ref):
    x = x_ref[...]
    g = jnp.dot(x, wg_ref[0].astype(BF16), preferred_element_type=F32)
    u = jnp.dot(x, wu_ref[0].astype(BF16), preferred_element_type=F32)
    h_ref[...] = (_silu(g) * u).astype(h_ref.dtype)


def _shared_down_body(h_ref, wd_ref, y_ref):
    y_ref[...] = jnp.dot(h_ref[...], wd_ref[0].astype(BF16), preferred_element_type=F32)


def shared_ffn(xb, w_gate, w_up, w_down, layer, tm=512):
    M, D = xb.shape
    FF = w_gate.shape[-1]
    tm = min(tm, M)
    wsel = lambda i: (layer, 0, 0)
    h = pl.pallas_call(
        _shared_up_body,
        grid=(M // tm,),
        in_specs=[pl.BlockSpec((tm, D), lambda i: (i, 0)),
                  pl.BlockSpec((1, D, FF), wsel), pl.BlockSpec((1, D, FF), wsel)],
        out_specs=pl.BlockSpec((tm, FF), lambda i: (i, 0)),
        out_shape=jax.ShapeDtypeStruct((M, FF), BF16),
        compiler_params=_params("parallel"),
        name="shared_up",
    )(xb, w_gate, w_up)
    return pl.pallas_call(
        _shared_down_body,
        grid=(M // tm,),
        in_specs=[pl.BlockSpec((tm, FF), lambda i: (i, 0)), pl.BlockSpec((1, FF, D), wsel)],
        out_specs=pl.BlockSpec((tm, D), lambda i: (i, 0)),
        out_shape=jax.ShapeDtypeStruct((M, D), F32),
        compiler_params=_params("parallel"),
        name="shared_down",
    )(h, w_down)


def _expert_changed(be_ref, i):
    return (i == 0) | (be_ref[i] != be_ref[jnp.maximum(i - 1, 0)])


def _moe_up_body(be_ref, na_ref, tok_ref, x_hbm, wg_ref, wu_ref, h_ref, xbuf, sem, wg_bf, wu_bf, *, tm):
    i = pl.program_id(0)
    na = na_ref[0]
    half = wg_bf.shape[0] // 2

    def issue(blk, slot):
        base = blk * tm
        for r in range(tm):
            tok = tok_ref[base + r]
            pltpu.make_async_copy(x_hbm.at[pl.ds(tok, 1)], xbuf.at[slot, pl.ds(r, 1)], sem.at[slot]).start()

    @pl.when(i == 0)
    def _():
        issue(0, 0)

    @pl.when(i >= na)
    def _():
        h_ref[...] = jnp.zeros_like(h_ref)

    @pl.when(i < na)
    def _():
        slot = i % 2

        @pl.when(i + 1 < na)
        def _():
            issue(i + 1, 1 - slot)

        @pl.when(_expert_changed(be_ref, i))
        def _():
            wg_bf[...] = wg_ref[0, 0].astype(BF16)
            wu_bf[...] = wu_ref[0, 0].astype(BF16)

        pltpu.make_async_copy(x_hbm.at[pl.ds(0, tm)], xbuf.at[slot], sem.at[slot]).wait()
        lo, hi = _unpack_pair(xbuf[slot])
        lo, hi = lo.astype(BF16), hi.astype(BF16)
        g = (jnp.dot(lo, wg_bf[0:half], preferred_element_type=F32)
             + jnp.dot(hi, wg_bf[half:], preferred_element_type=F32))
        u = (jnp.dot(lo, wu_bf[0:half], preferred_element_type=F32)
             + jnp.dot(hi, wu_bf[half:], preferred_element_type=F32))
        h_ref[...] = (_silu(g) * u).astype(h_ref.dtype)


def _moe_down_body(be_ref, na_ref, dst_ref, h_ref, wd_ref, y_hbm, ybuf, sem, wd_bf, *, tm):
    i = pl.program_id(0)
    na = na_ref[0]
    half = wd_bf.shape[1] // 2

    def wait(slot):
        pltpu.make_async_copy(ybuf.at[slot], y_hbm.at[pl.ds(0, tm)], sem.at[slot]).wait()

    @pl.when(i == 0)
    def _():
        spare0 = y_hbm.shape[0] - 2 * tm
        ybuf[0] = jnp.zeros(ybuf.shape[1:], ybuf.dtype)
        for s in range(2):
            fill = pltpu.make_async_copy(ybuf.at[0], y_hbm.at[pl.ds(spare0 + s * tm, tm)], sem.at[s])
            fill.start()
            fill.wait()

    @pl.when(i < na)
    def _():
        slot = i % 2

        @pl.when(_expert_changed(be_ref, i))
        def _():
            wd_bf[...] = wd_ref[0, 0].astype(BF16)

        y = jnp.dot(h_ref[...], wd_bf[...], preferred_element_type=F32)

        @pl.when(i >= 2)
        def _():
            wait(slot)

        ybuf[slot] = _pack_pair(y[:, :half], y[:, half:])
        base = i * tm
        for r in range(tm):
            dst = dst_ref[base + r]
            pltpu.make_async_copy(ybuf.at[slot, pl.ds(r, 1)], y_hbm.at[pl.ds(dst, 1)], sem.at[slot]).start()

    @pl.when(i == pl.num_programs(0) - 1)
    def _():
        wait((na - 1) % 2)

        @pl.when(na >= 2)
        def _():
            wait(na % 2)


def routed_ffn(x_packed, w_gate, w_up, w_down, layer, plan, tm):
    slot_token, slot_dest, block_e, n_active = plan
    T, DH = x_packed.shape
    NB = block_e.shape[0]
    D, FF = w_gate.shape[-2], w_gate.shape[-1]
    A = T * TOP_K
    row = lambda i, be, na, ix: (jnp.minimum(i, na[0] - 1), 0)
    wsel = lambda i, be, na, ix: (layer, be[i], 0, 0)
    params = pltpu.CompilerParams(dimension_semantics=("arbitrary",), vmem_limit_bytes=V7X_VMEM_ROUTED)
    h = pl.pallas_call(
        functools.partial(_moe_up_body, tm=tm),
        grid_spec=pltpu.PrefetchScalarGridSpec(
            num_scalar_prefetch=3, grid=(NB,),
            in_specs=[pl.BlockSpec(memory_space=pl.ANY),
                      pl.BlockSpec((1, 1, D, FF), wsel), pl.BlockSpec((1, 1, D, FF), wsel)],
            out_specs=pl.BlockSpec((tm, FF), lambda i, be, na, ix: (i, 0)),
            scratch_shapes=[pltpu.VMEM((2, tm, DH), jnp.uint32), pltpu.SemaphoreType.DMA((2,)),
                            pltpu.VMEM((D, FF), BF16), pltpu.VMEM((D, FF), BF16)]),
        out_shape=jax.ShapeDtypeStruct((NB * tm, FF), BF16),
        compiler_params=params,
        name="moe_up",
    )(block_e, n_active, slot_token, x_packed, w_gate, w_up)
    return pl.pallas_call(
        functools.partial(_moe_down_body, tm=tm),
        grid_spec=pltpu.PrefetchScalarGridSpec(
            num_scalar_prefetch=3, grid=(NB,),
            in_specs=[pl.BlockSpec((tm, FF), row), pl.BlockSpec((1, 1, FF, D), wsel)],
            out_specs=pl.BlockSpec(memory_space=pl.ANY),
            scratch_shapes=[pltpu.VMEM((2, tm, DH), jnp.uint32), pltpu.SemaphoreType.DMA((2,)),
                            pltpu.VMEM((FF, D), BF16)]),
        out_shape=jax.ShapeDtypeStruct((A + 2 * tm, DH), jnp.uint32),
        compiler_params=params,
        name="moe_down",
    )(block_e, n_active, slot_dest, h, w_down)


def _route(logits, e_bias):
    T = logits.shape[0]
    scores = jax.nn.sigmoid(logits)
    biased = scores + e_bias.astype(F32)
    grp_score = lax.top_k(biased.reshape(T, N_GROUPS, EXPERTS_PER_GROUP), 2)[0].sum(-1)
    _, gidx = lax.top_k(grp_score, TOPK_GROUPS)
    gmask = (gidx[..., None] == jnp.arange(N_GROUPS)[None, None, :]).any(axis=1)
    emask = jnp.repeat(gmask, EXPERTS_PER_GROUP, axis=1)
    _, idx = lax.top_k(jnp.where(emask, biased, -jnp.inf), TOP_K)
    w = jnp.take_along_axis(scores, idx, axis=1)
    w = w / w.sum(-1, keepdims=True) * ROUTED_SCALE
    return idx, w


def _dispatch_plan(idx, tm):
    T = idx.shape[0]
    A = T * TOP_K
    NB = -(-A // tm) + N_EXPERTS
    idx = idx.astype(jnp.int32)
    sel = (idx[:, :, None] == jnp.arange(N_EXPERTS, dtype=jnp.int32)[None, None, :]).any(axis=1).astype(jnp.int32)
    incl = jnp.cumsum(sel, axis=0)
    counts = incl[-1]
    rank = jnp.take_along_axis(incl - sel, idx, axis=1)
    padded = (counts + tm - 1) // tm * tm
    pad_end = jnp.cumsum(padded)
    pad_start = pad_end - padded
    n_active = (pad_end[-1] // tm).astype(jnp.int32)
    blk = jnp.arange(NB, dtype=jnp.int32)
    block_e = jnp.minimum(jnp.searchsorted(pad_end, jnp.minimum(blk, n_active - 1) * tm, side='right'),
                          N_EXPERTS - 1).astype(jnp.int32)
    dest = (pad_start[idx] + rank).reshape(A)
    slot_src = jnp.full((NB * tm,), -1, jnp.int32).at[dest].set(jnp.arange(A, dtype=jnp.int32),
                                                                unique_indices=True)
    slot = jnp.arange(NB * tm, dtype=jnp.int32)
    slot_token = jnp.where(slot_src >= 0, slot_src // TOP_K, 0)
    spare = A + (slot // tm % 2) * tm + slot % tm
    slot_dest = jnp.where(slot_src >= 0, slot_src, spare)
    return slot_token, slot_dest, block_e, n_active.reshape(1)


def moe_block(x_parts, xb_parts, xp_parts, p, layer):
    T = sum(x.shape[0] for x in x_parts)
    logits = jnp.concatenate([router_logits(x, p['router_w'], layer) for x in x_parts], axis=0)
    idx, wts = _route(logits, p['router_bias'][layer])
    plan = _dispatch_plan(idx, MOE_TM)
    y = routed_ffn(jnp.concatenate(xp_parts, axis=0), p['exp_w_gate'], p['exp_w_up'], p['exp_w_down'],
                   layer, plan, MOE_TM)
    y = y.reshape(y.shape[0] // TOP_K, TOP_K * y.shape[1])
    shared = [shared_ffn(xb, p['shared_w_gate'], p['shared_w_up'], p['shared_w_down'], layer)
              for xb in xb_parts]
    return y, wts, shared


def _rope_tables(pos, half):
    inv_freq = ROPE_THETA ** (-jnp.arange(half, dtype=F32) / half)
    ang = pos.astype(F32)[:, None] * inv_freq[None, :]
    return jnp.cos(ang), jnp.sin(ang)


def _even_prompt(x_bf, B, L, p, i):
    proj = matmul(x_bf, p['w_in_even'][i].astype(BF16), name="in_proj_even")
    proj3 = proj.reshape(B, L, -1)
    cos, sin = _rope_tables(jnp.arange(L, dtype=jnp.int32), RET_HEAD_DIM // 2)
    conv_out, conv_new = conv_prompt(proj3, p['conv_w'][i], p['conv_b'][i], p['conv_ln_g'][i], p['conv_ln_b'][i])
    ret_out, s_new = retention_prompt(proj3, cos, sin, p['ret_gn_g'][i], p['ret_gn_b'][i])
    mixed = jnp.concatenate([conv_out, ret_out], axis=-1).reshape(B * L, -1)
    return matmul(mixed, p['w_out_even'][i].astype(BF16), name="out_proj_even"), conv_new, s_new


def _even_sample(x_bf, conv_buf, ret_state, p, i):
    B = x_bf.shape[0]
    H, d = RET_HEADS, RET_HEAD_DIM
    proj = matmul(x_bf, p['w_in_even'][i].astype(BF16), name="in_proj_even")
    conv_out, conv_new = conv_sample(proj, conv_buf, p['conv_w'][i], p['conv_b'][i],
                                     p['conv_ln_g'][i], p['conv_ln_b'][i])
    o2 = 2 * CONV_CH
    q_col = proj[:, o2:o2 + RET_WIDTH].reshape(B, H, d, 1)
    k_col = proj[:, o2 + RET_WIDTH:o2 + 2 * RET_WIDTH].reshape(B, H, d, 1)
    v_row = proj[:, o2 + 2 * RET_WIDTH:o2 + 3 * RET_WIDTH].reshape(B, 1, RET_WIDTH)
    g_row = proj[:, o2 + 3 * RET_WIDTH:].reshape(B, 1, RET_WIDTH)
    cos, sin = _rope_tables(jnp.full((1,), PAST_LEN, jnp.int32), d // 2)
    ret_out, s_new = retention_sample(q_col, k_col, v_row, g_row, ret_state, cos.reshape(d // 2, 1),
                                      sin.reshape(d // 2, 1), p['ret_gn_g'][i], p['ret_gn_b'][i])
    mixed = jnp.concatenate([conv_out, ret_out.reshape(B, RET_WIDTH)], axis=-1)
    return matmul(mixed, p['w_out_even'][i].astype(BF16), name="out_proj_even"), conv_new, s_new


def _att_rope_tables(pos, rows):
    cos, sin = _rope_tables(pos, ATT_HEAD_DIM // 2)
    cos = jnp.tile(cos, (rows // cos.shape[0], LANES // cos.shape[1]))
    sin = jnp.tile(sin, (rows // sin.shape[0], LANES // sin.shape[1]))
    return cos, sin


def _odd_prompt(x_bf, B, L, p, i):
    cos, sin = _att_rope_tables(jnp.arange(L, dtype=jnp.int32), L)
    qkv = matmul(x_bf, p['w_qkv_odd'][i].astype(BF16), bias=p['b_qkv_odd'][i],
                 rope=(cos, sin, Q_WIDTH + KV_WIDTH), name="qkv_proj_odd")
    qkv3 = qkv.reshape(B, L, -1)
    o = swa_prompt(qkv3, p['sinks'][i])
    out = matmul(o.reshape(B * L, Q_WIDTH), p['w_out_odd'][i].astype(BF16), bias=p['b_out_odd'][i],
                 name="out_proj_odd")
    k_new = qkv3[:, L - WINDOW:, Q_WIDTH:Q_WIDTH + KV_WIDTH].reshape(B, WINDOW, N_KV_HEADS, ATT_HEAD_DIM)
    v_new = qkv3[:, L - WINDOW:, Q_WIDTH + KV_WIDTH:].reshape(B, WINDOW, N_KV_HEADS, ATT_HEAD_DIM)
    return out, k_new, v_new


def _odd_sample(x_bf, k_buf, v_buf, p, i):
    B = x_bf.shape[0]
    cos, sin = _att_rope_tables(jnp.full((1,), PAST_LEN, jnp.int32), B)
    qkv = matmul(x_bf, p['w_qkv_odd'][i].astype(BF16), bias=p['b_qkv_odd'][i],
                 rope=(cos, sin, Q_WIDTH + KV_WIDTH), name="qkv_proj_odd")
    q3 = qkv[:, :Q_WIDTH].reshape(B, N_HEADS, ATT_HEAD_DIM)
    kn3 = qkv[:, Q_WIDTH:Q_WIDTH + KV_WIDTH].reshape(B, N_KV_HEADS, ATT_HEAD_DIM)
    vn3 = qkv[:, Q_WIDTH + KV_WIDTH:].reshape(B, N_KV_HEADS, ATT_HEAD_DIM)
    o, k_new, v_new = swa_sample(q3, kn3, vn3, k_buf.reshape(B, WINDOW, KV_WIDTH),
                                 v_buf.reshape(B, WINDOW, KV_WIDTH), p['sinks'][i])
    out = matmul(o.reshape(B, Q_WIDTH), p['w_out_odd'][i].astype(BF16), bias=p['b_out_odd'][i],
                 name="out_proj_odd")
    shape = (B, WINDOW, N_KV_HEADS, ATT_HEAD_DIM)
    return out, k_new.reshape(shape), v_new.reshape(shape)


def kernel(x_prompt, x_sample, state_conv, state_ret, state_win_k, state_win_v,
           w_in_even, conv_w, conv_b, conv_ln_g, conv_ln_b, ret_gn_g, ret_gn_b, w_out_even,
           w_qkv_odd, b_qkv_odd, sinks, w_out_odd, b_out_odd, ln_g, ln_b,
           router_w, router_bias, exp_w_gate, exp_w_up, exp_w_down,
           shared_w_gate, shared_w_up, shared_w_down):
    p = dict(w_in_even=w_in_even, conv_w=conv_w, conv_b=conv_b, conv_ln_g=conv_ln_g, conv_ln_b=conv_ln_b,
             ret_gn_g=ret_gn_g, ret_gn_b=ret_gn_b, w_out_even=w_out_even,
             w_qkv_odd=w_qkv_odd, b_qkv_odd=b_qkv_odd, sinks=sinks, w_out_odd=w_out_odd, b_out_odd=b_out_odd,
             ln_g=ln_g, ln_b=ln_b, router_w=router_w, router_bias=router_bias,
             exp_w_gate=exp_w_gate, exp_w_up=exp_w_up, exp_w_down=exp_w_down,
             shared_w_gate=shared_w_gate, shared_w_up=shared_w_up, shared_w_down=shared_w_down)
    B, L, D = x_prompt.shape
    Bs = x_sample.shape[0]
    Tp = B * L
    xp = x_prompt.reshape(Tp, D)
    xs = x_sample.reshape(Bs, D)
    xp_bf, xs_bf = xp.astype(BF16), xs.astype(BF16)
    conv_p, conv_s, ret_p, ret_s, wk_p, wk_s, wv_p, wv_s = [], [], [], [], [], [], [], []
    for layer in range(DEPTH):
        i = layer // 2
        if layer % 2 == 0:
            hp, c_new, r_new = _even_prompt(xp_bf, B, L, p, i)
            conv_p.append(c_new)
            ret_p.append(r_new)
            hs, c_new, r_new = _even_sample(xs_bf, state_conv[i], state_ret[i], p, i)
            conv_s.append(c_new)
            ret_s.append(r_new)
        else:
            hp, k_new, v_new = _odd_prompt(xp_bf, B, L, p, i)
            wk_p.append(k_new)
            wv_p.append(v_new)
            hs, k_new, v_new = _odd_sample(xs_bf, state_win_k[i], state_win_v[i], p, i)
            wk_s.append(k_new)
            wv_s.append(v_new)
        xp, xp_bf, xp_pk = deepnorm(xp, [(hp, 0)], ln_g[layer, 0], ln_b[layer, 0], name="ln_mixer_prompt")
        xs, xs_bf, xs_pk = deepnorm(xs, [(hs, 0)], ln_g[layer, 0], ln_b[layer, 0], name="ln_mixer_sample")
        y, wts, shared = moe_block([xp, xs], [xp_bf, xs_bf], [xp_pk, xs_pk], p, layer)
        xp, xp_bf, _ = deepnorm(xp, [(shared[0], 0)], ln_g[layer, 1], ln_b[layer, 1], combine=(y, wts, 0),
                                tm=64, name="ln_moe_prompt")
        xs, xs_bf, _ = deepnorm(xs, [(shared[1], 0)], ln_g[layer, 1], ln_b[layer, 1], combine=(y, wts, Tp),
                                tm=32, name="ln_moe_sample")
    return (xp.reshape(B, L, D), xs.reshape(Bs, 1, D),
            jnp.stack(conv_p), jnp.stack(conv_s), jnp.stack(ret_p), jnp.stack(ret_s),
            jnp.stack(wk_p), jnp.stack(wk_s), jnp.stack(wv_p), jnp.stack(wv_s))
```

```python
import functools

import jax
import jax.numpy as jnp
import numpy as np
from jax import lax
from jax.experimental import pallas as pl
from jax.experimental.pallas import tpu as pltpu

F32 = jnp.float32
BF16 = jnp.bfloat16

D_MODEL = 4096
DEPTH = 2
PAST_LEN = 16384
CONV_CH = D_MODEL // 2
CONV_WIDTH = 31
CONV_HALO = 32
RET_HEADS = 8
RET_HEAD_DIM = 256
RET_WIDTH = RET_HEADS * RET_HEAD_DIM
RET_CHUNK = 128
ATT_HEAD_DIM = 64
N_HEADS = 64
N_KV_HEADS = 8
GROUP = N_HEADS // N_KV_HEADS
WINDOW = 128
Q_WIDTH = N_HEADS * ATT_HEAD_DIM
KV_WIDTH = N_KV_HEADS * ATT_HEAD_DIM
ROPE_THETA = 10000.0
N_EXPERTS = 128
TOP_K = 8
N_GROUPS = 8
TOPK_GROUPS = 4
EXPERTS_PER_GROUP = N_EXPERTS // N_GROUPS
EXPERT_FF = 512
ROUTED_SCALE = 2.5
DN_ALPHA = (2.0 * DEPTH) ** 0.25
LN_EPS = 1e-5

LANES = 128
V7X_VMEM_LIMIT = 52 * 1024 * 1024
V7X_VMEM_ROUTED = 58 * 1024 * 1024
MOE_TM = 256


def _params(*sem):
    return pltpu.CompilerParams(dimension_semantics=sem, vmem_limit_bytes=V7X_VMEM_LIMIT)


def _silu(x):
    return x * jax.nn.sigmoid(x)


def _rot_half_64(x):
    n = x.shape[-1]
    lane = lax.broadcasted_iota(jnp.int32, x.shape, x.ndim - 1)
    fwd = pltpu.roll(x, n - ATT_HEAD_DIM // 2, x.ndim - 1)
    bwd = pltpu.roll(x, ATT_HEAD_DIM // 2, x.ndim - 1)
    return jnp.where(lane % ATT_HEAD_DIM < ATT_HEAD_DIM // 2, -fwd, bwd)


def _mm_body(*refs, has_bias, rope_tiles):
    x_ref, w_ref = refs[0], refs[1]
    pos = 2
    b_ref = None
    if has_bias:
        b_ref = refs[pos]
        pos += 1
    if rope_tiles:
        cos_ref, sin_ref = refs[pos], refs[pos + 1]
        pos += 2
    o_ref = refs[pos]
    acc = jnp.dot(x_ref[...].astype(BF16), w_ref[...].astype(BF16), preferred_element_type=F32)
    if has_bias:
        acc = acc + b_ref[...]
    if rope_tiles:
        reps = acc.shape[1] // LANES
        cos = jnp.concatenate([cos_ref[...]] * reps, axis=1)
        sin = jnp.concatenate([sin_ref[...]] * reps, axis=1)
        @pl.when(pl.program_id(1) < rope_tiles)
        def _():
            o_ref[...] = (acc * cos + _rot_half_64(acc) * sin).astype(o_ref.dtype)

        @pl.when(pl.program_id(1) >= rope_tiles)
        def _():
            o_ref[...] = acc.astype(o_ref.dtype)
    else:
        o_ref[...] = acc.astype(o_ref.dtype)


def matmul(x, w, bias=None, rope=None, out_dtype=F32, tm=1024, tn=512, name="matmul"):
    M, K = x.shape
    N = w.shape[1]
    tm = min(tm, M)
    tn = min(tn, N)
    assert M % tm == 0 and N % tn == 0
    in_specs = [pl.BlockSpec((tm, K), lambda i, j: (i, 0)),
                pl.BlockSpec((K, tn), lambda i, j: (0, j))]
    args = [x, w]
    if bias is not None:
        in_specs.append(pl.BlockSpec((1, tn), lambda i, j: (0, j)))
        args.append(bias.reshape(1, N))
    rope_tiles = 0
    if rope is not None:
        cos, sin, n_cols = rope
        assert n_cols % tn == 0 and cos.shape[0] % tm == 0
        rope_tiles = n_cols // tn
        nrb = cos.shape[0] // tm
        in_specs += [pl.BlockSpec((tm, LANES), lambda i, j: (i % nrb, 0))] * 2
        args += [cos, sin]
    return pl.pallas_call(
        functools.partial(_mm_body, has_bias=bias is not None, rope_tiles=rope_tiles),
        grid=(M // tm, N // tn),
        in_specs=in_specs,
        out_specs=pl.BlockSpec((tm, tn), lambda i, j: (i, j)),
        out_shape=jax.ShapeDtypeStruct((M, N), out_dtype),
        compiler_params=_params("parallel", "arbitrary"),
        name=name,
    )(*args)


def _split_bf16(x):
    hi = x.astype(BF16)
    lo = (x - hi.astype(F32)).astype(BF16)
    return hi, lo


def _router_body(x_ref, w_ref, o_ref):
    xh, xl = _split_bf16(x_ref[...])
    wh, wl = _split_bf16(w_ref[0])
    acc = jnp.dot(xh, wh, preferred_element_type=F32)
    acc = acc + jnp.dot(xl, wh, preferred_element_type=F32)
    acc = acc + jnp.dot(xh, wl, preferred_element_type=F32)
    o_ref[...] = acc


def router_logits(x, w, layer, tm=512):
    M, K = x.shape
    N = w.shape[-1]
    tm = min(tm, M)
    return pl.pallas_call(
        _router_body,
        grid=(M // tm,),
        in_specs=[pl.BlockSpec((tm, K), lambda i: (i, 0)), pl.BlockSpec((1, K, N), lambda i: (layer, 0, 0))],
        out_specs=pl.BlockSpec((tm, N), lambda i: (i, 0)),
        out_shape=jax.ShapeDtypeStruct((M, N), F32),
        compiler_params=_params("parallel"),
        name="router",
    )(x, w)


def _layer_norm_rows(z, g, b):
    mu = jnp.mean(z, axis=-1, keepdims=True)
    zc = z - mu
    var = jnp.mean(zc * zc, axis=-1, keepdims=True)
    return zc * lax.rsqrt(var + LN_EPS) * g + b


def _unpack_pair(w):
    lo = lax.bitcast_convert_type(w << 16, F32)
    hi = lax.bitcast_convert_type(w & jnp.uint32(0xFFFF0000), F32)
    return lo, hi


def _pack_pair(lo, hi):
    lo_bits = lax.bitcast_convert_type(lo.astype(BF16).astype(F32), jnp.uint32) >> 16
    hi_bits = lax.bitcast_convert_type(hi.astype(BF16).astype(F32), jnp.uint32) & jnp.uint32(0xFFFF0000)
    return hi_bits | lo_bits


def _ln_body(*refs, n_h, combine):
    x_ref = refs[0]
    h_refs = refs[1:1 + n_h]
    pos = 1 + n_h
    if combine:
        y_ref, w_ref = refs[pos], refs[pos + 1]
        pos += 2
    g_ref, b_ref, o_ref, ob_ref, op_ref = refs[pos:]
    half = x_ref.shape[1] // 2
    z = DN_ALPHA * x_ref[...]
    for h_ref in h_refs:
        z = z + h_ref[...]
    if combine:
        zl, zh = z[:, :half], z[:, half:]
        for k in range(TOP_K):
            lo, hi = _unpack_pair(y_ref[k])
            wk = w_ref[:, k:k + 1]
            zl = zl + wk * lo
            zh = zh + wk * hi
        z = jnp.concatenate([zl, zh], axis=1)
    y = _layer_norm_rows(z, g_ref[...], b_ref[...])
    o_ref[...] = y
    ob_ref[...] = y.astype(BF16)
    op_ref[...] = _pack_pair(y[:, :half], y[:, half:])


def deepnorm(x, hs, g, b, combine=None, tm=256, name="deepnorm"):
    M, D = x.shape
    tm = min(tm, M)
    assert M % tm == 0
    row_spec = lambda width, off: pl.BlockSpec((tm, width), functools.partial(lambda i, o: (i + o, 0), o=off // tm))
    in_specs = [row_spec(D, 0)]
    args = [x]
    for h, off in hs:
        assert off % tm == 0
        in_specs.append(row_spec(D, off))
        args.append(h)
    if combine is not None:
        y, wts, off = combine
        assert off % tm == 0
        y_spec = pl.BlockSpec((TOP_K, tm, y.shape[2]), functools.partial(lambda i, o: (0, i + o, 0), o=off // tm))
        in_specs += [y_spec, row_spec(TOP_K, off)]
        args += [y, wts]
    in_specs += [pl.BlockSpec((1, D), lambda i: (0, 0))] * 2
    args += [g.reshape(1, D), b.reshape(1, D)]
    return pl.pallas_call(
        functools.partial(_ln_body, n_h=len(hs), combine=combine is not None),
        grid=(M // tm,),
        in_specs=in_specs,
        out_specs=[row_spec(D, 0), row_spec(D, 0), row_spec(D // 2, 0)],
        out_shape=[jax.ShapeDtypeStruct((M, D), F32), jax.ShapeDtypeStruct((M, D), BF16),
                   jax.ShapeDtypeStruct((M, D // 2), jnp.uint32)],
        compiler_params=_params("parallel"),
        name=name,
    )(*args)


def _conv_prompt_body(a_ref, g_ref, pa_ref, pg_ref, w_ref, bdw_ref, lng_ref, lnb_ref,
                      o_ref, st_ref, full_ref, y_ref, *, tl):
    i = pl.program_id(1)
    u = a_ref[0] * jax.nn.sigmoid(g_ref[0])
    up = pa_ref[0] * jax.nn.sigmoid(pg_ref[0])
    up = jnp.where(i > 0, up, 0.0)
    full_ref[0:CONV_HALO, :] = up
    full_ref[CONV_HALO:CONV_HALO + tl, :] = u
    base = CONV_HALO - (CONV_WIDTH - 1)
    n_col = u.shape[1] // LANES

    def col_step(c, carry):
        off = pl.multiple_of(c * LANES, LANES)
        acc = jnp.zeros((tl, LANES), F32)
        for j in range(CONV_WIDTH):
            acc = acc + w_ref[j:j + 1, pl.ds(off, LANES)] * full_ref[base + j:base + j + tl, pl.ds(off, LANES)]
        y_ref[:, pl.ds(off, LANES)] = acc
        return carry

    lax.fori_loop(0, n_col, col_step, 0)
    y = _layer_norm_rows(y_ref[...] + bdw_ref[...], lng_ref[...], lnb_ref[...])
    o_ref[0] = _silu(y).astype(o_ref.dtype)

    @pl.when(i == pl.num_programs(1) - 1)
    def _():
        st_ref[0] = full_ref[CONV_HALO + tl - (CONV_WIDTH - 1):CONV_HALO + tl, :]


def conv_prompt(proj3, w_dw, b_dw, ln_g, ln_b, tl=128):
    B, L, _ = proj3.shape
    C = CONV_CH
    nh = tl // CONV_HALO
    vec = lambda v: v.reshape(1, C)
    return pl.pallas_call(
        functools.partial(_conv_prompt_body, tl=tl),
        grid=(B, L // tl),
        in_specs=[
            pl.BlockSpec((1, tl, C), lambda b, i: (b, i, 0)),
            pl.BlockSpec((1, tl, C), lambda b, i: (b, i, 1)),
            pl.BlockSpec((1, CONV_HALO, C), lambda b, i: (b, jnp.maximum(i * nh - 1, 0), 0)),
            pl.BlockSpec((1, CONV_HALO, C), lambda b, i: (b, jnp.maximum(i * nh - 1, 0), 1)),
            pl.BlockSpec((CONV_WIDTH, C), lambda b, i: (0, 0)),
            pl.BlockSpec((1, C), lambda b, i: (0, 0)),
            pl.BlockSpec((1, C), lambda b, i: (0, 0)),
            pl.BlockSpec((1, C), lambda b, i: (0, 0)),
        ],
        out_specs=[
            pl.BlockSpec((1, tl, C), lambda b, i: (b, i, 0)),
            pl.BlockSpec((1, CONV_WIDTH - 1, C), lambda b, i: (b, 0, 0)),
        ],
        out_shape=[jax.ShapeDtypeStruct((B, L, C), BF16),
                   jax.ShapeDtypeStruct((B, CONV_WIDTH - 1, C), F32)],
        scratch_shapes=[pltpu.VMEM((CONV_HALO + tl, C), F32), pltpu.VMEM((tl, C), F32)],
        compiler_params=_params("parallel", "arbitrary"),
        name="conv_prompt",
    )(proj3, proj3, proj3, proj3, w_dw, vec(b_dw), vec(ln_g), vec(ln_b))


def _conv_sample_body(a_ref, g_ref, buf_ref, w_ref, bdw_ref, lng_ref, lnb_ref, o_ref, st_ref):
    u = a_ref[...] * jax.nn.sigmoid(g_ref[...])
    nb = CONV_WIDTH - 1
    acc = w_ref[nb:nb + 1, :] * u
    for j in range(nb):
        acc = acc + w_ref[j:j + 1, :] * buf_ref[:, j, :]
    y = _layer_norm_rows(acc + bdw_ref[...], lng_ref[...], lnb_ref[...])
    o_ref[...] = _silu(y).astype(o_ref.dtype)
    st_ref[:, 0:nb - 1, :] = buf_ref[:, 1:nb, :]
    st_ref[:, nb - 1, :] = u


def conv_sample(proj, buf, w_dw, b_dw, ln_g, ln_b, tb=8):
    B = proj.shape[0]
    C = CONV_CH
    nb = CONV_WIDTH - 1
    vec = lambda v: v.reshape(1, C)
    return pl.pallas_call(
        _conv_sample_body,
        grid=(B // tb,),
        in_specs=[
            pl.BlockSpec((tb, C), lambda b: (b, 0)),
            pl.BlockSpec((tb, C), lambda b: (b, 1)),
            pl.BlockSpec((tb, nb, C), lambda b: (b, 0, 0)),
            pl.BlockSpec((CONV_WIDTH, C), lambda b: (0, 0)),
            pl.BlockSpec((1, C), lambda b: (0, 0)),
            pl.BlockSpec((1, C), lambda b: (0, 0)),
            pl.BlockSpec((1, C), lambda b: (0, 0)),
        ],
        out_specs=[pl.BlockSpec((tb, C), lambda b: (b, 0)),
                   pl.BlockSpec((tb, nb, C), lambda b: (b, 0, 0))],
        out_shape=[jax.ShapeDtypeStruct((B, C), BF16), jax.ShapeDtypeStruct((B, nb, C), F32)],
        compiler_params=_params("parallel"),
        name="conv_sample",
    )(proj, proj, buf, w_dw, vec(b_dw), vec(ln_g), vec(ln_b))


def _ret_log_gamma():
    return np.log1p(-np.exp2(-5.0 - np.arange(RET_HEADS, dtype=np.float64)))


def _group_norm_gate(o, gate, gn_g, gn_b):
    mu = jnp.mean(o, axis=-1, keepdims=True)
    oc = o - mu
    var = jnp.mean(oc * oc, axis=-1, keepdims=True)
    return (oc * lax.rsqrt(var + LN_EPS) * gn_g + gn_b) * _silu(gate)


def _rope_256(x, cos, sin):
    half = RET_HEAD_DIM // 2
    x1, x2 = x[..., :half], x[..., half:]
    return jnp.concatenate([x1 * cos - x2 * sin, x2 * cos + x1 * sin], axis=-1)


def _ret_prompt_body(q_ref, k_ref, v_ref, g_ref, cos_ref, sin_ref, intra_ref, qd_ref, kd_ref, cd_ref,
                     gng_ref, gnb_ref, o_ref, s_ref):
    c = pl.program_id(2)

    @pl.when(c == 0)
    def _():
        s_ref[...] = jnp.zeros_like(s_ref)

    cos, sin = cos_ref[...], sin_ref[...]
    q = _rope_256(q_ref[0], cos, sin)
    k = _rope_256(k_ref[0], cos, sin) * (RET_HEAD_DIM ** -0.5)
    vb = v_ref[0].astype(BF16)
    s_prev = s_ref[0, 0]
    scores = lax.dot_general(q.astype(BF16), k.astype(BF16), (((1,), (1,)), ((), ())),
                             preferred_element_type=F32) * intra_ref[0]
    o = jnp.dot(scores.astype(BF16), vb, preferred_element_type=F32)
    o = o + jnp.dot((q * qd_ref[0]).astype(BF16), s_prev.astype(BF16), preferred_element_type=F32)
    kdt = (k * kd_ref[0]).T.astype(BF16)
    kv = jnp.dot(kdt, vb, preferred_element_type=F32)
    s_ref[0, 0] = cd_ref[0] * s_prev + kv
    o_ref[0] = _group_norm_gate(o, g_ref[0], gng_ref[...], gnb_ref[...]).astype(o_ref.dtype)


def retention_prompt(proj3, cos, sin, gn_g, gn_b):
    B, L, _ = proj3.shape
    H, d, c = RET_HEADS, RET_HEAD_DIM, RET_CHUNK
    lg = _ret_log_gamma()
    idx = np.arange(c, dtype=np.float64)
    rel = idx[:, None] - idx[None, :]
    intra = np.where(rel[None] >= 0, np.exp(lg[:, None, None] * np.maximum(rel, 0.0)[None]), 0.0)
    ones = np.ones((1, 1, d))
    qd = np.exp(lg[:, None] * (idx + 1.0)[None, :])[:, :, None] * ones
    kd = np.exp(lg[:, None] * (c - 1.0 - idx)[None, :])[:, :, None] * ones
    cd = np.exp(lg * c)[:, None, None] * ones
    col0 = 2 * CONV_CH // d
    head_spec = lambda off: pl.BlockSpec((1, c, d), lambda b, h, n: (b, n, col0 + off * H + h))
    tab_spec = pl.BlockSpec((1, c, d), lambda b, h, n: (h, 0, 0))
    return pl.pallas_call(
        _ret_prompt_body,
        grid=(B, H, L // c),
        in_specs=[
            head_spec(0), head_spec(1), head_spec(2), head_spec(3),
            pl.BlockSpec((c, d // 2), lambda b, h, n: (n, 0)),
            pl.BlockSpec((c, d // 2), lambda b, h, n: (n, 0)),
            pl.BlockSpec((1, c, c), lambda b, h, n: (h, 0, 0)),
            tab_spec, tab_spec,
            pl.BlockSpec((1, 1, d), lambda b, h, n: (h, 0, 0)),
            pl.BlockSpec((1, d), lambda b, h, n: (0, h)),
            pl.BlockSpec((1, d), lambda b, h, n: (0, h)),
        ],
        out_specs=[
            pl.BlockSpec((1, c, d), lambda b, h, n: (b, n, h)),
            pl.BlockSpec((1, 1, d, d), lambda b, h, n: (b, h, 0, 0)),
        ],
        out_shape=[jax.ShapeDtypeStruct((B, L, H * d), BF16),
                   jax.ShapeDtypeStruct((B, H, d, d), F32)],
        compiler_params=_params("parallel", "parallel", "arbitrary"),
        name="retention_prompt",
    )(proj3, proj3, proj3, proj3, cos, sin,
      jnp.asarray(intra, F32), jnp.asarray(qd, F32), jnp.asarray(kd, F32), jnp.asarray(cd, F32),
      gn_g.reshape(1, H * d), gn_b.reshape(1, H * d))


def _ret_sample_body(qc_ref, kc_ref, v_ref, g_ref, s_ref, cosc_ref, sinc_ref, gam_ref, gng_ref, gnb_ref,
                     o_ref, so_ref):
    half = RET_HEAD_DIM // 2
    cos, sin = cosc_ref[...], sinc_ref[...]

    def rope_col(x):
        x1, x2 = x[:half], x[half:]
        return jnp.concatenate([x1 * cos - x2 * sin, x2 * cos + x1 * sin], axis=0)

    for h in range(RET_HEADS):
        lo, hi = h * RET_HEAD_DIM, (h + 1) * RET_HEAD_DIM
        q = rope_col(qc_ref[0, h])
        k = rope_col(kc_ref[0, h]) * (RET_HEAD_DIM ** -0.5)
        v = v_ref[0, :, lo:hi]
        s_new = gam_ref[h] * s_ref[0, h] + k * v
        so_ref[0, h] = s_new
        o = jnp.sum(q * s_new, axis=0, keepdims=True)
        o_ref[0, :, lo:hi] = _group_norm_gate(o, g_ref[0, :, lo:hi], gng_ref[:, lo:hi],
                                              gnb_ref[:, lo:hi]).astype(o_ref.dtype)


def retention_sample(q_col, k_col, v_row, g_row, state, cos_col, sin_col, gn_g, gn_b):
    B = state.shape[0]
    H, d = RET_HEADS, RET_HEAD_DIM
    gam = jnp.asarray(np.exp(_ret_log_gamma()), F32)
    col_spec = pl.BlockSpec((1, H, d, 1), lambda b: (b, 0, 0, 0))
    row_spec = pl.BlockSpec((1, 1, H * d), lambda b: (b, 0, 0))
    return pl.pallas_call(
        _ret_sample_body,
        grid=(B,),
        in_specs=[
            col_spec, col_spec, row_spec, row_spec,
            pl.BlockSpec((1, H, d, d), lambda b: (b, 0, 0, 0)),
            pl.BlockSpec((d // 2, 1), lambda b: (0, 0)),
            pl.BlockSpec((d // 2, 1), lambda b: (0, 0)),
            pl.BlockSpec(memory_space=pltpu.SMEM),
            pl.BlockSpec((1, H * d), lambda b: (0, 0)),
            pl.BlockSpec((1, H * d), lambda b: (0, 0)),
        ],
        out_specs=[row_spec, pl.BlockSpec((1, H, d, d), lambda b: (b, 0, 0, 0))],
        out_shape=[jax.ShapeDtypeStruct((B, 1, H * d), BF16), jax.ShapeDtypeStruct((B, H, d, d), F32)],
        compiler_params=_params("parallel"),
        name="retention_sample",
    )(q_col, k_col, v_row, g_row, state, cos_col, sin_col, gam, gn_g.reshape(1, H * d), gn_b.reshape(1, H * d))


def _pair_operand(x2, h, lane):
    col = x2[:, (h // 2) * LANES:(h // 2 + 1) * LANES]
    swapped = pltpu.roll(col, ATT_HEAD_DIM, 1)
    first, second = (col, swapped) if h % 2 == 0 else (swapped, col)
    return jnp.concatenate([jnp.where(lane < ATT_HEAD_DIM, first, 0.0),
                            jnp.where(lane >= ATT_HEAD_DIM, second, 0.0)], axis=0).astype(BF16)


def _swa_prompt_body(sink_ref, q_ref, kc_ref, kp_ref, vc_ref, vp_ref, o_ref):
    j = pl.program_id(1)
    W = WINDOW
    k2 = jnp.concatenate([kp_ref[0], kc_ref[0]], axis=0)
    v2 = jnp.concatenate([vp_ref[0], vc_ref[0]], axis=0)
    row = lax.broadcasted_iota(jnp.int32, (W, 2 * W), 0)
    col = lax.broadcasted_iota(jnp.int32, (W, 2 * W), 1)
    first_key = jnp.where(j > 0, 0, W)
    mask = (col >= jnp.maximum(row, first_key)) & (col <= W + row)
    lane_kv = lax.broadcasted_iota(jnp.int32, (2 * W, LANES), 1)
    lane_o = lax.broadcasted_iota(jnp.int32, (W, LANES), 1)
    scale = ATT_HEAD_DIM ** -0.5
    for h in range(N_KV_HEADS):
        kk = _pair_operand(k2, h, lane_kv)
        vv = _pair_operand(v2, h, lane_kv)
        for p in range(GROUP // 2):
            c0 = h * GROUP * ATT_HEAD_DIM + p * LANES
            qp = q_ref[0, :, c0:c0 + LANES].astype(BF16)
            s = lax.dot_general(qp, kk, (((1,), (1,)), ((), ())), preferred_element_type=F32) * scale
            probs, inv = [], []
            for t in range(2):
                sink = sink_ref[h * GROUP + 2 * p + t]
                st = jnp.where(mask, s[:, t * 2 * W:(t + 1) * 2 * W], -jnp.inf)
                m = jnp.maximum(jnp.max(st, axis=-1, keepdims=True), sink)
                e = jnp.exp(st - m)
                denom = jnp.sum(e, axis=-1, keepdims=True) + jnp.exp(sink - m)
                probs.append(e.astype(BF16))
                inv.append(1.0 / denom)
            o2 = jnp.dot(jnp.concatenate(probs, axis=1), vv, preferred_element_type=F32)
            o2 = o2 * jnp.where(lane_o < ATT_HEAD_DIM, inv[0], inv[1])
            o_ref[0, :, c0:c0 + LANES] = o2.astype(o_ref.dtype)


def swa_prompt(qkv3, sinks):
    B, L, _ = qkv3.shape
    W = WINDOW
    kcol = Q_WIDTH // KV_WIDTH
    return pl.pallas_call(
        _swa_prompt_body,
        grid_spec=pltpu.PrefetchScalarGridSpec(
            num_scalar_prefetch=0,
            grid=(B, L // W),
            in_specs=[
                pl.BlockSpec(memory_space=pltpu.SMEM),
                pl.BlockSpec((1, W, Q_WIDTH), lambda b, j: (b, j, 0)),
                pl.BlockSpec((1, W, KV_WIDTH), lambda b, j: (b, j, kcol)),
                pl.BlockSpec((1, W, KV_WIDTH), lambda b, j: (b, jnp.maximum(j - 1, 0), kcol)),
                pl.BlockSpec((1, W, KV_WIDTH), lambda b, j: (b, j, kcol + 1)),
                pl.BlockSpec((1, W, KV_WIDTH), lambda b, j: (b, jnp.maximum(j - 1, 0), kcol + 1)),
            ],
            out_specs=pl.BlockSpec((1, W, Q_WIDTH), lambda b, j: (b, j, 0)),
        ),
        out_shape=jax.ShapeDtypeStruct((B, L, Q_WIDTH), BF16),
        compiler_params=_params("parallel", "parallel"),
        name="swa_prompt",
    )(sinks, qkv3, qkv3, qkv3, qkv3, qkv3)


def _swa_sample_body(q_ref, kn_ref, vn_ref, knf_ref, vnf_ref, kb_ref, vb_ref, sink_ref, o_ref, ko_ref, vo_ref):
    W = WINDOW
    scale = ATT_HEAD_DIM ** -0.5
    kb = kb_ref[0]
    vb = vb_ref[0]
    for h in range(N_KV_HEADS):
        lo, hi = h * ATT_HEAD_DIM, (h + 1) * ATT_HEAD_DIM
        qh = q_ref[0, h * GROUP:(h + 1) * GROUP, :]
        kn = kn_ref[0, h:h + 1, :]
        vn = vn_ref[0, h:h + 1, :]
        sink = sink_ref[h * GROUP:(h + 1) * GROUP, :]
        s_buf = lax.dot_general(qh.astype(BF16), kb[:, lo:hi].astype(BF16), (((1,), (1,)), ((), ())),
                                preferred_element_type=F32) * scale
        s_new = jnp.sum(qh * kn, axis=-1, keepdims=True) * scale
        m = jnp.maximum(jnp.maximum(jnp.max(s_buf, axis=-1, keepdims=True), s_new), sink)
        e_buf = jnp.exp(s_buf - m)
        e_new = jnp.exp(s_new - m)
        denom = jnp.sum(e_buf, axis=-1, keepdims=True) + e_new + jnp.exp(sink - m)
        o = jnp.dot(e_buf.astype(BF16), vb[:, lo:hi].astype(BF16), preferred_element_type=F32)
        o = o + e_new * vn
        o_ref[0, h * GROUP:(h + 1) * GROUP, :] = (o / denom).astype(o_ref.dtype)
    ko_ref[0, 0:W - 1, :] = kb[1:W, :]
    ko_ref[0, W - 1:W, :] = knf_ref[0]
    vo_ref[0, 0:W - 1, :] = vb[1:W, :]
    vo_ref[0, W - 1:W, :] = vnf_ref[0]


def swa_sample(q3, kn3, vn3, k_buf, v_buf, sinks):
    B = q3.shape[0]
    W, d = WINDOW, ATT_HEAD_DIM
    head_spec = lambda n: pl.BlockSpec((1, n, d), lambda b: (b, 0, 0))
    flat_spec = pl.BlockSpec((1, 1, KV_WIDTH), lambda b: (b, 0, 0))
    buf_spec = pl.BlockSpec((1, W, KV_WIDTH), lambda b: (b, 0, 0))
    return pl.pallas_call(
        _swa_sample_body,
        grid=(B,),
        in_specs=[head_spec(N_HEADS), head_spec(N_KV_HEADS), head_spec(N_KV_HEADS), flat_spec, flat_spec,
                  buf_spec, buf_spec, pl.BlockSpec((N_HEADS, 1), lambda b: (0, 0))],
        out_specs=[head_spec(N_HEADS), buf_spec, buf_spec],
        out_shape=[jax.ShapeDtypeStruct((B, N_HEADS, d), BF16),
                   jax.ShapeDtypeStruct((B, W, KV_WIDTH), F32),
                   jax.ShapeDtypeStruct((B, W, KV_WIDTH), F32)],
        compiler_params=_params("parallel"),
        name="swa_sample",
    )(q3, kn3, vn3, kn3.reshape(B, 1, KV_WIDTH), vn3.reshape(B, 1, KV_WIDTH), k_buf, v_buf,
      sinks.reshape(N_HEADS, 1))


def _shared_up_body(x_ref, wg_ref, wu_ref, h_ref):
    x = x_ref[...]
    g = jnp.dot(x, wg_ref[0].astype(BF16), preferred_element_type=F32)
    u = jnp.dot(x, wu_ref[0].astype(BF16), preferred_element_type=F32)
    h_ref[...] = (_silu(g) * u).astype(h_ref.dtype)


def _shared_down_body(h_ref, wd_ref, y_ref):
    y_ref[...] = jnp.dot(h_ref[...], wd_ref[0].astype(BF16), preferred_element_type=F32)


def shared_ffn(xb, w_gate, w_up, w_down, layer, tm=512):
    M, D = xb.shape
    FF = w_gate.shape[-1]
    tm = min(tm, M)
    wsel = lambda i: (layer, 0, 0)
    h = pl.pallas_call(
        _shared_up_body,
        grid=(M // tm,),
        in_specs=[pl.BlockSpec((tm, D), lambda i: (i, 0)),
                  pl.BlockSpec((1, D, FF), wsel), pl.BlockSpec((1, D, FF), wsel)],
        out_specs=pl.BlockSpec((tm, FF), lambda i: (i, 0)),
        out_shape=jax.ShapeDtypeStruct((M, FF), BF16),
        compiler_params=_params("parallel"),
        name="shared_up",
    )(xb, w_gate, w_up)
    return pl.pallas_call(
        _shared_down_body,
        grid=(M // tm,),
        in_specs=[pl.BlockSpec((tm, FF), lambda i: (i, 0)), pl.BlockSpec((1, FF, D), wsel)],
        out_specs=pl.BlockSpec((tm, D), lambda i: (i, 0)),
        out_shape=jax.ShapeDtypeStruct((M, D), F32),
        compiler_params=_params("parallel"),
        name="shared_down",
    )(h, w_down)


def _expert_changed(be_ref, i):
    return (i == 0) | (be_ref[i] != be_ref[jnp.maximum(i - 1, 0)])


def _moe_up_body(be_ref, na_ref, tok_ref, x_hbm, wg_ref, wu_ref, h_ref, xbuf, sem, wg_bf, wu_bf, *, tm):
    i = pl.program_id(0)
    na = na_ref[0]
    half = wg_bf.shape[0] // 2

    def issue(blk, slot):
        base = blk * tm
        for r in range(tm):
            tok = tok_ref[base + r]
            pltpu.make_async_copy(x_hbm.at[pl.ds(tok, 1)], xbuf.at[slot, pl.ds(r, 1)], sem.at[slot]).start()

    @pl.when(i == 0)
    def _():
        issue(0, 0)

    @pl.when(i >= na)
    def _():
        h_ref[...] = jnp.zeros_like(h_ref)

    @pl.when(i < na)
    def _():
        slot = i % 2

        @pl.when(i + 1 < na)
        def _():
            issue(i + 1, 1 - slot)

        @pl.when(_expert_changed(be_ref, i))
        def _():
            wg_bf[...] = wg_ref[0, 0].astype(BF16)
            wu_bf[...] = wu_ref[0, 0].astype(BF16)

        pltpu.make_async_copy(x_hbm.at[pl.ds(0, tm)], xbuf.at[slot], sem.at[slot]).wait()
        lo, hi = _unpack_pair(xbuf[slot])
        lo, hi = lo.astype(BF16), hi.astype(BF16)
        g = (jnp.dot(lo, wg_bf[0:half], preferred_element_type=F32)
             + jnp.dot(hi, wg_bf[half:], preferred_element_type=F32))
        u = (jnp.dot(lo, wu_bf[0:half], preferred_element_type=F32)
             + jnp.dot(hi, wu_bf[half:], preferred_element_type=F32))
        h_ref[...] = (_silu(g) * u).astype(h_ref.dtype)


def _moe_down_body(be_ref, na_ref, dst_ref, h_ref, wd_ref, y_hbm, ybuf, sem, wd_bf, *, tm):
    i = pl.program_id(0)
    na = na_ref[0]
    half = wd_bf.shape[1] // 2

    def wait(slot):
        pltpu.make_async_copy(ybuf.at[slot], y_hbm.at[pl.ds(0, tm)], sem.at[slot]).wait()

    @pl.when(i == 0)
    def _():
        t_pad = y_hbm.shape[0] // TOP_K
        spare_t = 2 * tm // TOP_K
        ybuf[0] = jnp.zeros(ybuf.shape[1:], ybuf.dtype)
        for k in range(TOP_K):
            fill = pltpu.make_async_copy(ybuf.at[0, pl.ds(0, spare_t)],
                                         y_hbm.at[pl.ds((k + 1) * t_pad - spare_t, spare_t)], sem.at[0])
            fill.start()
            fill.wait()

    @pl.when(i < na)
    def _():
        slot = i % 2

        @pl.when(_expert_changed(be_ref, i))
        def _():
            wd_bf[...] = wd_ref[0, 0].astype(BF16)

        y = jnp.dot(h_ref[...], wd_bf[...], preferred_element_type=F32)

        @pl.when(i >= 2)
        def _():
            wait(slot)

        ybuf[slot] = _pack_pair(y[:, :half], y[:, half:])
        base = i * tm
        for r in range(tm):
            dst = dst_ref[base + r]
            pltpu.make_async_copy(ybuf.at[slot, pl.ds(r, 1)], y_hbm.at[pl.ds(dst, 1)], sem.at[slot]).start()

    @pl.when(i == pl.num_programs(0) - 1)
    def _():
        wait((na - 1) % 2)

        @pl.when(na >= 2)
        def _():
            wait(na % 2)


def routed_ffn(x_packed, w_gate, w_up, w_down, layer, plan, tm):
    slot_token, slot_dest, block_e, n_active = plan
    T, DH = x_packed.shape
    NB = block_e.shape[0]
    D, FF = w_gate.shape[-2], w_gate.shape[-1]
    A = T * TOP_K
    row = lambda i, be, na, ix: (jnp.minimum(i, na[0] - 1), 0)
    wsel = lambda i, be, na, ix: (layer, be[i], 0, 0)
    params = pltpu.CompilerParams(dimension_semantics=("arbitrary",), vmem_limit_bytes=V7X_VMEM_ROUTED)
    h = pl.pallas_call(
        functools.partial(_moe_up_body, tm=tm),
        grid_spec=pltpu.PrefetchScalarGridSpec(
            num_scalar_prefetch=3, grid=(NB,),
            in_specs=[pl.BlockSpec(memory_space=pl.ANY),
                      pl.BlockSpec((1, 1, D, FF), wsel), pl.BlockSpec((1, 1, D, FF), wsel)],
            out_specs=pl.BlockSpec((tm, FF), lambda i, be, na, ix: (i, 0)),
            scratch_shapes=[pltpu.VMEM((2, tm, DH), jnp.uint32), pltpu.SemaphoreType.DMA((2,)),
                            pltpu.VMEM((D, FF), BF16), pltpu.VMEM((D, FF), BF16)]),
        out_shape=jax.ShapeDtypeStruct((NB * tm, FF), BF16),
        compiler_params=params,
        name="moe_up",
    )(block_e, n_active, slot_token, x_packed, w_gate, w_up)
    return pl.pallas_call(
        functools.partial(_moe_down_body, tm=tm),
        grid_spec=pltpu.PrefetchScalarGridSpec(
            num_scalar_prefetch=3, grid=(NB,),
            in_specs=[pl.BlockSpec((tm, FF), row), pl.BlockSpec((1, 1, FF, D), wsel)],
            out_specs=pl.BlockSpec(memory_space=pl.ANY),
            scratch_shapes=[pltpu.VMEM((2, tm, DH), jnp.uint32), pltpu.SemaphoreType.DMA((2,)),
                            pltpu.VMEM((FF, D), BF16)]),
        out_shape=jax.ShapeDtypeStruct((A + 2 * tm, DH), jnp.uint32),
        compiler_params=params,
        name="moe_down",
    )(block_e, n_active, slot_dest, h, w_down)


def _route(logits, e_bias):
    T = logits.shape[0]
    scores = jax.nn.sigmoid(logits)
    biased = scores + e_bias.astype(F32)
    grp_score = lax.top_k(biased.reshape(T, N_GROUPS, EXPERTS_PER_GROUP), 2)[0].sum(-1)
    _, gidx = lax.top_k(grp_score, TOPK_GROUPS)
    gmask = (gidx[..., None] == jnp.arange(N_GROUPS)[None, None, :]).any(axis=1)
    emask = jnp.repeat(gmask, EXPERTS_PER_GROUP, axis=1)
    _, idx = lax.top_k(jnp.where(emask, biased, -jnp.inf), TOP_K)
    onehot = idx[:, :, None] == jnp.arange(N_EXPERTS, dtype=idx.dtype)[None, None, :]
    w = jnp.sum(jnp.where(onehot, scores[:, None, :], 0.0), axis=-1)
    w = w / w.sum(-1, keepdims=True) * ROUTED_SCALE
    return idx, w


SLOT_MAP_CHUNK = 2048


def _slot_map_body(dest_ref, src_ref, *, clear_chunk):
    phase, i = pl.program_id(0), pl.program_id(1)

    @pl.when(phase == 0)
    def _():
        def clear(j, carry):
            src_ref[i * clear_chunk + j] = -1
            return carry

        lax.fori_loop(0, clear_chunk, clear, 0, unroll=8)

    @pl.when(phase == 1)
    def _():
        def place(a, carry):
            src_ref[dest_ref[a]] = i * SLOT_MAP_CHUNK + a
            return carry

        lax.fori_loop(0, SLOT_MAP_CHUNK, place, 0, unroll=8)


def _slot_map(dest, n_slots):
    A = dest.shape[0]
    n_chunks = -(-A // SLOT_MAP_CHUNK)
    clear_chunk = -(-(n_slots + 1) // (8 * n_chunks)) * 8
    dest = jnp.concatenate([dest, jnp.full((n_chunks * SLOT_MAP_CHUNK - A,), n_slots, jnp.int32)])
    src = pl.pallas_call(
        functools.partial(_slot_map_body, clear_chunk=clear_chunk),
        grid=(2, n_chunks),
        in_specs=[pl.BlockSpec((SLOT_MAP_CHUNK,), lambda ph, i: (i * ph,), memory_space=pltpu.SMEM)],
        out_specs=pl.BlockSpec(memory_space=pltpu.SMEM),
        out_shape=jax.ShapeDtypeStruct((n_chunks * clear_chunk,), jnp.int32),
        compiler_params=pltpu.CompilerParams(dimension_semantics=("arbitrary", "arbitrary")),
        name="slot_map",
    )(dest)
    return src[:n_slots]


def _dispatch_plan(idx, tm):
    T = idx.shape[0]
    A = T * TOP_K
    NB = -(-A // tm) + N_EXPERTS
    spare_t = 2 * tm // TOP_K
    t_pad = T + spare_t
    idx = idx.astype(jnp.int32)
    onehot = idx[:, :, None] == jnp.arange(N_EXPERTS, dtype=jnp.int32)[None, None, :]
    sel = onehot.any(axis=1).astype(jnp.int32)
    incl = jnp.cumsum(sel, axis=0)
    counts = incl[-1]
    padded = (counts + tm - 1) // tm * tm
    pad_end = jnp.cumsum(padded)
    pad_start = pad_end - padded
    n_active = (pad_end[-1] // tm).astype(jnp.int32)
    blk = jnp.arange(NB, dtype=jnp.int32)
    block_e = jnp.minimum(jnp.searchsorted(pad_end, jnp.minimum(blk, n_active - 1) * tm, side='right'),
                          N_EXPERTS - 1).astype(jnp.int32)
    slot_of = (incl - sel + pad_start[None, :])[:, None, :]
    dest = jnp.sum(jnp.where(onehot, slot_of, 0), axis=-1).reshape(A)
    slot_src = _slot_map(dest, NB * tm)
    slot = jnp.arange(NB * tm, dtype=jnp.int32)
    valid = slot_src >= 0
    slot_token = jnp.where(valid, slot_src // TOP_K, 0)
    s2 = (slot // tm % 2) * tm + slot % tm
    spare = (s2 // spare_t) * t_pad + T + s2 % spare_t
    slot_dest = jnp.where(valid, (slot_src % TOP_K) * t_pad + slot_src // TOP_K, spare)
    return slot_token, slot_dest, block_e, n_active.reshape(1)


def moe_block(x_parts, xb_parts, xp_parts, p, layer):
    T = sum(x.shape[0] for x in x_parts)
    logits = jnp.concatenate([router_logits(x, p['router_w'], layer) for x in x_parts], axis=0)
    idx, wts = _route(logits, p['router_bias'][layer])
    plan = _dispatch_plan(idx, MOE_TM)
    y = routed_ffn(jnp.concatenate(xp_parts, axis=0), p['exp_w_gate'], p['exp_w_up'], p['exp_w_down'],
                   layer, plan, MOE_TM)
    y = y.reshape(TOP_K, y.shape[0] // TOP_K, y.shape[1])
    shared =[shared_ffn(xb, p['shared_w_gate'], p['shared_w_up'], p['shared_w_down'], layer)
              for xb in xb_parts]
    return y, wts, shared


def _rope_tables(pos, half):
    inv_freq = ROPE_THETA ** (-jnp.arange(half, dtype=F32) / half)
    ang = pos.astype(F32)[:, None] * inv_freq[None, :]
    return jnp.cos(ang), jnp.sin(ang)


def _even_prompt(x_bf, B, L, p, i):
    proj = matmul(x_bf, p['w_in_even'][i].astype(BF16), name="in_proj_even")
    proj3 = proj.reshape(B, L, -1)
    cos, sin = _rope_tables(jnp.arange(L, dtype=jnp.int32), RET_HEAD_DIM // 2)
    conv_out, conv_new = conv_prompt(proj3, p['conv_w'][i], p['conv_b'][i], p['conv_ln_g'][i], p['conv_ln_b'][i])
    ret_out, s_new = retention_prompt(proj3, cos, sin, p['ret_gn_g'][i], p['ret_gn_b'][i])
    mixed = jnp.concatenate([conv_out, ret_out], axis=-1).reshape(B * L, -1)
    return matmul(mixed, p['w_out_even'][i].astype(BF16), name="out_proj_even"), conv_new, s_new


def _even_sample(x_bf, conv_buf, ret_state, p, i):
    B = x_bf.shape[0]
    H, d = RET_HEADS, RET_HEAD_DIM
    proj = matmul(x_bf, p['w_in_even'][i].astype(BF16), name="in_proj_even")
    conv_out, conv_new = conv_sample(proj, conv_buf, p['conv_w'][i], p['conv_b'][i],
                                     p['conv_ln_g'][i], p['conv_ln_b'][i])
    o2 = 2 * CONV_CH
    q_col = proj[:, o2:o2 + RET_WIDTH].reshape(B, H, d, 1)
    k_col = proj[:, o2 + RET_WIDTH:o2 + 2 * RET_WIDTH].reshape(B, H, d, 1)
    v_row = proj[:, o2 + 2 * RET_WIDTH:o2 + 3 * RET_WIDTH].reshape(B, 1, RET_WIDTH)
    g_row = proj[:, o2 + 3 * RET_WIDTH:].reshape(B, 1, RET_WIDTH)
    cos, sin = _rope_tables(jnp.full((1,), PAST_LEN, jnp.int32), d // 2)
    ret_out, s_new = retention_sample(q_col, k_col, v_row, g_row, ret_state, cos.reshape(d // 2, 1),
                                      sin.reshape(d // 2, 1), p['ret_gn_g'][i], p['ret_gn_b'][i])
    mixed = jnp.concatenate([conv_out, ret_out.reshape(B, RET_WIDTH)], axis=-1)
    return matmul(mixed, p['w_out_even'][i].astype(BF16), name="out_proj_even"), conv_new, s_new


def _att_rope_tables(pos, rows):
    cos, sin = _rope_tables(pos, ATT_HEAD_DIM // 2)
    cos = jnp.tile(cos, (rows // cos.shape[0], LANES // cos.shape[1]))
    sin = jnp.tile(sin, (rows // sin.shape[0], LANES // sin.shape[1]))
    return cos, sin


def _odd_prompt(x_bf, B, L, p, i):
    cos, sin = _att_rope_tables(jnp.arange(L, dtype=jnp.int32), L)
    qkv = matmul(x_bf, p['w_qkv_odd'][i].astype(BF16), bias=p['b_qkv_odd'][i],
                 rope=(cos, sin, Q_WIDTH + KV_WIDTH), name="qkv_proj_odd")
    qkv3 = qkv.reshape(B, L, -1)
    o = swa_prompt(qkv3, p['sinks'][i])
    out = matmul(o.reshape(B * L, Q_WIDTH), p['w_out_odd'][i].astype(BF16), bias=p['b_out_odd'][i],
                 name="out_proj_odd")
    k_new = qkv3[:, L - WINDOW:, Q_WIDTH:Q_WIDTH + KV_WIDTH].reshape(B, WINDOW, N_KV_HEADS, ATT_HEAD_DIM)
    v_new = qkv3[:, L - WINDOW:, Q_WIDTH + KV_WIDTH:].reshape(B, WINDOW, N_KV_HEADS, ATT_HEAD_DIM)
    return out, k_new, v_new


def _odd_sample(x_bf, k_buf, v_buf, p, i):
    B = x_bf.shape[0]
    cos, sin = _att_rope_tables(jnp.full((1,), PAST_LEN, jnp.int32), B)
    qkv = matmul(x_bf, p['w_qkv_odd'][i].astype(BF16), bias=p['b_qkv_odd'][i],
                 rope=(cos, sin, Q_WIDTH + KV_WIDTH), name="qkv_proj_odd")
    q3 = qkv[:, :Q_WIDTH].reshape(B, N_HEADS, ATT_HEAD_DIM)
    kn3 = qkv[:, Q_WIDTH:Q_WIDTH + KV_WIDTH].reshape(B, N_KV_HEADS, ATT_HEAD_DIM)
    vn3 = qkv[:, Q_WIDTH + KV_WIDTH:].reshape(B, N_KV_HEADS, ATT_HEAD_DIM)
    o, k_new, v_new = swa_sample(q3, kn3, vn3, k_buf.reshape(B, WINDOW, KV_WIDTH),
                                 v_buf.reshape(B, WINDOW, KV_WIDTH), p['sinks'][i])
    out = matmul(o.reshape(B, Q_WIDTH), p['w_out_odd'][i].astype(BF16), bias=p['b_out_odd'][i],
                 name="out_proj_odd")
    shape = (B, WINDOW, N_KV_HEADS, ATT_HEAD_DIM)
    return out, k_new.reshape(shape), v_new.reshape(shape)


def kernel(x_prompt, x_sample, state_conv, state_ret, state_win_k, state_win_v,
           w_in_even, conv_w, conv_b, conv_ln_g, conv_ln_b, ret_gn_g, ret_gn_b, w_out_even,
           w_qkv_odd, b_qkv_odd, sinks, w_out_odd, b_out_odd, ln_g, ln_b,
           router_w, router_bias, exp_w_gate, exp_w_up, exp_w_down,
           shared_w_gate, shared_w_up, shared_w_down):
    p = dict(w_in_even=w_in_even, conv_w=conv_w, conv_b=conv_b, conv_ln_g=conv_ln_g, conv_ln_b=conv_ln_b,
             ret_gn_g=ret_gn_g, ret_gn_b=ret_gn_b, w_out_even=w_out_even,
             w_qkv_odd=w_qkv_odd, b_qkv_odd=b_qkv_odd, sinks=sinks, w_out_odd=w_out_odd, b_out_odd=b_out_odd,
             ln_g=ln_g, ln_b=ln_b, router_w=router_w, router_bias=router_bias,
             exp_w_gate=exp_w_gate, exp_w_up=exp_w_up, exp_w_down=exp_w_down,
             shared_w_gate=shared_w_gate, shared_w_up=shared_w_up, shared_w_down=shared_w_down)
    B, L, D = x_prompt.shape
    Bs = x_sample.shape[0]
    Tp = B * L
    xp = x_prompt.reshape(Tp, D)
    xs = x_sample.reshape(Bs, D)
    xp_bf, xs_bf = xp.astype(BF16), xs.astype(BF16)
    conv_p, conv_s, ret_p, ret_s, wk_p, wk_s, wv_p, wv_s = [], [], [], [], [], [], [], []
    for layer in range(DEPTH):
        i = layer // 2
        if layer % 2 == 0:
            hp, c_new, r_new = _even_prompt(xp_bf, B, L, p, i)
            conv_p.append(c_new)
            ret_p.append(r_new)
            hs, c_new, r_new = _even_sample(xs_bf, state_conv[i], state_ret[i], p, i)
            conv_s.append(c_new)
            ret_s.append(r_new)
        else:
            hp, k_new, v_new = _odd_prompt(xp_bf, B, L, p, i)
            wk_p.append(k_new)
            wv_p.append(v_new)
            hs, k_new, v_new = _odd_sample(xs_bf, state_win_k[i], state_win_v[i], p, i)
            wk_s.append(k_new)
            wv_s.append(v_new)
        xp, xp_bf, xp_pk = deepnorm(xp, [(hp, 0)], ln_g[layer, 0], ln_b[layer, 0], name="ln_mixer_prompt")
        xs, xs_bf, xs_pk = deepnorm(xs, [(hs, 0)], ln_g[layer, 0], ln_b[layer, 0], name="ln_mixer_sample")
        y, wts, shared = moe_block([xp, xs], [xp_bf, xs_bf], [xp_pk, xs_pk], p, layer)
        xp, xp_bf, _ = deepnorm(xp, [(shared[0], 0)], ln_g[layer, 1], ln_b[layer, 1], combine=(y, wts, 0),
                                tm=64, name="ln_moe_prompt")
        xs, xs_bf, _ = deepnorm(xs, [(shared[1], 0)], ln_g[layer, 1], ln_b[layer, 1], combine=(y, wts, Tp),
                                tm=32, name="ln_moe_sample")
    return (xp.reshape(B, L, D), xs.reshape(Bs, 1, D),
            jnp.stack(conv_p), jnp.stack(conv_s), jnp.stack(ret_p), jnp.stack(ret_s),
            jnp.stack(wk_p), jnp.stack(wk_s), jnp.stack(wv_p), jnp.stack(wv_s))
```

```python
import functools

import jax
import jax.numpy as jnp
import numpy as np
from jax import lax
from jax.experimental import pallas as pl
from jax.experimental.pallas import tpu as pltpu

F32 = jnp.float32
BF16 = jnp.bfloat16

D_MODEL = 4096
DEPTH = 2
PAST_LEN = 16384
CONV_CH = D_MODEL // 2
CONV_WIDTH = 31
CONV_HALO = 32
RET_HEADS = 8
RET_HEAD_DIM = 256
RET_WIDTH = RET_HEADS * RET_HEAD_DIM
RET_CHUNK = 128
ATT_HEAD_DIM = 64
N_HEADS = 64
N_KV_HEADS = 8
GROUP = N_HEADS // N_KV_HEADS
WINDOW = 128
Q_WIDTH = N_HEADS * ATT_HEAD_DIM
KV_WIDTH = N_KV_HEADS * ATT_HEAD_DIM
ROPE_THETA = 10000.0
N_EXPERTS = 128
TOP_K = 8
N_GROUPS = 8
TOPK_GROUPS = 4
EXPERTS_PER_GROUP = N_EXPERTS // N_GROUPS
EXPERT_FF = 512
ROUTED_SCALE = 2.5
DN_ALPHA = (2.0 * DEPTH) ** 0.25
LN_EPS = 1e-5

LANES = 128
V7X_VMEM_LIMIT = 52 * 1024 * 1024
V7X_VMEM_ROUTED = 58 * 1024 * 1024
MOE_TM = 256


def _params(*sem):
    return pltpu.CompilerParams(dimension_semantics=sem, vmem_limit_bytes=V7X_VMEM_LIMIT)


def _silu(x):
    return x * jax.nn.sigmoid(x)


def _rot_half_64(x):
    n = x.shape[-1]
    lane = lax.broadcasted_iota(jnp.int32, x.shape, x.ndim - 1)
    fwd = pltpu.roll(x, n - ATT_HEAD_DIM // 2, x.ndim - 1)
    bwd = pltpu.roll(x, ATT_HEAD_DIM // 2, x.ndim - 1)
    return jnp.where(lane % ATT_HEAD_DIM < ATT_HEAD_DIM // 2, -fwd, bwd)


def _mm_body(*refs, has_bias, rope_tiles):
    x_ref, w_ref = refs[0], refs[1]
    pos = 2
    b_ref = None
    if has_bias:
        b_ref = refs[pos]
        pos += 1
    if rope_tiles:
        cos_ref, sin_ref = refs[pos], refs[pos + 1]
        pos += 2
    o_ref = refs[pos]
    acc = jnp.dot(x_ref[...].astype(BF16), w_ref[...].astype(BF16), preferred_element_type=F32)
    if has_bias:
        acc = acc + b_ref[...]
    if rope_tiles:
        reps = acc.shape[1] // LANES
        cos = jnp.concatenate([cos_ref[...]] * reps, axis=1)
        sin = jnp.concatenate([sin_ref[...]] * reps, axis=1)
        @pl.when(pl.program_id(1) < rope_tiles)
        def _():
            o_ref[...] = (acc * cos + _rot_half_64(acc) * sin).astype(o_ref.dtype)

        @pl.when(pl.program_id(1) >= rope_tiles)
        def _():
            o_ref[...] = acc.astype(o_ref.dtype)
    else:
        o_ref[...] = acc.astype(o_ref.dtype)


def matmul(x, w, bias=None, rope=None, out_dtype=F32, tm=1024, tn=512, name="matmul"):
    M, K = x.shape
    N = w.shape[1]
    tm = min(tm, M)
    tn = min(tn, N)
    assert M % tm == 0 and N % tn == 0
    in_specs = [pl.BlockSpec((tm, K), lambda i, j: (i, 0)),
                pl.BlockSpec((K, tn), lambda i, j: (0, j))]
    args = [x, w]
    if bias is not None:
        in_specs.append(pl.BlockSpec((1, tn), lambda i, j: (0, j)))
        args.append(bias.reshape(1, N))
    rope_tiles = 0
    if rope is not None:
        cos, sin, n_cols = rope
        assert n_cols % tn == 0 and cos.shape[0] % tm == 0
        rope_tiles = n_cols // tn
        nrb = cos.shape[0] // tm
        in_specs += [pl.BlockSpec((tm, LANES), lambda i, j: (i % nrb, 0))] * 2
        args += [cos, sin]
    return pl.pallas_call(
        functools.partial(_mm_body, has_bias=bias is not None, rope_tiles=rope_tiles),
        grid=(M // tm, N // tn),
        in_specs=in_specs,
        out_specs=pl.BlockSpec((tm, tn), lambda i, j: (i, j)),
        out_shape=jax.ShapeDtypeStruct((M, N), out_dtype),
        compiler_params=_params("parallel", "arbitrary"),
        name=name,
    )(*args)


def _split_bf16(x):
    hi = x.astype(BF16)
    lo = (x - hi.astype(F32)).astype(BF16)
    return hi, lo


def _first_argmax(x, row, n):
    m = jnp.max(x, axis=0, keepdims=True)
    i = jnp.min(jnp.where(x == m, row, n), axis=0, keepdims=True)
    return m, i


def _router_body(x_ref, wt_ref, bias_ref, idx_ref, wts_ref):
    nt = (((1,), (1,)), ((), ()))
    xh, xl = _split_bf16(x_ref[...])
    wh, wl = _split_bf16(wt_ref[0])
    logits = lax.dot_general(wh, xh, nt, preferred_element_type=F32)
    logits = logits + lax.dot_general(wh, xl, nt, preferred_element_type=F32)
    logits = logits + lax.dot_general(wl, xh, nt, preferred_element_type=F32)
    scores = jax.nn.sigmoid(logits)
    biased = scores + bias_ref[0]
    tm = scores.shape[1]
    neg = -jnp.inf
    row_g = lax.broadcasted_iota(jnp.int32, (EXPERTS_PER_GROUP, tm), 0)
    grp = []
    for g in range(N_GROUPS):
        xg = biased[g * EXPERTS_PER_GROUP:(g + 1) * EXPERTS_PER_GROUP]
        m1, i1 = _first_argmax(xg, row_g, EXPERTS_PER_GROUP)
        m2 = jnp.max(jnp.where(row_g == i1, neg, xg), axis=0, keepdims=True)
        grp.append(m1 + m2)
    gs = jnp.concatenate(grp, axis=0)
    row_n = lax.broadcasted_iota(jnp.int32, (N_GROUPS, tm), 0)
    chosen = jnp.zeros((N_GROUPS, tm), F32)
    for _ in range(TOPK_GROUPS):
        _, i = _first_argmax(gs, row_n, N_GROUPS)
        hit = row_n == i
        chosen = jnp.where(hit, 1.0, chosen)
        gs = jnp.where(hit, neg, gs)
    emask = jnp.concatenate([jnp.broadcast_to(chosen[g:g + 1], (EXPERTS_PER_GROUP, tm))
                             for g in range(N_GROUPS)], axis=0)
    cand = jnp.where(emask > 0.0, biased, neg)
    row_e = lax.broadcasted_iota(jnp.int32, (N_EXPERTS, tm), 0)
    ids, ws = [], []
    for _ in range(TOP_K):
        _, i = _first_argmax(cand, row_e, N_EXPERTS)
        hit = row_e == i
        ids.append(i)
        ws.append(jnp.sum(jnp.where(hit, scores, 0.0), axis=0, keepdims=True))
        cand = jnp.where(hit, neg, cand)
    w = jnp.concatenate(ws, axis=0)
    idx_ref[...] = jnp.concatenate(ids, axis=0)
    wts_ref[...] = w / jnp.sum(w, axis=0, keepdims=True) * ROUTED_SCALE


def route(x, w_t, bias_col, layer, tm=512):
    M, K = x.shape
    E = w_t.shape[1]
    tm = min(tm, M)
    return pl.pallas_call(
        _router_body,
        grid=(M // tm,),
        in_specs=[pl.BlockSpec((tm, K), lambda i: (i, 0)),
                  pl.BlockSpec((1, E, K), lambda i: (layer, 0, 0)),
                  pl.BlockSpec((1, E, 1), lambda i: (layer, 0, 0))],
        out_specs=[pl.BlockSpec((TOP_K, tm), lambda i: (0, i))] * 2,
        out_shape=[jax.ShapeDtypeStruct((TOP_K, M), jnp.int32), jax.ShapeDtypeStruct((TOP_K, M), F32)],
        compiler_params=_params("parallel"),
        name="router",
    )(x, w_t, bias_col)


def _layer_norm_rows(z, g, b):
    mu = jnp.mean(z, axis=-1, keepdims=True)
    zc = z - mu
    var = jnp.mean(zc * zc, axis=-1, keepdims=True)
    return zc * lax.rsqrt(var + LN_EPS) * g + b


def _unpack_pair(w):
    lo = lax.bitcast_convert_type(w << 16, F32)
    hi = lax.bitcast_convert_type(w & jnp.uint32(0xFFFF0000), F32)
    return lo, hi


def _pack_pair(lo, hi):
    lo_bits = lax.bitcast_convert_type(lo.astype(BF16).astype(F32), jnp.uint32) >> 16
    hi_bits = lax.bitcast_convert_type(hi.astype(BF16).astype(F32), jnp.uint32) & jnp.uint32(0xFFFF0000)
    return hi_bits | lo_bits


TOK_ROWS = D_MODEL // 2 // LANES


def _load_token_rows(ref, s, tm):
    return ref[pl.ds(s, tm, stride=TOK_ROWS), :]


def _store_token_rows(ref, y, tm):
    half = y.shape[1] // 2
    for s in range(TOK_ROWS):
        words = _pack_pair(y[:, s * LANES:(s + 1) * LANES], y[:, half + s * LANES:half + (s + 1) * LANES])
        ref[pl.ds(s, tm, stride=TOK_ROWS), :] = words


def _ln_body(*refs, n_h, combine):
    x_ref = refs[0]
    h_refs = refs[1:1 + n_h]
    pos = 1 + n_h
    if combine:
        y_ref, w_ref = refs[pos], refs[pos + 1]
        pos += 2
    g_ref, b_ref, o_ref, ob_ref, op_ref = refs[pos:]
    tm, d = x_ref.shape
    half = d // 2
    z = DN_ALPHA * x_ref[...]
    for h_ref in h_refs:
        z = z + h_ref[...]
    if combine:
        wk = [jnp.broadcast_to(w_ref[:, k:k + 1], (tm, LANES)) for k in range(TOP_K)]
        cols_lo, cols_hi = [], []
        for s in range(TOK_ROWS):
            acc_lo = z[:, s * LANES:(s + 1) * LANES]
            acc_hi = z[:, half + s * LANES:half + (s + 1) * LANES]
            for k in range(TOP_K):
                lo, hi = _unpack_pair(_load_token_rows(y_ref.at[k], s, tm))
                acc_lo = acc_lo + wk[k] * lo
                acc_hi = acc_hi + wk[k] * hi
            cols_lo.append(acc_lo)
            cols_hi.append(acc_hi)
        z = jnp.concatenate(cols_lo + cols_hi, axis=1)
    y = _layer_norm_rows(z, g_ref[...], b_ref[...])
    o_ref[...] = y
    ob_ref[...] = y.astype(BF16)
    _store_token_rows(op_ref, y, tm)


def deepnorm(x, hs, g, b, combine=None, tm=256, name="deepnorm"):
    M, D = x.shape
    tm = min(tm, M)
    assert M % tm == 0
    row_spec = lambda width, off: pl.BlockSpec((tm, width), functools.partial(lambda i, o: (i + o, 0), o=off // tm))
    in_specs = [row_spec(D, 0)]
    args = [x]
    for h, off in hs:
        assert off % tm == 0
        in_specs.append(row_spec(D, off))
        args.append(h)
    if combine is not None:
        y, wts, off = combine
        assert off % tm == 0
        y_spec = pl.BlockSpec((TOP_K, tm * TOK_ROWS, LANES),
                              functools.partial(lambda i, o: (0, i + o, 0), o=off // tm))
        in_specs += [y_spec, row_spec(TOP_K, off)]
        args += [y, wts]
    in_specs += [pl.BlockSpec((1, D), lambda i: (0, 0))] * 2
    args += [g.reshape(1, D), b.reshape(1, D)]
    return pl.pallas_call(
        functools.partial(_ln_body, n_h=len(hs), combine=combine is not None),
        grid=(M // tm,),
        in_specs=in_specs,
        out_specs=[row_spec(D, 0), row_spec(D, 0), pl.BlockSpec((tm * TOK_ROWS, LANES), lambda i: (i, 0))],
        out_shape=[jax.ShapeDtypeStruct((M, D), F32), jax.ShapeDtypeStruct((M, D), BF16),
                   jax.ShapeDtypeStruct((M * TOK_ROWS, LANES), jnp.uint32)],
        compiler_params=_params("parallel"),
        name=name,
    )(*args)


def _conv_prompt_body(a_ref, g_ref, pa_ref, pg_ref, w_ref, bdw_ref, lng_ref, lnb_ref,
                      o_ref, st_ref, full_ref, y_ref, *, tl):
    i = pl.program_id(1)
    u = a_ref[0] * jax.nn.sigmoid(g_ref[0])
    up = pa_ref[0] * jax.nn.sigmoid(pg_ref[0])
    up = jnp.where(i > 0, up, 0.0)
    full_ref[0:CONV_HALO, :] = up
    full_ref[CONV_HALO:CONV_HALO + tl, :] = u
    base = CONV_HALO - (CONV_WIDTH - 1)
    n_col = u.shape[1] // LANES

    def col_step(c, carry):
        off = pl.multiple_of(c * LANES, LANES)
        acc = jnp.zeros((tl, LANES), F32)
        for j in range(CONV_WIDTH):
            acc = acc + w_ref[j:j + 1, pl.ds(off, LANES)] * full_ref[base + j:base + j + tl, pl.ds(off, LANES)]
        y_ref[:, pl.ds(off, LANES)] = acc
        return carry

    lax.fori_loop(0, n_col, col_step, 0)
    y = _layer_norm_rows(y_ref[...] + bdw_ref[...], lng_ref[...], lnb_ref[...])
    o_ref[0] = _silu(y).astype(o_ref.dtype)

    @pl.when(i == pl.num_programs(1) - 1)
    def _():
        st_ref[0] = full_ref[CONV_HALO + tl - (CONV_WIDTH - 1):CONV_HALO + tl, :]


def conv_prompt(proj3, w_dw, b_dw, ln_g, ln_b, tl=128):
    B, L, _ = proj3.shape
    C = CONV_CH
    nh = tl // CONV_HALO
    vec = lambda v: v.reshape(1, C)
    return pl.pallas_call(
        functools.partial(_conv_prompt_body, tl=tl),
        grid=(B, L // tl),
        in_specs=[
            pl.BlockSpec((1, tl, C), lambda b, i: (b, i, 0)),
            pl.BlockSpec((1, tl, C), lambda b, i: (b, i, 1)),
            pl.BlockSpec((1, CONV_HALO, C), lambda b, i: (b, jnp.maximum(i * nh - 1, 0), 0)),
            pl.BlockSpec((1, CONV_HALO, C), lambda b, i: (b, jnp.maximum(i * nh - 1, 0), 1)),
            pl.BlockSpec((CONV_WIDTH, C), lambda b, i: (0, 0)),
            pl.BlockSpec((1, C), lambda b, i: (0, 0)),
            pl.BlockSpec((1, C), lambda b, i: (0, 0)),
            pl.BlockSpec((1, C), lambda b, i: (0, 0)),
        ],
        out_specs=[
            pl.BlockSpec((1, tl, C), lambda b, i: (b, i, 0)),
            pl.BlockSpec((1, CONV_WIDTH - 1, C), lambda b, i: (b, 0, 0)),
        ],
        out_shape=[jax.ShapeDtypeStruct((B, L, C), BF16),
                   jax.ShapeDtypeStruct((B, CONV_WIDTH - 1, C), F32)],
        scratch_shapes=[pltpu.VMEM((CONV_HALO + tl, C), F32), pltpu.VMEM((tl, C), F32)],
        compiler_params=_params("parallel", "arbitrary"),
        name="conv_prompt",
    )(proj3, proj3, proj3, proj3, w_dw, vec(b_dw), vec(ln_g), vec(ln_b))


def _conv_sample_body(a_ref, g_ref, buf_ref, w_ref, bdw_ref, lng_ref, lnb_ref, o_ref, st_ref):
    u = a_ref[...] * jax.nn.sigmoid(g_ref[...])
    nb = CONV_WIDTH - 1
    acc = w_ref[nb:nb + 1, :] * u
    for j in range(nb):
        acc = acc + w_ref[j:j + 1, :] * buf_ref[:, j, :]
    y = _layer_norm_rows(acc + bdw_ref[...], lng_ref[...], lnb_ref[...])
    o_ref[...] = _silu(y).astype(o_ref.dtype)
    st_ref[:, 0:nb - 1, :] = buf_ref[:, 1:nb, :]
    st_ref[:, nb - 1, :] = u


def conv_sample(proj, buf, w_dw, b_dw, ln_g, ln_b, tb=8):
    B = proj.shape[0]
    C = CONV_CH
    nb = CONV_WIDTH - 1
    vec = lambda v: v.reshape(1, C)
    return pl.pallas_call(
        _conv_sample_body,
        grid=(B // tb,),
        in_specs=[
            pl.BlockSpec((tb, C), lambda b: (b, 0)),
            pl.BlockSpec((tb, C), lambda b: (b, 1)),
            pl.BlockSpec((tb, nb, C), lambda b: (b, 0, 0)),
            pl.BlockSpec((CONV_WIDTH, C), lambda b: (0, 0)),
            pl.BlockSpec((1, C), lambda b: (0, 0)),
            pl.BlockSpec((1, C), lambda b: (0, 0)),
            pl.BlockSpec((1, C), lambda b: (0, 0)),
        ],
        out_specs=[pl.BlockSpec((tb, C), lambda b: (b, 0)),
                   pl.BlockSpec((tb, nb, C), lambda b: (b, 0, 0))],
        out_shape=[jax.ShapeDtypeStruct((B, C), BF16), jax.ShapeDtypeStruct((B, nb, C), F32)],
        compiler_params=_params("parallel"),
        name="conv_sample",
    )(proj, proj, buf, w_dw, vec(b_dw), vec(ln_g), vec(ln_b))


def _ret_log_gamma():
    return np.log1p(-np.exp2(-5.0 - np.arange(RET_HEADS, dtype=np.float64)))


def _group_norm_gate(o, gate, gn_g, gn_b):
    mu = jnp.mean(o, axis=-1, keepdims=True)
    oc = o - mu
    var = jnp.mean(oc * oc, axis=-1, keepdims=True)
    return (oc * lax.rsqrt(var + LN_EPS) * gn_g + gn_b) * _silu(gate)


def _rope_256(x, cos, sin):
    half = RET_HEAD_DIM // 2
    x1, x2 = x[..., :half], x[..., half:]
    return jnp.concatenate([x1 * cos - x2 * sin, x2 * cos + x1 * sin], axis=-1)


def _ret_prompt_body(q_ref, k_ref, v_ref, g_ref, cos_ref, sin_ref, intra_ref, qd_ref, kd_ref, cd_ref,
                     gng_ref, gnb_ref, o_ref, s_ref):
    c = pl.program_id(2)

    @pl.when(c == 0)
    def _():
        s_ref[...] = jnp.zeros_like(s_ref)

    cos, sin = cos_ref[...], sin_ref[...]
    q = _rope_256(q_ref[0], cos, sin)
    k = _rope_256(k_ref[0], cos, sin) * (RET_HEAD_DIM ** -0.5)
    vb = v_ref[0].astype(BF16)
    s_prev = s_ref[0, 0]
    scores = lax.dot_general(q.astype(BF16), k.astype(BF16), (((1,), (1,)), ((), ())),
                             preferred_element_type=F32) * intra_ref[0]
    o = jnp.dot(scores.astype(BF16), vb, preferred_element_type=F32)
    o = o + jnp.dot((q * qd_ref[0]).astype(BF16), s_prev.astype(BF16), preferred_element_type=F32)
    kdt = (k * kd_ref[0]).T.astype(BF16)
    kv = jnp.dot(kdt, vb, preferred_element_type=F32)
    s_ref[0, 0] = cd_ref[0] * s_prev + kv
    o_ref[0] = _group_norm_gate(o, g_ref[0], gng_ref[...], gnb_ref[...]).astype(o_ref.dtype)


def retention_prompt(proj3, cos, sin, gn_g, gn_b):
    B, L, _ = proj3.shape
    H, d, c = RET_HEADS, RET_HEAD_DIM, RET_CHUNK
    lg = _ret_log_gamma()
    idx = np.arange(c, dtype=np.float64)
    rel = idx[:, None] - idx[None, :]
    intra = np.where(rel[None] >= 0, np.exp(lg[:, None, None] * np.maximum(rel, 0.0)[None]), 0.0)
    ones = np.ones((1, 1, d))
    qd = np.exp(lg[:, None] * (idx + 1.0)[None, :])[:, :, None] * ones
    kd = np.exp(lg[:, None] * (c - 1.0 - idx)[None, :])[:, :, None] * ones
    cd = np.exp(lg * c)[:, None, None] * ones
    col0 = 2 * CONV_CH // d
    head_spec = lambda off: pl.BlockSpec((1, c, d), lambda b, h, n: (b, n, col0 + off * H + h))
    tab_spec = pl.BlockSpec((1, c, d), lambda b, h, n: (h, 0, 0))
    return pl.pallas_call(
        _ret_prompt_body,
        grid=(B, H, L // c),
        in_specs=[
            head_spec(0), head_spec(1), head_spec(2), head_spec(3),
            pl.BlockSpec((c, d // 2), lambda b, h, n: (n, 0)),
            pl.BlockSpec((c, d // 2), lambda b, h, n: (n, 0)),
            pl.BlockSpec((1, c, c), lambda b, h, n: (h, 0, 0)),
            tab_spec, tab_spec,
            pl.BlockSpec((1, 1, d), lambda b, h, n: (h, 0, 0)),
            pl.BlockSpec((1, d), lambda b, h, n: (0, h)),
            pl.BlockSpec((1, d), lambda b, h, n: (0, h)),
        ],
        out_specs=[
            pl.BlockSpec((1, c, d), lambda b, h, n: (b, n, h)),
            pl.BlockSpec((1, 1, d, d), lambda b, h, n: (b, h, 0, 0)),
        ],
        out_shape=[jax.ShapeDtypeStruct((B, L, H * d), BF16),
                   jax.ShapeDtypeStruct((B, H, d, d), F32)],
        compiler_params=_params("parallel", "parallel", "arbitrary"),
        name="retention_prompt",
    )(proj3, proj3, proj3, proj3, cos, sin,
      jnp.asarray(intra, F32), jnp.asarray(qd, F32), jnp.asarray(kd, F32), jnp.asarray(cd, F32),
      gn_g.reshape(1, H * d), gn_b.reshape(1, H * d))


def _ret_sample_body(qc_ref, kc_ref, v_ref, g_ref, s_ref, cosc_ref, sinc_ref, gam_ref, gng_ref, gnb_ref,
                     o_ref, so_ref):
    half = RET_HEAD_DIM // 2
    cos, sin = cosc_ref[...], sinc_ref[...]

    def rope_col(x):
        x1, x2 = x[:half], x[half:]
        return jnp.concatenate([x1 * cos - x2 * sin, x2 * cos + x1 * sin], axis=0)

    for h in range(RET_HEADS):
        lo, hi = h * RET_HEAD_DIM, (h + 1) * RET_HEAD_DIM
        q = rope_col(qc_ref[0, h])
        k = rope_col(kc_ref[0, h]) * (RET_HEAD_DIM ** -0.5)
        v = v_ref[0, :, lo:hi]
        s_new = gam_ref[h] * s_ref[0, h] + k * v
        so_ref[0, h] = s_new
        o = jnp.sum(q * s_new, axis=0, keepdims=True)
        o_ref[0, :, lo:hi] = _group_norm_gate(o, g_ref[0, :, lo:hi], gng_ref[:, lo:hi],
                                              gnb_ref[:, lo:hi]).astype(o_ref.dtype)


def retention_sample(q_col, k_col, v_row, g_row, state, cos_col, sin_col, gn_g, gn_b):
    B = state.shape[0]
    H, d = RET_HEADS, RET_HEAD_DIM
    gam = jnp.asarray(np.exp(_ret_log_gamma()), F32)
    col_spec = pl.BlockSpec((1, H, d, 1), lambda b: (b, 0, 0, 0))
    row_spec = pl.BlockSpec((1, 1, H * d), lambda b: (b, 0, 0))
    return pl.pallas_call(
        _ret_sample_body,
        grid=(B,),
        in_specs=[
            col_spec, col_spec, row_spec, row_spec,
            pl.BlockSpec((1, H, d, d), lambda b: (b, 0, 0, 0)),
            pl.BlockSpec((d // 2, 1), lambda b: (0, 0)),
            pl.BlockSpec((d // 2, 1), lambda b: (0, 0)),
            pl.BlockSpec(memory_space=pltpu.SMEM),
            pl.BlockSpec((1, H * d), lambda b: (0, 0)),
            pl.BlockSpec((1, H * d), lambda b: (0, 0)),
        ],
        out_specs=[row_spec, pl.BlockSpec((1, H, d, d), lambda b: (b, 0, 0, 0))],
        out_shape=[jax.ShapeDtypeStruct((B, 1, H * d), BF16), jax.ShapeDtypeStruct((B, H, d, d), F32)],
        compiler_params=_params("parallel"),
        name="retention_sample",
    )(q_col, k_col, v_row, g_row, state, cos_col, sin_col, gam, gn_g.reshape(1, H * d), gn_b.reshape(1, H * d))


def _pair_operand(x2, h, lane):
    col = x2[:, (h // 2) * LANES:(h // 2 + 1) * LANES]
    swapped = pltpu.roll(col, ATT_HEAD_DIM, 1)
    first, second = (col, swapped) if h % 2 == 0 else (swapped, col)
    return jnp.concatenate([jnp.where(lane < ATT_HEAD_DIM, first, 0.0),
                            jnp.where(lane >= ATT_HEAD_DIM, second, 0.0)], axis=0).astype(BF16)


def _swa_prompt_body(sink_ref, q_ref, kc_ref, kp_ref, vc_ref, vp_ref, o_ref):
    j = pl.program_id(1)
    W = WINDOW
    k2 = jnp.concatenate([kp_ref[0], kc_ref[0]], axis=0)
    v2 = jnp.concatenate([vp_ref[0], vc_ref[0]], axis=0)
    row = lax.broadcasted_iota(jnp.int32, (W, 2 * W), 0)
    col = lax.broadcasted_iota(jnp.int32, (W, 2 * W), 1)
    first_key = jnp.where(j > 0, 0, W)
    mask = (col >= jnp.maximum(row, first_key)) & (col <= W + row)
    lane_kv = lax.broadcasted_iota(jnp.int32, (2 * W, LANES), 1)
    lane_o = lax.broadcasted_iota(jnp.int32, (W, LANES), 1)
    scale = ATT_HEAD_DIM ** -0.5
    for h in range(N_KV_HEADS):
        kk = _pair_operand(k2, h, lane_kv)
        vv = _pair_operand(v2, h, lane_kv)
        for p in range(GROUP // 2):
            c0 = h * GROUP * ATT_HEAD_DIM + p * LANES
            qp = q_ref[0, :, c0:c0 + LANES].astype(BF16)
            s = lax.dot_general(qp, kk, (((1,), (1,)), ((), ())), preferred_element_type=F32) * scale
            probs, inv = [], []
            for t in range(2):
                sink = sink_ref[h * GROUP + 2 * p + t]
                st = jnp.where(mask, s[:, t * 2 * W:(t + 1) * 2 * W], -jnp.inf)
                m = jnp.maximum(jnp.max(st, axis=-1, keepdims=True), sink)
                e = jnp.exp(st - m)
                denom = jnp.sum(e, axis=-1, keepdims=True) + jnp.exp(sink - m)
                probs.append(e.astype(BF16))
                inv.append(1.0 / denom)
            o2 = jnp.dot(jnp.concatenate(probs, axis=1), vv, preferred_element_type=F32)
            o2 = o2 * jnp.where(lane_o < ATT_HEAD_DIM, inv[0], inv[1])
            o_ref[0, :, c0:c0 + LANES] = o2.astype(o_ref.dtype)


def swa_prompt(qkv3, sinks):
    B, L, _ = qkv3.shape
    W = WINDOW
    kcol = Q_WIDTH // KV_WIDTH
    return pl.pallas_call(
        _swa_prompt_body,
        grid_spec=pltpu.PrefetchScalarGridSpec(
            num_scalar_prefetch=0,
            grid=(B, L // W),
            in_specs=[
                pl.BlockSpec(memory_space=pltpu.SMEM),
                pl.BlockSpec((1, W, Q_WIDTH), lambda b, j: (b, j, 0)),
                pl.BlockSpec((1, W, KV_WIDTH), lambda b, j: (b, j, kcol)),
                pl.BlockSpec((1, W, KV_WIDTH), lambda b, j: (b, jnp.maximum(j - 1, 0), kcol)),
                pl.BlockSpec((1, W, KV_WIDTH), lambda b, j: (b, j, kcol + 1)),
                pl.BlockSpec((1, W, KV_WIDTH), lambda b, j: (b, jnp.maximum(j - 1, 0), kcol + 1)),
            ],
            out_specs=pl.BlockSpec((1, W, Q_WIDTH), lambda b, j: (b, j, 0)),
        ),
        out_shape=jax.ShapeDtypeStruct((B, L, Q_WIDTH), BF16),
        compiler_params=_params("parallel", "parallel"),
        name="swa_prompt",
    )(sinks, qkv3, qkv3, qkv3, qkv3, qkv3)


def _swa_sample_body(q_ref, kn_ref, vn_ref, knf_ref, vnf_ref, kb_ref, vb_ref, sink_ref, o_ref, ko_ref, vo_ref):
    W = WINDOW
    scale = ATT_HEAD_DIM ** -0.5
    kb = kb_ref[0]
    vb = vb_ref[0]
    for h in range(N_KV_HEADS):
        lo, hi = h * ATT_HEAD_DIM, (h + 1) * ATT_HEAD_DIM
        qh = q_ref[0, h * GROUP:(h + 1) * GROUP, :]
        kn = kn_ref[0, h:h + 1, :]
        vn = vn_ref[0, h:h + 1, :]
        sink = sink_ref[h * GROUP:(h + 1) * GROUP, :]
        s_buf = lax.dot_general(qh.astype(BF16), kb[:, lo:hi].astype(BF16), (((1,), (1,)), ((), ())),
                                preferred_element_type=F32) * scale
        s_new = jnp.sum(qh * kn, axis=-1, keepdims=True) * scale
        m = jnp.maximum(jnp.maximum(jnp.max(s_buf, axis=-1, keepdims=True), s_new), sink)
        e_buf = jnp.exp(s_buf - m)
        e_new = jnp.exp(s_new - m)
        denom = jnp.sum(e_buf, axis=-1, keepdims=True) + e_new + jnp.exp(sink - m)
        o = jnp.dot(e_buf.astype(BF16), vb[:, lo:hi].astype(BF16), preferred_element_type=F32)
        o = o + e_new * vn
        o_ref[0, h * GROUP:(h + 1) * GROUP, :] = (o / denom).astype(o_ref.dtype)
    ko_ref[0, 0:W - 1, :] = kb[1:W, :]
    ko_ref[0, W - 1:W, :] = knf_ref[0]
    vo_ref[0, 0:W - 1, :] = vb[1:W, :]
    vo_ref[0, W - 1:W, :] = vnf_ref[0]


def swa_sample(q3, kn3, vn3, k_buf, v_buf, sinks):
    B = q3.shape[0]
    W, d = WINDOW, ATT_HEAD_DIM
    head_spec = lambda n: pl.BlockSpec((1, n, d), lambda b: (b, 0, 0))
    flat_spec = pl.BlockSpec((1, 1, KV_WIDTH), lambda b: (b, 0, 0))
    buf_spec = pl.BlockSpec((1, W, KV_WIDTH), lambda b: (b, 0, 0))
    return pl.pallas_call(
        _swa_sample_body,
        grid=(B,),
        in_specs=[head_spec(N_HEADS), head_spec(N_KV_HEADS), head_spec(N_KV_HEADS), flat_spec, flat_spec,
                  buf_spec, buf_spec, pl.BlockSpec((N_HEADS, 1), lambda b: (0, 0))],
        out_specs=[head_spec(N_HEADS), buf_spec, buf_spec],
        out_shape=[jax.ShapeDtypeStruct((B, N_HEADS, d), BF16),
                   jax.ShapeDtypeStruct((B, W, KV_WIDTH), F32),
                   jax.ShapeDtypeStruct((B, W, KV_WIDTH), F32)],
        compiler_params=_params("parallel"),
        name="swa_sample",
    )(q3, kn3, vn3, kn3.reshape(B, 1, KV_WIDTH), vn3.reshape(B, 1, KV_WIDTH), k_buf, v_buf,
      sinks.reshape(N_HEADS, 1))


def _shared_up_body(x_ref, wg_ref, wu_ref, h_ref):
    x = x_ref[...]
    g = jnp.dot(x, wg_ref[0].astype(BF16), preferred_element_type=F32)
    u = jnp.dot(x, wu_ref[0].astype(BF16), preferred_element_type=F32)
    h_ref[...] = (_silu(g) * u).astype(h_ref.dtype)


def _shared_down_body(h_ref, wd_ref, y_ref):
    y_ref[...] = jnp.dot(h_ref[...], wd_ref[0].astype(BF16), preferred_element_type=F32)


def shared_ffn(xb, w_gate, w_up, w_down, layer, tm=512):
    M, D = xb.shape
    FF = w_gate.shape[-1]
    tm = min(tm, M)
    wsel = lambda i: (layer, 0, 0)
    h = pl.pallas_call(
        _shared_up_body,
        grid=(M // tm,),
        in_specs=[pl.BlockSpec((tm, D), lambda i: (i, 0)),
                  pl.BlockSpec((1, D, FF), wsel), pl.BlockSpec((1, D, FF), wsel)],
        out_specs=pl.BlockSpec((tm, FF), lambda i: (i, 0)),
        out_shape=jax.ShapeDtypeStruct((M, FF), BF16),
        compiler_params=_params("parallel"),
        name="shared_up",
    )(xb, w_gate, w_up)
    return pl.pallas_call(
        _shared_down_body,
        grid=(M // tm,),
        in_specs=[pl.BlockSpec((tm, FF), lambda i: (i, 0)), pl.BlockSpec((1, FF, D), wsel)],
        out_specs=pl.BlockSpec((tm, D), lambda i: (i, 0)),
        out_shape=jax.ShapeDtypeStruct((M, D), F32),
        compiler_params=_params("parallel"),
        name="shared_down",
    )(h, w_down)


def _expert_changed(be_ref, i):
    return (i == 0) | (be_ref[i] != be_ref[jnp.maximum(i - 1, 0)])


GATHER_SLOTS = 3


def _moe_up_body(be_ref, na_ref, tok_ref, x_hbm, wg_ref, wu_ref, h_ref, xbuf, sem, x_lhs, wg_bf, wu_bf, *, tm):
    i = pl.program_id(0)
    na = na_ref[0]
    last_blk = pl.num_programs(0) - 1
    half = wg_bf.shape[0] // 2

    def issue(blk, slot):
        base = jnp.minimum(blk, last_blk) * tm
        for r in range(tm):
            src = pl.multiple_of(tok_ref[base + r], TOK_ROWS)
            pltpu.make_async_copy(x_hbm.at[pl.ds(src, TOK_ROWS)],
                                  xbuf.at[slot, pl.ds(r * TOK_ROWS, TOK_ROWS)], sem.at[slot]).start()

    def wait(slot):
        pltpu.make_async_copy(x_hbm.at[pl.ds(0, tm * TOK_ROWS)], xbuf.at[slot], sem.at[slot]).wait()

    @pl.when(i == 0)
    def _():
        issue(0, 0)
        issue(1, 1)

    @pl.when(i >= na)
    def _():
        h_ref[...] = jnp.zeros_like(h_ref)

    @pl.when(i < na)
    def _():
        slot = i % GATHER_SLOTS

        @pl.when(_expert_changed(be_ref, i))
        def _():
            wg_bf[...] = wg_ref[0, 0].astype(BF16)
            wu_bf[...] = wu_ref[0, 0].astype(BF16)

        wait(slot)
        for s in range(TOK_ROWS):
            lo, hi = _unpack_pair(_load_token_rows(xbuf.at[slot], s, tm))
            x_lhs[:, s * LANES:(s + 1) * LANES] = lo.astype(BF16)
            x_lhs[:, half + s * LANES:half + (s + 1) * LANES] = hi.astype(BF16)
        issue(i + 2, (i + 2) % GATHER_SLOTS)
        x = x_lhs[...]
        g = jnp.dot(x, wg_bf[...], preferred_element_type=F32)
        u = jnp.dot(x, wu_bf[...], preferred_element_type=F32)
        h_ref[...] = (_silu(g) * u).astype(h_ref.dtype)

        @pl.when(i + 1 == na)
        def _():
            wait((i + 1) % GATHER_SLOTS)
            wait((i + 2) % GATHER_SLOTS)


def _moe_down_body(be_ref, na_ref, dst_ref, h_ref, wd_ref, y_hbm, ybuf, sem, wd_bf, *, tm):
    i = pl.program_id(0)
    na = na_ref[0]

    def wait(slot):
        pltpu.make_async_copy(ybuf.at[slot], y_hbm.at[pl.ds(0, tm * TOK_ROWS)], sem.at[slot]).wait()

    def scatter(blk, slot):
        base = blk * tm
        for r in range(tm):
            dst = pl.multiple_of(dst_ref[base + r], TOK_ROWS)
            pltpu.make_async_copy(ybuf.at[slot, pl.ds(r * TOK_ROWS, TOK_ROWS)],
                                  y_hbm.at[pl.ds(dst, TOK_ROWS)], sem.at[slot]).start()

    def compute(slot):
        y = jnp.dot(h_ref[...], wd_bf[...], preferred_element_type=F32)
        _store_token_rows(ybuf.at[slot], y, tm)

    @pl.when(i == 0)
    def _():
        plane = y_hbm.shape[0] // TOP_K
        spare = 2 * tm // TOP_K * TOK_ROWS
        ybuf[0] = jnp.zeros(ybuf.shape[1:], ybuf.dtype)
        for k in range(TOP_K):
            fill = pltpu.make_async_copy(ybuf.at[0, pl.ds(0, spare)],
                                         y_hbm.at[pl.ds((k + 1) * plane - spare, spare)], sem.at[0])
            fill.start()
            fill.wait()

    @pl.when(i < na)
    def _():
        @pl.when(_expert_changed(be_ref, i))
        def _():
            wd_bf[...] = wd_ref[0, 0].astype(BF16)

        @pl.when(i == 0)
        def _():
            compute(0)

        @pl.when(i > 0)
        def _():
            slot = i % 2

            @pl.when(i >= 2)
            def _():
                wait(slot)

            scatter(i - 1, 1 - slot)
            compute(slot)

    @pl.when(i == pl.num_programs(0) - 1)
    def _():
        last = na - 1

        @pl.when(na >= 2)
        def _():
            wait(na % 2)

        scatter(last, last % 2)
        wait(last % 2)


def routed_ffn(x_packed, w_gate, w_up, w_down, layer, plan, tm):
    slot_token, slot_dest, block_e, n_active = plan
    T = x_packed.shape[0] // TOK_ROWS
    NB = block_e.shape[0]
    D, FF = w_gate.shape[-2], w_gate.shape[-1]
    t_pad = T + 2 * tm // TOP_K
    buf = lambda n: pltpu.VMEM((n, tm * TOK_ROWS, LANES), jnp.uint32)
    row = lambda i, be, na, ix: (jnp.minimum(i, na[0] - 1), 0)
    wsel = lambda i, be, na, ix: (layer, be[i], 0, 0)
    params = pltpu.CompilerParams(dimension_semantics=("arbitrary",), vmem_limit_bytes=V7X_VMEM_ROUTED)
    h = pl.pallas_call(
        functools.partial(_moe_up_body, tm=tm),
        grid_spec=pltpu.PrefetchScalarGridSpec(
            num_scalar_prefetch=3, grid=(NB,),
            in_specs=[pl.BlockSpec(memory_space=pl.ANY),
                      pl.BlockSpec((1, 1, D, FF), wsel), pl.BlockSpec((1, 1, D, FF), wsel)],
            out_specs=pl.BlockSpec((tm, FF), lambda i, be, na, ix: (i, 0)),
            scratch_shapes=[buf(GATHER_SLOTS), pltpu.SemaphoreType.DMA((GATHER_SLOTS,)),
                            pltpu.VMEM((tm, D), BF16), pltpu.VMEM((D, FF), BF16), pltpu.VMEM((D, FF), BF16)]),
        out_shape=jax.ShapeDtypeStruct((NB * tm, FF), BF16),
        compiler_params=params,
        name="moe_up",
    )(block_e, n_active, slot_token * TOK_ROWS, x_packed, w_gate, w_up)
    y = pl.pallas_call(
        functools.partial(_moe_down_body, tm=tm),
        grid_spec=pltpu.PrefetchScalarGridSpec(
            num_scalar_prefetch=3, grid=(NB,),
            in_specs=[pl.BlockSpec((tm, FF), row), pl.BlockSpec((1, 1, FF, D), wsel)],
            out_specs=pl.BlockSpec(memory_space=pl.ANY),
            scratch_shapes=[buf(2), pltpu.SemaphoreType.DMA((2,)), pltpu.VMEM((FF, D), BF16)]),
        out_shape=jax.ShapeDtypeStruct((TOP_K * t_pad * TOK_ROWS, LANES), jnp.uint32),
        compiler_params=params,
        name="moe_down",
    )(block_e, n_active, slot_dest * TOK_ROWS, h, w_down)
    return y.reshape(TOP_K, t_pad * TOK_ROWS, LANES)


SLOT_MAP_CHUNK = 2048


def _slot_map_body(dest_ref, src_ref, *, clear_chunk):
    phase, i = pl.program_id(0), pl.program_id(1)

    @pl.when(phase == 0)
    def _():
        def clear(j, carry):
            src_ref[i * clear_chunk + j] = -1
            return carry

        lax.fori_loop(0, clear_chunk, clear, 0, unroll=8)

    @pl.when(phase == 1)
    def _():
        def place(a, carry):
            src_ref[dest_ref[a]] = i * SLOT_MAP_CHUNK + a
            return carry

        lax.fori_loop(0, SLOT_MAP_CHUNK, place, 0, unroll=8)


def _slot_map(dest, n_slots):
    A = dest.shape[0]
    n_chunks = -(-A // SLOT_MAP_CHUNK)
    clear_chunk = -(-(n_slots + 1) // (8 * n_chunks)) * 8
    dest = jnp.concatenate([dest, jnp.full((n_chunks * SLOT_MAP_CHUNK - A,), n_slots, jnp.int32)])
    src = pl.pallas_call(
        functools.partial(_slot_map_body, clear_chunk=clear_chunk),
        grid=(2, n_chunks),
        in_specs=[pl.BlockSpec((SLOT_MAP_CHUNK,), lambda ph, i: (i * ph,), memory_space=pltpu.SMEM)],
        out_specs=pl.BlockSpec(memory_space=pltpu.SMEM),
        out_shape=jax.ShapeDtypeStruct((n_chunks * clear_chunk,), jnp.int32),
        compiler_params=pltpu.CompilerParams(dimension_semantics=("arbitrary", "arbitrary")),
        name="slot_map",
    )(dest)
    return src[:n_slots]


def _dispatch_plan(idx, tm):
    T = idx.shape[0]
    A = T * TOP_K
    NB = -(-A // tm) + N_EXPERTS
    spare_t = 2 * tm // TOP_K
    t_pad = T + spare_t
    idx = idx.astype(jnp.int32)
    onehot = idx[:, :, None] == jnp.arange(N_EXPERTS, dtype=jnp.int32)[None, None, :]
    sel = onehot.any(axis=1).astype(jnp.int32)
    incl = jnp.cumsum(sel, axis=0)
    counts = incl[-1]
    padded = (counts + tm - 1) // tm * tm
    pad_end = jnp.cumsum(padded)
    pad_start = pad_end - padded
    n_active = (pad_end[-1] // tm).astype(jnp.int32)
    blk = jnp.arange(NB, dtype=jnp.int32)
    block_e = jnp.minimum(jnp.searchsorted(pad_end, jnp.minimum(blk, n_active - 1) * tm, side='right'),
                          N_EXPERTS - 1).astype(jnp.int32)
    slot_of = (incl - sel + pad_start[None, :])[:, None, :]
    dest = jnp.sum(jnp.where(onehot, slot_of, 0), axis=-1).reshape(A)
    slot_src = _slot_map(dest, NB * tm)
    slot = jnp.arange(NB * tm, dtype=jnp.int32)
    valid = slot_src >= 0
    slot_token = jnp.where(valid, slot_src // TOP_K, 0)
    s2 = (slot // tm % 2) * tm + slot % tm
    spare = (s2 // spare_t) * t_pad + T + s2 % spare_t
    slot_dest = jnp.where(valid, (slot_src % TOP_K) * t_pad + slot_src // TOP_K, spare)
    return slot_token, slot_dest, block_e, n_active.reshape(1)


def moe_block(x_parts, xb_parts, xp_parts, p, layer):
    w_t = jnp.swapaxes(p['router_w'], 1, 2)
    bias_col = p['router_bias'][:, :, None]
    routed = [route(x, w_t, bias_col, layer) for x in x_parts]
    idx = jnp.concatenate([r[0] for r in routed], axis=1).T
    wts = jnp.concatenate([r[1] for r in routed], axis=1).T
    plan = _dispatch_plan(idx, MOE_TM)
    y = routed_ffn(jnp.concatenate(xp_parts, axis=0), p['exp_w_gate'], p['exp_w_up'], p['exp_w_down'],
                   layer, plan, MOE_TM)
    shared = [shared_ffn(xb, p['shared_w_gate'], p['shared_w_up'], p['shared_w_down'], layer)
              for xb in xb_parts]
    return y, wts, shared


def _rope_tables(pos, half):
    inv_freq = ROPE_THETA ** (-jnp.arange(half, dtype=F32) / half)
    ang = pos.astype(F32)[:, None] * inv_freq[None, :]
    return jnp.cos(ang), jnp.sin(ang)


def _even_prompt(x_bf, B, L, p, i):
    proj = matmul(x_bf, p['w_in_even'][i].astype(BF16), name="in_proj_even")
    proj3 = proj.reshape(B, L, -1)
    cos, sin = _rope_tables(jnp.arange(L, dtype=jnp.int32), RET_HEAD_DIM // 2)
    conv_out, conv_new = conv_prompt(proj3, p['conv_w'][i], p['conv_b'][i], p['conv_ln_g'][i], p['conv_ln_b'][i])
    ret_out, s_new = retention_prompt(proj3, cos, sin, p['ret_gn_g'][i], p['ret_gn_b'][i])
    mixed = jnp.concatenate([conv_out, ret_out], axis=-1).reshape(B * L, -1)
    return matmul(mixed, p['w_out_even'][i].astype(BF16), name="out_proj_even"), conv_new, s_new


def _even_sample(x_bf, conv_buf, ret_state, p, i):
    B = x_bf.shape[0]
    H, d = RET_HEADS, RET_HEAD_DIM
    proj = matmul(x_bf, p['w_in_even'][i].astype(BF16), name="in_proj_even")
    conv_out, conv_new = conv_sample(proj, conv_buf, p['conv_w'][i], p['conv_b'][i],
                                     p['conv_ln_g'][i], p['conv_ln_b'][i])
    o2 = 2 * CONV_CH
    q_col = proj[:, o2:o2 + RET_WIDTH].reshape(B, H, d, 1)
    k_col = proj[:, o2 + RET_WIDTH:o2 + 2 * RET_WIDTH].reshape(B, H, d, 1)
    v_row = proj[:, o2 + 2 * RET_WIDTH:o2 + 3 * RET_WIDTH].reshape(B, 1, RET_WIDTH)
    g_row = proj[:, o2 + 3 * RET_WIDTH:].reshape(B, 1, RET_WIDTH)
    cos, sin = _rope_tables(jnp.full((1,), PAST_LEN, jnp.int32), d // 2)
    ret_out, s_new = retention_sample(q_col, k_col, v_row, g_row, ret_state, cos.reshape(d // 2, 1),
                                      sin.reshape(d // 2, 1), p['ret_gn_g'][i], p['ret_gn_b'][i])
    mixed = jnp.concatenate([conv_out, ret_out.reshape(B, RET_WIDTH)], axis=-1)
    return matmul(mixed, p['w_out_even'][i].astype(BF16), name="out_proj_even"), conv_new, s_new


def _att_rope_tables(pos, rows):
    cos, sin = _rope_tables(pos, ATT_HEAD_DIM // 2)
    cos = jnp.tile(cos, (rows // cos.shape[0], LANES // cos.shape[1]))
    sin = jnp.tile(sin, (rows // sin.shape[0], LANES // sin.shape[1]))
    return cos, sin


def _odd_prompt(x_bf, B, L, p, i):
    cos, sin = _att_rope_tables(jnp.arange(L, dtype=jnp.int32), L)
    qkv = matmul(x_bf, p['w_qkv_odd'][i].astype(BF16), bias=p['b_qkv_odd'][i],
                 rope=(cos, sin, Q_WIDTH + KV_WIDTH), name="qkv_proj_odd")
    qkv3 = qkv.reshape(B, L, -1)
    o = swa_prompt(qkv3, p['sinks'][i])
    out = matmul(o.reshape(B * L, Q_WIDTH), p['w_out_odd'][i].astype(BF16), bias=p['b_out_odd'][i],
                 name="out_proj_odd")
    k_new = qkv3[:, L - WINDOW:, Q_WIDTH:Q_WIDTH + KV_WIDTH].reshape(B, WINDOW, N_KV_HEADS, ATT_HEAD_DIM)
    v_new = qkv3[:, L - WINDOW:, Q_WIDTH + KV_WIDTH:].reshape(B, WINDOW, N_KV_HEADS, ATT_HEAD_DIM)
    return out, k_new, v_new


def _odd_sample(x_bf, k_buf, v_buf, p, i):
    B = x_bf.shape[0]
    cos, sin = _att_rope_tables(jnp.full((1,), PAST_LEN, jnp.int32), B)
    qkv = matmul(x_bf, p['w_qkv_odd'][i].astype(BF16), bias=p['b_qkv_odd'][i],
                 rope=(cos, sin, Q_WIDTH + KV_WIDTH), name="qkv_proj_odd")
    q3 = qkv[:, :Q_WIDTH].reshape(B, N_HEADS, ATT_HEAD_DIM)
    kn3 = qkv[:, Q_WIDTH:Q_WIDTH + KV_WIDTH].reshape(B, N_KV_HEADS, ATT_HEAD_DIM)
    vn3 = qkv[:, Q_WIDTH + KV_WIDTH:].reshape(B, N_KV_HEADS, ATT_HEAD_DIM)
    o, k_new, v_new = swa_sample(q3, kn3, vn3, k_buf.reshape(B, WINDOW, KV_WIDTH),
                                 v_buf.reshape(B, WINDOW, KV_WIDTH), p['sinks'][i])
    out = matmul(o.reshape(B, Q_WIDTH), p['w_out_odd'][i].astype(BF16), bias=p['b_out_odd'][i],
                 name="out_proj_odd")
    shape = (B, WINDOW, N_KV_HEADS, ATT_HEAD_DIM)
    return out, k_new.reshape(shape), v_new.reshape(shape)


def kernel(x_prompt, x_sample, state_conv, state_ret, state_win_k, state_win_v,
           w_in_even, conv_w, conv_b, conv_ln_g, conv_ln_b, ret_gn_g, ret_gn_b, w_out_even,
           w_qkv_odd, b_qkv_odd, sinks, w_out_odd, b_out_odd, ln_g, ln_b,
           router_w, router_bias, exp_w_gate, exp_w_up, exp_w_down,
           shared_w_gate, shared_w_up, shared_w_down):
    p = dict(w_in_even=w_in_even, conv_w=conv_w, conv_b=conv_b, conv_ln_g=conv_ln_g, conv_ln_b=conv_ln_b,
             ret_gn_g=ret_gn_g, ret_gn_b=ret_gn_b, w_out_even=w_out_even,
             w_qkv_odd=w_qkv_odd, b_qkv_odd=b_qkv_odd, sinks=sinks, w_out_odd=w_out_odd, b_out_odd=b_out_odd,
             ln_g=ln_g, ln_b=ln_b, router_w=router_w, router_bias=router_bias,
             exp_w_gate=exp_w_gate, exp_w_up=exp_w_up, exp_w_down=exp_w_down,
             shared_w_gate=shared_w_gate, shared_w_up=shared_w_up, shared_w_down=shared_w_down)
    B, L, D = x_prompt.shape
    Bs = x_sample.shape[0]
    Tp = B * L
    xp = x_prompt.reshape(Tp, D)
    xs = x_sample.reshape(Bs, D)
    xp_bf, xs_bf = xp.astype(BF16), xs.astype(BF16)
    conv_p, conv_s, ret_p, ret_s, wk_p, wk_s, wv_p, wv_s = [], [], [], [], [], [], [], []
    for layer in range(DEPTH):
        i = layer // 2
        if layer % 2 == 0:
            hp, c_new, r_new = _even_prompt(xp_bf, B, L, p, i)
            conv_p.append(c_new)
            ret_p.append(r_new)
            hs, c_new, r_new = _even_sample(xs_bf, state_conv[i], state_ret[i], p, i)
            conv_s.append(c_new)
            ret_s.append(r_new)
        else:
            hp, k_new, v_new = _odd_prompt(xp_bf, B, L, p, i)
            wk_p.append(k_new)
            wv_p.append(v_new)
            hs, k_new, v_new = _odd_sample(xs_bf, state_win_k[i], state_win_v[i], p, i)
            wk_s.append(k_new)
            wv_s.append(v_new)
        xp, xp_bf, xp_pk = deepnorm(xp, [(hp, 0)], ln_g[layer, 0], ln_b[layer, 0], name="ln_mixer_prompt")
        xs, xs_bf, xs_pk = deepnorm(xs, [(hs, 0)], ln_g[layer, 0], ln_b[layer, 0], name="ln_mixer_sample")
        y, wts, shared = moe_block([xp, xs], [xp_bf, xs_bf], [xp_pk, xs_pk], p, layer)
        xp, xp_bf, _ = deepnorm(xp, [(shared[0], 0)], ln_g[layer, 1], ln_b[layer, 1], combine=(y, wts, 0),
                                tm=64, name="ln_moe_prompt")
        xs, xs_bf, _ = deepnorm(xs, [(shared[1], 0)], ln_g[layer, 1], ln_b[layer, 1], combine=(y, wts, Tp),
                                tm=32, name="ln_moe_sample")
    return (xp.reshape(B, L, D), xs.reshape(Bs, 1, D),
            jnp.stack(conv_p), jnp.stack(conv_s), jnp.stack(ret_p), jnp.stack(ret_s),
            jnp.stack(wk_p), jnp.stack(wk_s), jnp.stack(wv_p), jnp.stack(wv_s))
```

```python
import functools

import jax
import jax.numpy as jnp
import numpy as np
from jax import lax
from jax.experimental import pallas as pl
from jax.experimental.pallas import tpu as pltpu

F32 = jnp.float32
BF16 = jnp.bfloat16

D_MODEL = 4096
DEPTH = 2
PAST_LEN = 16384
CONV_CH = D_MODEL // 2
CONV_WIDTH = 31
CONV_HALO = 32
RET_HEADS = 8
RET_HEAD_DIM = 256
RET_WIDTH = RET_HEADS * RET_HEAD_DIM
RET_CHUNK = 128
ATT_HEAD_DIM = 64
N_HEADS = 64
N_KV_HEADS = 8
GROUP = N_HEADS // N_KV_HEADS
WINDOW = 128
Q_WIDTH = N_HEADS * ATT_HEAD_DIM
KV_WIDTH = N_KV_HEADS * ATT_HEAD_DIM
ROPE_THETA = 10000.0
N_EXPERTS = 128
TOP_K = 8
N_GROUPS = 8
TOPK_GROUPS = 4
EXPERTS_PER_GROUP = N_EXPERTS // N_GROUPS
EXPERT_FF = 512
ROUTED_SCALE = 2.5
DN_ALPHA = (2.0 * DEPTH) ** 0.25
LN_EPS = 1e-5

LANES = 128
V7X_VMEM_LIMIT = 52 * 1024 * 1024
MOE_TM = 256


def _params(*sem):
    return pltpu.CompilerParams(dimension_semantics=sem, vmem_limit_bytes=V7X_VMEM_LIMIT)


def _silu(x):
    return x * jax.nn.sigmoid(x)


def _rot_half_64(x):
    n = x.shape[-1]
    lane = lax.broadcasted_iota(jnp.int32, x.shape, x.ndim - 1)
    fwd = pltpu.roll(x, n - ATT_HEAD_DIM // 2, x.ndim - 1)
    bwd = pltpu.roll(x, ATT_HEAD_DIM // 2, x.ndim - 1)
    return jnp.where(lane % ATT_HEAD_DIM < ATT_HEAD_DIM // 2, -fwd, bwd)


def _mm_body(*refs, has_bias, rope_tiles):
    x_ref, w_ref = refs[0], refs[1]
    pos = 2
    b_ref = None
    if has_bias:
        b_ref = refs[pos]
        pos += 1
    if rope_tiles:
        cos_ref, sin_ref = refs[pos], refs[pos + 1]
        pos += 2
    o_ref = refs[pos]
    acc = jnp.dot(x_ref[...].astype(BF16), w_ref[...].astype(BF16), preferred_element_type=F32)
    if has_bias:
        acc = acc + b_ref[...]
    if rope_tiles:
        reps = acc.shape[1] // LANES
        cos = jnp.concatenate([cos_ref[...]] * reps, axis=1)
        sin = jnp.concatenate([sin_ref[...]] * reps, axis=1)
        @pl.when(pl.program_id(1) < rope_tiles)
        def _():
            o_ref[...] = (acc * cos + _rot_half_64(acc) * sin).astype(o_ref.dtype)

        @pl.when(pl.program_id(1) >= rope_tiles)
        def _():
            o_ref[...] = acc.astype(o_ref.dtype)
    else:
        o_ref[...] = acc.astype(o_ref.dtype)


def matmul(x, w, bias=None, rope=None, out_dtype=F32, tm=1024, tn=512, name="matmul"):
    M, K = x.shape
    N = w.shape[1]
    tm = min(tm, M)
    tn = min(tn, N)
    assert M % tm == 0 and N % tn == 0
    in_specs = [pl.BlockSpec((tm, K), lambda i, j: (i, 0)),
                pl.BlockSpec((K, tn), lambda i, j: (0, j))]
    args = [x, w]
    if bias is not None:
        in_specs.append(pl.BlockSpec((1, tn), lambda i, j: (0, j)))
        args.append(bias.reshape(1, N))
    rope_tiles = 0
    if rope is not None:
        cos, sin, n_cols = rope
        assert n_cols % tn == 0 and cos.shape[0] % tm == 0
        rope_tiles = n_cols // tn
        nrb = cos.shape[0] // tm
        in_specs += [pl.BlockSpec((tm, LANES), lambda i, j: (i % nrb, 0))] * 2
        args += [cos, sin]
    return pl.pallas_call(
        functools.partial(_mm_body, has_bias=bias is not None, rope_tiles=rope_tiles),
        grid=(M // tm, N // tn),
        in_specs=in_specs,
        out_specs=pl.BlockSpec((tm, tn), lambda i, j: (i, j)),
        out_shape=jax.ShapeDtypeStruct((M, N), out_dtype),
        compiler_params=_params("parallel", "arbitrary"),
        name=name,
    )(*args)


def _split_bf16(x):
    hi = x.astype(BF16)
    lo = (x - hi.astype(F32)).astype(BF16)
    return hi, lo


def _first_argmax(x, row, n):
    m = jnp.max(x, axis=0, keepdims=True)
    i = jnp.min(jnp.where(x == m, row, n), axis=0, keepdims=True)
    return m, i


def _router_body(x_ref, wt_ref, bias_ref, idx_ref, wts_ref):
    nt = (((1,), (1,)), ((), ()))
    xh, xl = _split_bf16(x_ref[...])
    wh, wl = _split_bf16(wt_ref[0])
    logits = lax.dot_general(wh, xh, nt, preferred_element_type=F32)
    logits = logits + lax.dot_general(wh, xl, nt, preferred_element_type=F32)
    logits = logits + lax.dot_general(wl, xh, nt, preferred_element_type=F32)
    scores = jax.nn.sigmoid(logits)
    biased = scores + bias_ref[0]
    tm = scores.shape[1]
    neg = -jnp.inf
    row_g = lax.broadcasted_iota(jnp.int32, (EXPERTS_PER_GROUP, tm), 0)
    grp = []
    for g in range(N_GROUPS):
        xg = biased[g * EXPERTS_PER_GROUP:(g + 1) * EXPERTS_PER_GROUP]
        m1, i1 = _first_argmax(xg, row_g, EXPERTS_PER_GROUP)
        m2 = jnp.max(jnp.where(row_g == i1, neg, xg), axis=0, keepdims=True)
        grp.append(m1 + m2)
    gs = jnp.concatenate(grp, axis=0)
    row_n = lax.broadcasted_iota(jnp.int32, (N_GROUPS, tm), 0)
    chosen = jnp.zeros((N_GROUPS, tm), F32)
    for _ in range(TOPK_GROUPS):
        _, i = _first_argmax(gs, row_n, N_GROUPS)
        hit = row_n == i
        chosen = jnp.where(hit, 1.0, chosen)
        gs = jnp.where(hit, neg, gs)
    emask = jnp.concatenate([jnp.broadcast_to(chosen[g:g + 1], (EXPERTS_PER_GROUP, tm))
                             for g in range(N_GROUPS)], axis=0)
    cand = jnp.where(emask > 0.0, biased, neg)
    row_e = lax.broadcasted_iota(jnp.int32, (N_EXPERTS, tm), 0)
    ids, ws = [], []
    for _ in range(TOP_K):
        _, i = _first_argmax(cand, row_e, N_EXPERTS)
        hit = row_e == i
        ids.append(i)
        ws.append(jnp.sum(jnp.where(hit, scores, 0.0), axis=0, keepdims=True))
        cand = jnp.where(hit, neg, cand)
    w = jnp.concatenate(ws, axis=0)
    idx_ref[...] = jnp.concatenate(ids, axis=0)
    wts_ref[...] = w / jnp.sum(w, axis=0, keepdims=True) * ROUTED_SCALE


def route(x, w_t, bias_col, layer, tm=512):
    M, K = x.shape
    E = w_t.shape[1]
    tm = min(tm, M)
    return pl.pallas_call(
        _router_body,
        grid=(M // tm,),
        in_specs=[pl.BlockSpec((tm, K), lambda i: (i, 0)),
                  pl.BlockSpec((1, E, K), lambda i: (layer, 0, 0)),
                  pl.BlockSpec((1, E, 1), lambda i: (layer, 0, 0))],
        out_specs=[pl.BlockSpec((TOP_K, tm), lambda i: (0, i))] * 2,
        out_shape=[jax.ShapeDtypeStruct((TOP_K, M), jnp.int32), jax.ShapeDtypeStruct((TOP_K, M), F32)],
        compiler_params=_params("parallel"),
        name="router",
    )(x, w_t, bias_col)


def _layer_norm_rows(z, g, b):
    mu = jnp.mean(z, axis=-1, keepdims=True)
    zc = z - mu
    var = jnp.mean(zc * zc, axis=-1, keepdims=True)
    return zc * lax.rsqrt(var + LN_EPS) * g + b


def _unpack_pair(w):
    lo = lax.bitcast_convert_type(w << 16, F32)
    hi = lax.bitcast_convert_type(w & jnp.uint32(0xFFFF0000), F32)
    return lo, hi


def _pack_pair(lo, hi):
    lo_bits = lax.bitcast_convert_type(lo.astype(BF16).astype(F32), jnp.uint32) >> 16
    hi_bits = lax.bitcast_convert_type(hi.astype(BF16).astype(F32), jnp.uint32) & jnp.uint32(0xFFFF0000)
    return hi_bits | lo_bits


TOK_ROWS = D_MODEL // 2 // LANES


def _load_token_rows(ref, s, tm):
    return ref[pl.ds(s, tm, stride=TOK_ROWS), :]


def _store_token_rows(ref, y, tm):
    half = y.shape[1] // 2
    for s in range(TOK_ROWS):
        words = _pack_pair(y[:, s * LANES:(s + 1) * LANES], y[:, half + s * LANES:half + (s + 1) * LANES])
        ref[pl.ds(s, tm, stride=TOK_ROWS), :] = words


def _ln_body(*refs, n_h, combine):
    x_ref = refs[0]
    h_refs = refs[1:1 + n_h]
    pos = 1 + n_h
    if combine:
        y_ref, w_ref = refs[pos], refs[pos + 1]
        pos += 2
    g_ref, b_ref, o_ref, ob_ref, op_ref = refs[pos:]
    tm, d = x_ref.shape
    half = d // 2
    z = DN_ALPHA * x_ref[...]
    for h_ref in h_refs:
        z = z + h_ref[...]
    if combine:
        wk = [jnp.broadcast_to(w_ref[:, k:k + 1], (tm, LANES)) for k in range(TOP_K)]
        cols_lo, cols_hi = [], []
        for s in range(TOK_ROWS):
            acc_lo = z[:, s * LANES:(s + 1) * LANES]
            acc_hi = z[:, half + s * LANES:half + (s + 1) * LANES]
            for k in range(TOP_K):
                lo, hi = _unpack_pair(_load_token_rows(y_ref.at[k], s, tm))
                acc_lo = acc_lo + wk[k] * lo
                acc_hi = acc_hi + wk[k] * hi
            cols_lo.append(acc_lo)
            cols_hi.append(acc_hi)
        z = jnp.concatenate(cols_lo + cols_hi, axis=1)
    y = _layer_norm_rows(z, g_ref[...], b_ref[...])
    o_ref[...] = y
    ob_ref[...] = y.astype(BF16)
    _store_token_rows(op_ref, y, tm)


def deepnorm(x, hs, g, b, combine=None, tm=256, name="deepnorm"):
    M, D = x.shape
    tm = min(tm, M)
    assert M % tm == 0
    row_spec = lambda width, off: pl.BlockSpec((tm, width), functools.partial(lambda i, o: (i + o, 0), o=off // tm))
    in_specs = [row_spec(D, 0)]
    args = [x]
    for h, off in hs:
        assert off % tm == 0
        in_specs.append(row_spec(D, off))
        args.append(h)
    if combine is not None:
        y, wts, off = combine
        assert off % tm == 0
        y_spec = pl.BlockSpec((TOP_K, tm * TOK_ROWS, LANES),
                              functools.partial(lambda i, o: (0, i + o, 0), o=off // tm))
        in_specs += [y_spec, row_spec(TOP_K, off)]
        args += [y, wts]
    in_specs += [pl.BlockSpec((1, D), lambda i: (0, 0))] * 2
    args += [g.reshape(1, D), b.reshape(1, D)]
    return pl.pallas_call(
        functools.partial(_ln_body, n_h=len(hs), combine=combine is not None),
        grid=(M // tm,),
        in_specs=in_specs,
        out_specs=[row_spec(D, 0), row_spec(D, 0), pl.BlockSpec((tm * TOK_ROWS, LANES), lambda i: (i, 0))],
        out_shape=[jax.ShapeDtypeStruct((M, D), F32), jax.ShapeDtypeStruct((M, D), BF16),
                   jax.ShapeDtypeStruct((M * TOK_ROWS, LANES), jnp.uint32)],
        compiler_params=_params("parallel"),
        name=name,
    )(*args)


def _conv_prompt_body(a_ref, g_ref, pa_ref, pg_ref, w_ref, bdw_ref, lng_ref, lnb_ref,
                      o_ref, st_ref, full_ref, y_ref, *, tl):
    i = pl.program_id(1)
    u = a_ref[0] * jax.nn.sigmoid(g_ref[0])
    up = pa_ref[0] * jax.nn.sigmoid(pg_ref[0])
    up = jnp.where(i > 0, up, 0.0)
    full_ref[0:CONV_HALO, :] = up
    full_ref[CONV_HALO:CONV_HALO + tl, :] = u
    base = CONV_HALO - (CONV_WIDTH - 1)
    n_col = u.shape[1] // LANES

    def col_step(c, carry):
        off = pl.multiple_of(c * LANES, LANES)
        acc = jnp.zeros((tl, LANES), F32)
        for j in range(CONV_WIDTH):
            acc = acc + w_ref[j:j + 1, pl.ds(off, LANES)] * full_ref[base + j:base + j + tl, pl.ds(off, LANES)]
        y_ref[:, pl.ds(off, LANES)] = acc
        return carry

    lax.fori_loop(0, n_col, col_step, 0)
    y = _layer_norm_rows(y_ref[...] + bdw_ref[...], lng_ref[...], lnb_ref[...])
    o_ref[0] = _silu(y).astype(o_ref.dtype)

    @pl.when(i == pl.num_programs(1) - 1)
    def _():
        st_ref[0] = full_ref[CONV_HALO + tl - (CONV_WIDTH - 1):CONV_HALO + tl, :]


def conv_prompt(proj3, w_dw, b_dw, ln_g, ln_b, tl=128):
    B, L, _ = proj3.shape
    C = CONV_CH
    nh = tl // CONV_HALO
    vec = lambda v: v.reshape(1, C)
    return pl.pallas_call(
        functools.partial(_conv_prompt_body, tl=tl),
        grid=(B, L // tl),
        in_specs=[
            pl.BlockSpec((1, tl, C), lambda b, i: (b, i, 0)),
            pl.BlockSpec((1, tl, C), lambda b, i: (b, i, 1)),
            pl.BlockSpec((1, CONV_HALO, C), lambda b, i: (b, jnp.maximum(i * nh - 1, 0), 0)),
            pl.BlockSpec((1, CONV_HALO, C), lambda b, i: (b, jnp.maximum(i * nh - 1, 0), 1)),
            pl.BlockSpec((CONV_WIDTH, C), lambda b, i: (0, 0)),
            pl.BlockSpec((1, C), lambda b, i: (0, 0)),
            pl.BlockSpec((1, C), lambda b, i: (0, 0)),
            pl.BlockSpec((1, C), lambda b, i: (0, 0)),
        ],
        out_specs=[
            pl.BlockSpec((1, tl, C), lambda b, i: (b, i, 0)),
            pl.BlockSpec((1, CONV_WIDTH - 1, C), lambda b, i: (b, 0, 0)),
        ],
        out_shape=[jax.ShapeDtypeStruct((B, L, C), BF16),
                   jax.ShapeDtypeStruct((B, CONV_WIDTH - 1, C), F32)],
        scratch_shapes=[pltpu.VMEM((CONV_HALO + tl, C), F32), pltpu.VMEM((tl, C), F32)],
        compiler_params=_params("parallel", "arbitrary"),
        name="conv_prompt",
    )(proj3, proj3, proj3, proj3, w_dw, vec(b_dw), vec(ln_g), vec(ln_b))


def _conv_sample_body(a_ref, g_ref, buf_ref, w_ref, bdw_ref, lng_ref, lnb_ref, o_ref, st_ref):
    u = a_ref[...] * jax.nn.sigmoid(g_ref[...])
    nb = CONV_WIDTH - 1
    acc = w_ref[nb:nb + 1, :] * u
    for j in range(nb):
        acc = acc + w_ref[j:j + 1, :] * buf_ref[:, j, :]
    y = _layer_norm_rows(acc + bdw_ref[...], lng_ref[...], lnb_ref[...])
    o_ref[...] = _silu(y).astype(o_ref.dtype)
    st_ref[:, 0:nb - 1, :] = buf_ref[:, 1:nb, :]
    st_ref[:, nb - 1, :] = u


def conv_sample(proj, buf, w_dw, b_dw, ln_g, ln_b, tb=8):
    B = proj.shape[0]
    C = CONV_CH
    nb = CONV_WIDTH - 1
    vec = lambda v: v.reshape(1, C)
    return pl.pallas_call(
        _conv_sample_body,
        grid=(B // tb,),
        in_specs=[
            pl.BlockSpec((tb, C), lambda b: (b, 0)),
            pl.BlockSpec((tb, C), lambda b: (b, 1)),
            pl.BlockSpec((tb, nb, C), lambda b: (b, 0, 0)),
            pl.BlockSpec((CONV_WIDTH, C), lambda b: (0, 0)),
            pl.BlockSpec((1, C), lambda b: (0, 0)),
            pl.BlockSpec((1, C), lambda b: (0, 0)),
            pl.BlockSpec((1, C), lambda b: (0, 0)),
        ],
        out_specs=[pl.BlockSpec((tb, C), lambda b: (b, 0)),
                   pl.BlockSpec((tb, nb, C), lambda b: (b, 0, 0))],
        out_shape=[jax.ShapeDtypeStruct((B, C), BF16), jax.ShapeDtypeStruct((B, nb, C), F32)],
        compiler_params=_params("parallel"),
        name="conv_sample",
    )(proj, proj, buf, w_dw, vec(b_dw), vec(ln_g), vec(ln_b))


def _ret_log_gamma():
    return np.log1p(-np.exp2(-5.0 - np.arange(RET_HEADS, dtype=np.float64)))


def _group_norm_gate(o, gate, gn_g, gn_b):
    mu = jnp.mean(o, axis=-1, keepdims=True)
    oc = o - mu
    var = jnp.mean(oc * oc, axis=-1, keepdims=True)
    return (oc * lax.rsqrt(var + LN_EPS) * gn_g + gn_b) * _silu(gate)


def _rope_256(x, cos, sin):
    half = RET_HEAD_DIM // 2
    x1, x2 = x[..., :half], x[..., half:]
    return jnp.concatenate([x1 * cos - x2 * sin, x2 * cos + x1 * sin], axis=-1)


def _ret_prompt_body(q_ref, k_ref, v_ref, g_ref, cos_ref, sin_ref, intra_ref, qd_ref, kd_ref, cd_ref,
                     gng_ref, gnb_ref, o_ref, s_ref):
    c = pl.program_id(2)

    @pl.when(c == 0)
    def _():
        s_ref[...] = jnp.zeros_like(s_ref)

    cos, sin = cos_ref[...], sin_ref[...]
    q = _rope_256(q_ref[0], cos, sin)
    k = _rope_256(k_ref[0], cos, sin) * (RET_HEAD_DIM ** -0.5)
    vb = v_ref[0].astype(BF16)
    s_prev = s_ref[0, 0]
    scores = lax.dot_general(q.astype(BF16), k.astype(BF16), (((1,), (1,)), ((), ())),
                             preferred_element_type=F32) * intra_ref[0]
    o = jnp.dot(scores.astype(BF16), vb, preferred_element_type=F32)
    o = o + jnp.dot((q * qd_ref[0]).astype(BF16), s_prev.astype(BF16), preferred_element_type=F32)
    kdt = (k * kd_ref[0]).T.astype(BF16)
    kv = jnp.dot(kdt, vb, preferred_element_type=F32)
    s_ref[0, 0] = cd_ref[0] * s_prev + kv
    o_ref[0] = _group_norm_gate(o, g_ref[0], gng_ref[...], gnb_ref[...]).astype(o_ref.dtype)


def retention_prompt(proj3, cos, sin, gn_g, gn_b):
    B, L, _ = proj3.shape
    H, d, c = RET_HEADS, RET_HEAD_DIM, RET_CHUNK
    lg = _ret_log_gamma()
    idx = np.arange(c, dtype=np.float64)
    rel = idx[:, None] - idx[None, :]
    intra = np.where(rel[None] >= 0, np.exp(lg[:, None, None] * np.maximum(rel, 0.0)[None]), 0.0)
    ones = np.ones((1, 1, d))
    qd = np.exp(lg[:, None] * (idx + 1.0)[None, :])[:, :, None] * ones
    kd = np.exp(lg[:, None] * (c - 1.0 - idx)[None, :])[:, :, None] * ones
    cd = np.exp(lg * c)[:, None, None] * ones
    col0 = 2 * CONV_CH // d
    head_spec = lambda off: pl.BlockSpec((1, c, d), lambda b, h, n: (b, n, col0 + off * H + h))
    tab_spec = pl.BlockSpec((1, c, d), lambda b, h, n: (h, 0, 0))
    return pl.pallas_call(
        _ret_prompt_body,
        grid=(B, H, L // c),
        in_specs=[
            head_spec(0), head_spec(1), head_spec(2), head_spec(3),
            pl.BlockSpec((c, d // 2), lambda b, h, n: (n, 0)),
            pl.BlockSpec((c, d // 2), lambda b, h, n: (n, 0)),
            pl.BlockSpec((1, c, c), lambda b, h, n: (h, 0, 0)),
            tab_spec, tab_spec,
            pl.BlockSpec((1, 1, d), lambda b, h, n: (h, 0, 0)),
            pl.BlockSpec((1, d), lambda b, h, n: (0, h)),
            pl.BlockSpec((1, d), lambda b, h, n: (0, h)),
        ],
        out_specs=[
            pl.BlockSpec((1, c, d), lambda b, h, n: (b, n, h)),
            pl.BlockSpec((1, 1, d, d), lambda b, h, n: (b, h, 0, 0)),
        ],
        out_shape=[jax.ShapeDtypeStruct((B, L, H * d), BF16),
                   jax.ShapeDtypeStruct((B, H, d, d), F32)],
        compiler_params=_params("parallel", "parallel", "arbitrary"),
        name="retention_prompt",
    )(proj3, proj3, proj3, proj3, cos, sin,
      jnp.asarray(intra, F32), jnp.asarray(qd, F32), jnp.asarray(kd, F32), jnp.asarray(cd, F32),
      gn_g.reshape(1, H * d), gn_b.reshape(1, H * d))


def _ret_sample_body(qc_ref, kc_ref, v_ref, g_ref, s_ref, cosc_ref, sinc_ref, gam_ref, gng_ref, gnb_ref,
                     o_ref, so_ref):
    half = RET_HEAD_DIM // 2
    cos, sin = cosc_ref[...], sinc_ref[...]

    def rope_col(x):
        x1, x2 = x[:half], x[half:]
        return jnp.concatenate([x1 * cos - x2 * sin, x2 * cos + x1 * sin], axis=0)

    for h in range(RET_HEADS):
        lo, hi = h * RET_HEAD_DIM, (h + 1) * RET_HEAD_DIM
        q = rope_col(qc_ref[0, h])
        k = rope_col(kc_ref[0, h]) * (RET_HEAD_DIM ** -0.5)
        v = v_ref[0, :, lo:hi]
        s_new = gam_ref[h] * s_ref[0, h] + k * v
        so_ref[0, h] = s_new
        o = jnp.sum(q * s_new, axis=0, keepdims=True)
        o_ref[0, :, lo:hi] = _group_norm_gate(o, g_ref[0, :, lo:hi], gng_ref[:, lo:hi],
                                              gnb_ref[:, lo:hi]).astype(o_ref.dtype)


def retention_sample(q_col, k_col, v_row, g_row, state, cos_col, sin_col, gn_g, gn_b):
    B = state.shape[0]
    H, d = RET_HEADS, RET_HEAD_DIM
    gam = jnp.asarray(np.exp(_ret_log_gamma()), F32)
    col_spec = pl.BlockSpec((1, H, d, 1), lambda b: (b, 0, 0, 0))
    row_spec = pl.BlockSpec((1, 1, H * d), lambda b: (b, 0, 0))
    return pl.pallas_call(
        _ret_sample_body,
        grid=(B,),
        in_specs=[
            col_spec, col_spec, row_spec, row_spec,
            pl.BlockSpec((1, H, d, d), lambda b: (b, 0, 0, 0)),
            pl.BlockSpec((d // 2, 1), lambda b: (0, 0)),
            pl.BlockSpec((d // 2, 1), lambda b: (0, 0)),
            pl.BlockSpec(memory_space=pltpu.SMEM),
            pl.BlockSpec((1, H * d), lambda b: (0, 0)),
            pl.BlockSpec((1, H * d), lambda b: (0, 0)),
        ],
        out_specs=[row_spec, pl.BlockSpec((1, H, d, d), lambda b: (b, 0, 0, 0))],
        out_shape=[jax.ShapeDtypeStruct((B, 1, H * d), BF16), jax.ShapeDtypeStruct((B, H, d, d), F32)],
        compiler_params=_params("parallel"),
        name="retention_sample",
    )(q_col, k_col, v_row, g_row, state, cos_col, sin_col, gam, gn_g.reshape(1, H * d), gn_b.reshape(1, H * d))


def _pair_operand(x2, h, lane):
    col = x2[:, (h // 2) * LANES:(h // 2 + 1) * LANES]
    swapped = pltpu.roll(col, ATT_HEAD_DIM, 1)
    first, second = (col, swapped) if h % 2 == 0 else (swapped, col)
    return jnp.concatenate([jnp.where(lane < ATT_HEAD_DIM, first, 0.0),
                            jnp.where(lane >= ATT_HEAD_DIM, second, 0.0)], axis=0).astype(BF16)


def _swa_prompt_body(sink_ref, q_ref, kc_ref, kp_ref, vc_ref, vp_ref, o_ref):
    j = pl.program_id(1)
    W = WINDOW
    k2 = jnp.concatenate([kp_ref[0], kc_ref[0]], axis=0)
    v2 = jnp.concatenate([vp_ref[0], vc_ref[0]], axis=0)
    row = lax.broadcasted_iota(jnp.int32, (W, 2 * W), 0)
    col = lax.broadcasted_iota(jnp.int32, (W, 2 * W), 1)
    first_key = jnp.where(j > 0, 0, W)
    mask = (col >= jnp.maximum(row, first_key)) & (col <= W + row)
    lane_kv = lax.broadcasted_iota(jnp.int32, (2 * W, LANES), 1)
    lane_o = lax.broadcasted_iota(jnp.int32, (W, LANES), 1)
    scale = ATT_HEAD_DIM ** -0.5
    for h in range(N_KV_HEADS):
        kk = _pair_operand(k2, h, lane_kv)
        vv = _pair_operand(v2, h, lane_kv)
        for p in range(GROUP // 2):
            c0 = h * GROUP * ATT_HEAD_DIM + p * LANES
            qp = q_ref[0, :, c0:c0 + LANES].astype(BF16)
            s = lax.dot_general(qp, kk, (((1,), (1,)), ((), ())), preferred_element_type=F32) * scale
            probs, inv = [], []
            for t in range(2):
                sink = sink_ref[h * GROUP + 2 * p + t]
                st = jnp.where(mask, s[:, t * 2 * W:(t + 1) * 2 * W], -jnp.inf)
                m = jnp.maximum(jnp.max(st, axis=-1, keepdims=True), sink)
                e = jnp.exp(st - m)
                denom = jnp.sum(e, axis=-1, keepdims=True) + jnp.exp(sink - m)
                probs.append(e.astype(BF16))
                inv.append(1.0 / denom)
            o2 = jnp.dot(jnp.concatenate(probs, axis=1), vv, preferred_element_type=F32)
            o2 = o2 * jnp.where(lane_o < ATT_HEAD_DIM, inv[0], inv[1])
            o_ref[0, :, c0:c0 + LANES] = o2.astype(o_ref.dtype)


def swa_prompt(qkv3, sinks):
    B, L, _ = qkv3.shape
    W = WINDOW
    kcol = Q_WIDTH // KV_WIDTH
    return pl.pallas_call(
        _swa_prompt_body,
        grid_spec=pltpu.PrefetchScalarGridSpec(
            num_scalar_prefetch=0,
            grid=(B, L // W),
            in_specs=[
                pl.BlockSpec(memory_space=pltpu.SMEM),
                pl.BlockSpec((1, W, Q_WIDTH), lambda b, j: (b, j, 0)),
                pl.BlockSpec((1, W, KV_WIDTH), lambda b, j: (b, j, kcol)),
                pl.BlockSpec((1, W, KV_WIDTH), lambda b, j: (b, jnp.maximum(j - 1, 0), kcol)),
                pl.BlockSpec((1, W, KV_WIDTH), lambda b, j: (b, j, kcol + 1)),
                pl.BlockSpec((1, W, KV_WIDTH), lambda b, j: (b, jnp.maximum(j - 1, 0), kcol + 1)),
            ],
            out_specs=pl.BlockSpec((1, W, Q_WIDTH), lambda b, j: (b, j, 0)),
        ),
        out_shape=jax.ShapeDtypeStruct((B, L, Q_WIDTH), BF16),
        compiler_params=_params("parallel", "parallel"),
        name="swa_prompt",
    )(sinks, qkv3, qkv3, qkv3, qkv3, qkv3)


def _swa_sample_body(q_ref, kn_ref, vn_ref, knf_ref, vnf_ref, kb_ref, vb_ref, sink_ref, o_ref, ko_ref, vo_ref):
    W = WINDOW
    scale = ATT_HEAD_DIM ** -0.5
    kb = kb_ref[0]
    vb = vb_ref[0]
    for h in range(N_KV_HEADS):
        lo, hi = h * ATT_HEAD_DIM, (h + 1) * ATT_HEAD_DIM
        qh = q_ref[0, h * GROUP:(h + 1) * GROUP, :]
        kn = kn_ref[0, h:h + 1, :]
        vn = vn_ref[0, h:h + 1, :]
        sink = sink_ref[h * GROUP:(h + 1) * GROUP, :]
        s_buf = lax.dot_general(qh.astype(BF16), kb[:, lo:hi].astype(BF16), (((1,), (1,)), ((), ())),
                                preferred_element_type=F32) * scale
        s_new = jnp.sum(qh * kn, axis=-1, keepdims=True) * scale
        m = jnp.maximum(jnp.maximum(jnp.max(s_buf, axis=-1, keepdims=True), s_new), sink)
        e_buf = jnp.exp(s_buf - m)
        e_new = jnp.exp(s_new - m)
        denom = jnp.sum(e_buf, axis=-1, keepdims=True) + e_new + jnp.exp(sink - m)
        o = jnp.dot(e_buf.astype(BF16), vb[:, lo:hi].astype(BF16), preferred_element_type=F32)
        o = o + e_new * vn
        o_ref[0, h * GROUP:(h + 1) * GROUP, :] = (o / denom).astype(o_ref.dtype)
    ko_ref[0, 0:W - 1, :] = kb[1:W, :]
    ko_ref[0, W - 1:W, :] = knf_ref[0]
    vo_ref[0, 0:W - 1, :] = vb[1:W, :]
    vo_ref[0, W - 1:W, :] = vnf_ref[0]


def swa_sample(q3, kn3, vn3, k_buf, v_buf, sinks):
    B = q3.shape[0]
    W, d = WINDOW, ATT_HEAD_DIM
    head_spec = lambda n: pl.BlockSpec((1, n, d), lambda b: (b, 0, 0))
    flat_spec = pl.BlockSpec((1, 1, KV_WIDTH), lambda b: (b, 0, 0))
    buf_spec = pl.BlockSpec((1, W, KV_WIDTH), lambda b: (b, 0, 0))
    return pl.pallas_call(
        _swa_sample_body,
        grid=(B,),
        in_specs=[head_spec(N_HEADS), head_spec(N_KV_HEADS), head_spec(N_KV_HEADS), flat_spec, flat_spec,
                  buf_spec, buf_spec, pl.BlockSpec((N_HEADS, 1), lambda b: (0, 0))],
        out_specs=[head_spec(N_HEADS), buf_spec, buf_spec],
        out_shape=[jax.ShapeDtypeStruct((B, N_HEADS, d), BF16),
                   jax.ShapeDtypeStruct((B, W, KV_WIDTH), F32),
                   jax.ShapeDtypeStruct((B, W, KV_WIDTH), F32)],
        compiler_params=_params("parallel"),
        name="swa_sample",
    )(q3, kn3, vn3, kn3.reshape(B, 1, KV_WIDTH), vn3.reshape(B, 1, KV_WIDTH), k_buf, v_buf,
      sinks.reshape(N_HEADS, 1))


def _shared_up_body(x_ref, wg_ref, wu_ref, h_ref):
    x = x_ref[...]
    g = jnp.dot(x, wg_ref[0].astype(BF16), preferred_element_type=F32)
    u = jnp.dot(x, wu_ref[0].astype(BF16), preferred_element_type=F32)
    h_ref[...] = (_silu(g) * u).astype(h_ref.dtype)


def _shared_down_body(h_ref, wd_ref, y_ref):
    y_ref[...] = jnp.dot(h_ref[...], wd_ref[0].astype(BF16), preferred_element_type=F32)


def shared_ffn(xb, w_gate, w_up, w_down, layer, tm=512):
    M, D = xb.shape
    FF = w_gate.shape[-1]
    tm = min(tm, M)
    wsel = lambda i: (layer, 0, 0)
    h = pl.pallas_call(
        _shared_up_body,
        grid=(M // tm,),
        in_specs=[pl.BlockSpec((tm, D), lambda i: (i, 0)),
                  pl.BlockSpec((1, D, FF), wsel), pl.BlockSpec((1, D, FF), wsel)],
        out_specs=pl.BlockSpec((tm, FF), lambda i: (i, 0)),
        out_shape=jax.ShapeDtypeStruct((M, FF), BF16),
        compiler_params=_params("parallel"),
        name="shared_up",
    )(xb, w_gate, w_up)
    return pl.pallas_call(
        _shared_down_body,
        grid=(M // tm,),
        in_specs=[pl.BlockSpec((tm, FF), lambda i: (i, 0)), pl.BlockSpec((1, FF, D), wsel)],
        out_specs=pl.BlockSpec((tm, D), lambda i: (i, 0)),
        out_shape=jax.ShapeDtypeStruct((M, D), F32),
        compiler_params=_params("parallel"),
        name="shared_down",
    )(h, w_down)


def _expert_changed(be_ref, i):
    return (i == 0) | (be_ref[i] != be_ref[jnp.maximum(i - 1, 0)])


GATHER_SLOTS = 3


def _stream_expert_weights(be_ref, nx_ref, i, w_hbm, stage, wsem, w_bf, layer):
    def copy(e, j):
        return pltpu.make_async_copy(w_hbm[j].at[layer, e], stage.at[j], wsem.at[j])

    @pl.when(i == 0)
    def _():
        for j in range(len(w_hbm)):
            copy(be_ref[0], j).start()

    @pl.when(_expert_changed(be_ref, i))
    def _():
        e, nxt = be_ref[i], nx_ref[i]
        for j in range(len(w_hbm)):
            copy(e, j).wait()
            w_bf[j][...] = stage[j].astype(BF16)

            @pl.when(nxt != e)
            def _():
                copy(nxt, j).start()


def _moe_up_body(be_ref, nx_ref, na_ref, tok_ref, x_hbm, wg_hbm, wu_hbm, h_ref, xbuf, sem, x_lhs, stage, wsem,
                 wg_bf, wu_bf, *, tm, layer):
    i = pl.program_id(0)
    na = na_ref[0]
    last_blk = pl.num_programs(0) - 1
    half = wg_bf.shape[0] // 2

    def issue(blk, slot):
        base = jnp.minimum(blk, last_blk) * tm
        for r in range(tm):
            src = pl.multiple_of(tok_ref[base + r], TOK_ROWS)
            pltpu.make_async_copy(x_hbm.at[pl.ds(src, TOK_ROWS)],
                                  xbuf.at[slot, pl.ds(r * TOK_ROWS, TOK_ROWS)], sem.at[slot]).start()

    def wait(slot):
        pltpu.make_async_copy(x_hbm.at[pl.ds(0, tm * TOK_ROWS)], xbuf.at[slot], sem.at[slot]).wait()

    @pl.when(i == 0)
    def _():
        issue(0, 0)
        issue(1, 1)

    @pl.when(i >= na)
    def _():
        h_ref[...] = jnp.zeros_like(h_ref)

    @pl.when(i < na)
    def _():
        slot = i % GATHER_SLOTS
        _stream_expert_weights(be_ref, nx_ref, i, (wg_hbm, wu_hbm), stage, wsem, (wg_bf, wu_bf), layer)
        wait(slot)
        for s in range(TOK_ROWS):
            lo, hi = _unpack_pair(_load_token_rows(xbuf.at[slot], s, tm))
            x_lhs[:, s * LANES:(s + 1) * LANES] = lo.astype(BF16)
            x_lhs[:, half + s * LANES:half + (s + 1) * LANES] = hi.astype(BF16)
        issue(i + 2, (i + 2) % GATHER_SLOTS)
        x = x_lhs[...]
        g = jnp.dot(x, wg_bf[...], preferred_element_type=F32)
        u = jnp.dot(x, wu_bf[...], preferred_element_type=F32)
        h_ref[...] = (_silu(g) * u).astype(h_ref.dtype)

        @pl.when(i + 1 == na)
        def _():
            wait((i + 1) % GATHER_SLOTS)
            wait((i + 2) % GATHER_SLOTS)


def _moe_down_body(be_ref, nx_ref, na_ref, dst_ref, h_ref, wd_hbm, y_hbm, ybuf, sem, stage, wsem, wd_bf, *, tm,
                   layer):
    i = pl.program_id(0)
    na = na_ref[0]

    def wait(slot):
        pltpu.make_async_copy(ybuf.at[slot], y_hbm.at[pl.ds(0, tm * TOK_ROWS)], sem.at[slot]).wait()

    def scatter(blk, slot):
        base = blk * tm
        for r in range(tm):
            dst = pl.multiple_of(dst_ref[base + r], TOK_ROWS)
            pltpu.make_async_copy(ybuf.at[slot, pl.ds(r * TOK_ROWS, TOK_ROWS)],
                                  y_hbm.at[pl.ds(dst, TOK_ROWS)], sem.at[slot]).start()

    def compute(slot):
        y = jnp.dot(h_ref[...], wd_bf[...], preferred_element_type=F32)
        _store_token_rows(ybuf.at[slot], y, tm)

    @pl.when(i == 0)
    def _():
        plane = y_hbm.shape[0] // TOP_K
        spare = 2 * tm // TOP_K * TOK_ROWS
        ybuf[0] = jnp.zeros(ybuf.shape[1:], ybuf.dtype)
        for k in range(TOP_K):
            fill = pltpu.make_async_copy(ybuf.at[0, pl.ds(0, spare)],
                                         y_hbm.at[pl.ds((k + 1) * plane - spare, spare)], sem.at[0])
            fill.start()
            fill.wait()

    @pl.when(i < na)
    def _():
        _stream_expert_weights(be_ref, nx_ref, i, (wd_hbm,), stage, wsem, (wd_bf,), layer)

        @pl.when(i == 0)
        def _():
            compute(0)

        @pl.when(i > 0)
        def _():
            slot = i % 2

            @pl.when(i >= 2)
            def _():
                wait(slot)

            scatter(i - 1, 1 - slot)
            compute(slot)

    @pl.when(i == pl.num_programs(0) - 1)
    def _():
        last = na - 1

        @pl.when(na >= 2)
        def _():
            wait(na % 2)

        scatter(last, last % 2)
        wait(last % 2)


def routed_ffn(x_packed, w_gate, w_up, w_down, layer, plan, tm):
    slot_token, slot_dest, block_e, next_e, n_active = plan
    T = x_packed.shape[0] // TOK_ROWS
    NB = block_e.shape[0]
    D, FF = w_gate.shape[-2], w_gate.shape[-1]
    t_pad = T + 2 * tm // TOP_K
    buf = lambda n: pltpu.VMEM((n, tm * TOK_ROWS, LANES), jnp.uint32)
    row = lambda i, be, nx, na, ix: (jnp.minimum(i, na[0] - 1), 0)
    hbm = pl.BlockSpec(memory_space=pl.ANY)
    params = pltpu.CompilerParams(dimension_semantics=("arbitrary",), vmem_limit_bytes=V7X_VMEM_LIMIT)
    h = pl.pallas_call(
        functools.partial(_moe_up_body, tm=tm, layer=layer),
        grid_spec=pltpu.PrefetchScalarGridSpec(
            num_scalar_prefetch=4, grid=(NB,),
            in_specs=[hbm, hbm, hbm],
            out_specs=pl.BlockSpec((tm, FF), lambda i, be, nx, na, ix: (i, 0)),
            scratch_shapes=[buf(GATHER_SLOTS), pltpu.SemaphoreType.DMA((GATHER_SLOTS,)),
                            pltpu.VMEM((tm, D), BF16),
                            pltpu.VMEM((2, D, FF), F32), pltpu.SemaphoreType.DMA((2,)),
                            pltpu.VMEM((D, FF), BF16), pltpu.VMEM((D, FF), BF16)]),
        out_shape=jax.ShapeDtypeStruct((NB * tm, FF), BF16),
        compiler_params=params,
        name="moe_up",
    )(block_e, next_e, n_active, slot_token * TOK_ROWS, x_packed, w_gate, w_up)
    y = pl.pallas_call(
        functools.partial(_moe_down_body, tm=tm, layer=layer),
        grid_spec=pltpu.PrefetchScalarGridSpec(
            num_scalar_prefetch=4, grid=(NB,),
            in_specs=[pl.BlockSpec((tm, FF), row), hbm],
            out_specs=hbm,
            scratch_shapes=[buf(2), pltpu.SemaphoreType.DMA((2,)),
                            pltpu.VMEM((1, FF, D), F32), pltpu.SemaphoreType.DMA((1,)),
                            pltpu.VMEM((FF, D), BF16)]),
        out_shape=jax.ShapeDtypeStruct((TOP_K * t_pad * TOK_ROWS, LANES), jnp.uint32),
        compiler_params=params,
        name="moe_down",
    )(block_e, next_e, n_active, slot_dest * TOK_ROWS, h, w_down)
    return y.reshape(TOP_K, t_pad * TOK_ROWS, LANES)


SLOT_MAP_CHUNK = 2048


def _slot_map_body(dest_ref, src_ref, *, clear_chunk):
    phase, i = pl.program_id(0), pl.program_id(1)

    @pl.when(phase == 0)
    def _():
        def clear(j, carry):
            src_ref[i * clear_chunk + j] = -1
            return carry

        lax.fori_loop(0, clear_chunk, clear, 0, unroll=8)

    @pl.when(phase == 1)
    def _():
        def place(a, carry):
            src_ref[dest_ref[a]] = i * SLOT_MAP_CHUNK + a
            return carry

        lax.fori_loop(0, SLOT_MAP_CHUNK, place, 0, unroll=8)


def _slot_map(dest, n_slots):
    A = dest.shape[0]
    n_chunks = -(-A // SLOT_MAP_CHUNK)
    clear_chunk = -(-(n_slots + 1) // (8 * n_chunks)) * 8
    dest = jnp.concatenate([dest, jnp.full((n_chunks * SLOT_MAP_CHUNK - A,), n_slots, jnp.int32)])
    src = pl.pallas_call(
        functools.partial(_slot_map_body, clear_chunk=clear_chunk),
        grid=(2, n_chunks),
        in_specs=[pl.BlockSpec((SLOT_MAP_CHUNK,), lambda ph, i: (i * ph,), memory_space=pltpu.SMEM)],
        out_specs=pl.BlockSpec(memory_space=pltpu.SMEM),
        out_shape=jax.ShapeDtypeStruct((n_chunks * clear_chunk,), jnp.int32),
        compiler_params=pltpu.CompilerParams(dimension_semantics=("arbitrary", "arbitrary")),
        name="slot_map",
    )(dest)
    return src[:n_slots]


def _dispatch_plan(idx, tm):
    T = idx.shape[0]
    A = T * TOP_K
    NB = -(-A // tm) + N_EXPERTS
    spare_t = 2 * tm // TOP_K
    t_pad = T + spare_t
    idx = idx.astype(jnp.int32)
    onehot = idx[:, :, None] == jnp.arange(N_EXPERTS, dtype=jnp.int32)[None, None, :]
    sel = onehot.any(axis=1).astype(jnp.int32)
    incl = jnp.cumsum(sel, axis=0)
    counts = incl[-1]
    padded = (counts + tm - 1) // tm * tm
    pad_end = jnp.cumsum(padded)
    pad_start = pad_end - padded
    n_active = (pad_end[-1] // tm).astype(jnp.int32)
    blk = jnp.arange(NB, dtype=jnp.int32)
    block_e = jnp.minimum(jnp.searchsorted(pad_end, jnp.minimum(blk, n_active - 1) * tm, side='right'),
                          N_EXPERTS - 1).astype(jnp.int32)
    e_ids = jnp.arange(N_EXPERTS, dtype=jnp.int32)
    later_used = lax.cummin(jnp.where(counts > 0, e_ids, N_EXPERTS), reverse=True)
    next_used = jnp.concatenate([later_used[1:], jnp.full((1,), N_EXPERTS, jnp.int32)])
    next_e = jnp.where(next_used[block_e] < N_EXPERTS, next_used[block_e], block_e).astype(jnp.int32)
    slot_of = (incl - sel + pad_start[None, :])[:, None, :]
    dest = jnp.sum(jnp.where(onehot, slot_of, 0), axis=-1).reshape(A)
    slot_src = _slot_map(dest, NB * tm)
    slot = jnp.arange(NB * tm, dtype=jnp.int32)
    valid = slot_src >= 0
    slot_token = jnp.where(valid, slot_src // TOP_K, 0)
    s2 = (slot // tm % 2) * tm + slot % tm
    spare = (s2 // spare_t) * t_pad + T + s2 % spare_t
    slot_dest = jnp.where(valid, (slot_src % TOP_K) * t_pad + slot_src // TOP_K, spare)
    return slot_token, slot_dest, block_e, next_e, n_active.reshape(1)


def moe_block(x_parts, xb_parts, xp_parts, p, layer):
    w_t = jnp.swapaxes(p['router_w'], 1, 2)
    bias_col = p['router_bias'][:, :, None]
    routed = [route(x, w_t, bias_col, layer) for x in x_parts]
    idx = jnp.concatenate([r[0] for r in routed], axis=1).T
    wts = jnp.concatenate([r[1] for r in routed], axis=1).T
    plan = _dispatch_plan(idx, MOE_TM)
    y = routed_ffn(jnp.concatenate(xp_parts, axis=0), p['exp_w_gate'], p['exp_w_up'], p['exp_w_down'],
                   layer, plan, MOE_TM)
    shared = [shared_ffn(xb, p['shared_w_gate'], p['shared_w_up'], p['shared_w_down'], layer)
              for xb in xb_parts]
    return y, wts, shared


def _rope_tables(pos, half):
    inv_freq = ROPE_THETA ** (-jnp.arange(half, dtype=F32) / half)
    ang = pos.astype(F32)[:, None] * inv_freq[None, :]
    return jnp.cos(ang), jnp.sin(ang)


def _even_prompt(x_bf, B, L, p, i):
    proj = matmul(x_bf, p['w_in_even'][i].astype(BF16), name="in_proj_even")
    proj3 = proj.reshape(B, L, -1)
    cos, sin = _rope_tables(jnp.arange(L, dtype=jnp.int32), RET_HEAD_DIM // 2)
    conv_out, conv_new = conv_prompt(proj3, p['conv_w'][i], p['conv_b'][i], p['conv_ln_g'][i], p['conv_ln_b'][i])
    ret_out, s_new = retention_prompt(proj3, cos, sin, p['ret_gn_g'][i], p['ret_gn_b'][i])
    mixed = jnp.concatenate([conv_out, ret_out], axis=-1).reshape(B * L, -1)
    return matmul(mixed, p['w_out_even'][i].astype(BF16), name="out_proj_even"), conv_new, s_new


def _even_sample(x_bf, conv_buf, ret_state, p, i):
    B = x_bf.shape[0]
    H, d = RET_HEADS, RET_HEAD_DIM
    proj = matmul(x_bf, p['w_in_even'][i].astype(BF16), name="in_proj_even")
    conv_out, conv_new = conv_sample(proj, conv_buf, p['conv_w'][i], p['conv_b'][i],
                                     p['conv_ln_g'][i], p['conv_ln_b'][i])
    o2 = 2 * CONV_CH
    q_col = proj[:, o2:o2 + RET_WIDTH].reshape(B, H, d, 1)
    k_col = proj[:, o2 + RET_WIDTH:o2 + 2 * RET_WIDTH].reshape(B, H, d, 1)
    v_row = proj[:, o2 + 2 * RET_WIDTH:o2 + 3 * RET_WIDTH].reshape(B, 1, RET_WIDTH)
    g_row = proj[:, o2 + 3 * RET_WIDTH:].reshape(B, 1, RET_WIDTH)
    cos, sin = _rope_tables(jnp.full((1,), PAST_LEN, jnp.int32), d // 2)
    ret_out, s_new = retention_sample(q_col, k_col, v_row, g_row, ret_state, cos.reshape(d // 2, 1),
                                      sin.reshape(d // 2, 1), p['ret_gn_g'][i], p['ret_gn_b'][i])
    mixed = jnp.concatenate([conv_out, ret_out.reshape(B, RET_WIDTH)], axis=-1)
    return matmul(mixed, p['w_out_even'][i].astype(BF16), name="out_proj_even"), conv_new, s_new


def _att_rope_tables(pos, rows):
    cos, sin = _rope_tables(pos, ATT_HEAD_DIM // 2)
    cos = jnp.tile(cos, (rows // cos.shape[0], LANES // cos.shape[1]))
    sin = jnp.tile(sin, (rows // sin.shape[0], LANES // sin.shape[1]))
    return cos, sin


def _odd_prompt(x_bf, B, L, p, i):
    cos, sin = _att_rope_tables(jnp.arange(L, dtype=jnp.int32), L)
    qkv = matmul(x_bf, p['w_qkv_odd'][i].astype(BF16), bias=p['b_qkv_odd'][i],
                 rope=(cos, sin, Q_WIDTH + KV_WIDTH), name="qkv_proj_odd")
    qkv3 = qkv.reshape(B, L, -1)
    o = swa_prompt(qkv3, p['sinks'][i])
    out = matmul(o.reshape(B * L, Q_WIDTH), p['w_out_odd'][i].astype(BF16), bias=p['b_out_odd'][i],
                 name="out_proj_odd")
    k_new = qkv3[:, L - WINDOW:, Q_WIDTH:Q_WIDTH + KV_WIDTH].reshape(B, WINDOW, N_KV_HEADS, ATT_HEAD_DIM)
    v_new = qkv3[:, L - WINDOW:, Q_WIDTH + KV_WIDTH:].reshape(B, WINDOW, N_KV_HEADS, ATT_HEAD_DIM)
    return out, k_new, v_new


def _odd_sample(x_bf, k_buf, v_buf, p, i):
    B = x_bf.shape[0]
    cos, sin = _att_rope_tables(jnp.full((1,), PAST_LEN, jnp.int32), B)
    qkv = matmul(x_bf, p['w_qkv_odd'][i].astype(BF16), bias=p['b_qkv_odd'][i],
                 rope=(cos, sin, Q_WIDTH + KV_WIDTH), name="qkv_proj_odd")
    q3 = qkv[:, :Q_WIDTH].reshape(B, N_HEADS, ATT_HEAD_DIM)
    kn3 = qkv[:, Q_WIDTH:Q_WIDTH + KV_WIDTH].reshape(B, N_KV_HEADS, ATT_HEAD_DIM)
    vn3 = qkv[:, Q_WIDTH + KV_WIDTH:].reshape(B, N_KV_HEADS, ATT_HEAD_DIM)
    o, k_new, v_new = swa_sample(q3, kn3, vn3, k_buf.reshape(B, WINDOW, KV_WIDTH),
                                 v_buf.reshape(B, WINDOW, KV_WIDTH), p['sinks'][i])
    out = matmul(o.reshape(B, Q_WIDTH), p['w_out_odd'][i].astype(BF16), bias=p['b_out_odd'][i],
                 name="out_proj_odd")
    shape = (B, WINDOW, N_KV_HEADS, ATT_HEAD_DIM)
    return out, k_new.reshape(shape), v_new.reshape(shape)


def kernel(x_prompt, x_sample, state_conv, state_ret, state_win_k, state_win_v,
           w_in_even, conv_w, conv_b, conv_ln_g, conv_ln_b, ret_gn_g, ret_gn_b, w_out_even,
           w_qkv_odd, b_qkv_odd, sinks, w_out_odd, b_out_odd, ln_g, ln_b,
           router_w, router_bias, exp_w_gate, exp_w_up, exp_w_down,
           shared_w_gate, shared_w_up, shared_w_down):
    p = dict(w_in_even=w_in_even, conv_w=conv_w, conv_b=conv_b, conv_ln_g=conv_ln_g, conv_ln_b=conv_ln_b,
             ret_gn_g=ret_gn_g, ret_gn_b=ret_gn_b, w_out_even=w_out_even,
             w_qkv_odd=w_qkv_odd, b_qkv_odd=b_qkv_odd, sinks=sinks, w_out_odd=w_out_odd, b_out_odd=b_out_odd,
             ln_g=ln_g, ln_b=ln_b, router_w=router_w, router_bias=router_bias,
             exp_w_gate=exp_w_gate, exp_w_up=exp_w_up, exp_w_down=exp_w_down,
             shared_w_gate=shared_w_gate, shared_w_up=shared_w_up, shared_w_down=shared_w_down)
    B, L, D = x_prompt.shape
    Bs = x_sample.shape[0]
    Tp = B * L
    xp = x_prompt.reshape(Tp, D)
    xs = x_sample.reshape(Bs, D)
    xp_bf, xs_bf = xp.astype(BF16), xs.astype(BF16)
    conv_p, conv_s, ret_p, ret_s, wk_p, wk_s, wv_p, wv_s = [], [], [], [], [], [], [], []
    for layer in range(DEPTH):
        i = layer // 2
        if layer % 2 == 0:
            hp, c_new, r_new = _even_prompt(xp_bf, B, L, p, i)
            conv_p.append(c_new)
            ret_p.append(r_new)
            hs, c_new, r_new = _even_sample(xs_bf, state_conv[i], state_ret[i], p, i)
            conv_s.append(c_new)
            ret_s.append(r_new)
        else:
            hp, k_new, v_new = _odd_prompt(xp_bf, B, L, p, i)
            wk_p.append(k_new)
            wv_p.append(v_new)
            hs, k_new, v_new = _odd_sample(xs_bf, state_win_k[i], state_win_v[i], p, i)
            wk_s.append(k_new)
            wv_s.append(v_new)
        xp, xp_bf, xp_pk = deepnorm(xp, [(hp, 0)], ln_g[layer, 0], ln_b[layer, 0], name="ln_mixer_prompt")
        xs, xs_bf, xs_pk = deepnorm(xs, [(hs, 0)], ln_g[layer, 0], ln_b[layer, 0], name="ln_mixer_sample")
        y, wts, shared = moe_block([xp, xs], [xp_bf, xs_bf], [xp_pk, xs_pk], p, layer)
        xp, xp_bf, _ = deepnorm(xp, [(shared[0], 0)], ln_g[layer, 1], ln_b[layer, 1], combine=(y, wts, 0),
                                tm=64, name="ln_moe_prompt")
        xs, xs_bf, _ = deepnorm(xs, [(shared[1], 0)], ln_g[layer, 1], ln_b[layer, 1], combine=(y, wts, Tp),
                                tm=32, name="ln_moe_sample")
    return (xp.reshape(B, L, D), xs.reshape(Bs, 1, D),
            jnp.stack(conv_p), jnp.stack(conv_s), jnp.stack(ret_p), jnp.stack(ret_s),
            jnp.stack(wk_p), jnp.stack(wk_s), jnp.stack(wv_p), jnp.stack(wv_s))
```

```python
import functools

import jax
import jax.numpy as jnp
import numpy as np
from jax import lax
from jax.experimental import pallas as pl
from jax.experimental.pallas import tpu as pltpu

F32 = jnp.float32
BF16 = jnp.bfloat16

D_MODEL = 4096
DEPTH = 2
PAST_LEN = 16384
CONV_CH = D_MODEL // 2
CONV_WIDTH = 31
CONV_HALO = 32
RET_HEADS = 8
RET_HEAD_DIM = 256
RET_WIDTH = RET_HEADS * RET_HEAD_DIM
RET_CHUNK = 128
RET_STEP_ROWS = 512
ATT_HEAD_DIM = 64
N_HEADS = 64
N_KV_HEADS = 8
GROUP = N_HEADS // N_KV_HEADS
WINDOW = 128
Q_WIDTH = N_HEADS * ATT_HEAD_DIM
KV_WIDTH = N_KV_HEADS * ATT_HEAD_DIM
ROPE_THETA = 10000.0
N_EXPERTS = 128
TOP_K = 8
N_GROUPS = 8
TOPK_GROUPS = 4
EXPERTS_PER_GROUP = N_EXPERTS // N_GROUPS
EXPERT_FF = 512
ROUTED_SCALE = 2.5
DN_ALPHA = (2.0 * DEPTH) ** 0.25
LN_EPS = 1e-5

LANES = 128
V7X_VMEM_LIMIT = 52 * 1024 * 1024
MOE_TM = 256


def _params(*sem):
    return pltpu.CompilerParams(dimension_semantics=sem, vmem_limit_bytes=V7X_VMEM_LIMIT)


def _silu(x):
    return x * jax.nn.sigmoid(x)


def _rot_half_64(x):
    n = x.shape[-1]
    lane = lax.broadcasted_iota(jnp.int32, x.shape, x.ndim - 1)
    fwd = pltpu.roll(x, n - ATT_HEAD_DIM // 2, x.ndim - 1)
    bwd = pltpu.roll(x, ATT_HEAD_DIM // 2, x.ndim - 1)
    return jnp.where(lane % ATT_HEAD_DIM < ATT_HEAD_DIM // 2, -fwd, bwd)


def _mm_body(*refs, has_bias, rope_tiles):
    x_ref, w_ref = refs[0], refs[1]
    pos = 2
    b_ref = None
    if has_bias:
        b_ref = refs[pos]
        pos += 1
    if rope_tiles:
        cos_ref, sin_ref = refs[pos], refs[pos + 1]
        pos += 2
    o_ref = refs[pos]
    acc = jnp.dot(x_ref[...].astype(BF16), w_ref[...].astype(BF16), preferred_element_type=F32)
    if has_bias:
        acc = acc + b_ref[...]
    if rope_tiles:
        reps = acc.shape[1] // LANES
        cos = jnp.concatenate([cos_ref[...]] * reps, axis=1)
        sin = jnp.concatenate([sin_ref[...]] * reps, axis=1)
        @pl.when(pl.program_id(1) < rope_tiles)
        def _():
            o_ref[...] = (acc * cos + _rot_half_64(acc) * sin).astype(o_ref.dtype)

        @pl.when(pl.program_id(1) >= rope_tiles)
        def _():
            o_ref[...] = acc.astype(o_ref.dtype)
    else:
        o_ref[...] = acc.astype(o_ref.dtype)


def matmul(x, w, bias=None, rope=None, out_dtype=F32, tm=1024, tn=512, name="matmul"):
    M, K = x.shape
    N = w.shape[1]
    tm = min(tm, M)
    tn = min(tn, N)
    assert M % tm == 0 and N % tn == 0
    in_specs = [pl.BlockSpec((tm, K), lambda i, j: (i, 0)),
                pl.BlockSpec((K, tn), lambda i, j: (0, j))]
    args = [x, w]
    if bias is not None:
        in_specs.append(pl.BlockSpec((1, tn), lambda i, j: (0, j)))
        args.append(bias.reshape(1, N))
    rope_tiles = 0
    if rope is not None:
        cos, sin, n_cols = rope
        assert n_cols % tn == 0 and cos.shape[0] % tm == 0
        rope_tiles = n_cols // tn
        nrb = cos.shape[0] // tm
        in_specs += [pl.BlockSpec((tm, LANES), lambda i, j: (i % nrb, 0))] * 2
        args += [cos, sin]
    return pl.pallas_call(
        functools.partial(_mm_body, has_bias=bias is not None, rope_tiles=rope_tiles),
        grid=(M // tm, N // tn),
        in_specs=in_specs,
        out_specs=pl.BlockSpec((tm, tn), lambda i, j: (i, j)),
        out_shape=jax.ShapeDtypeStruct((M, N), out_dtype),
        compiler_params=_params("parallel", "arbitrary"),
        name=name,
    )(*args)


def _split_bf16(x):
    hi = x.astype(BF16)
    lo = (x - hi.astype(F32)).astype(BF16)
    return hi, lo


def _first_argmax(x, row, n):
    m = jnp.max(x, axis=0, keepdims=True)
    i = jnp.min(jnp.where(x == m, row, n), axis=0, keepdims=True)
    return m, i


def _router_body(x_ref, wt_ref, bias_ref, idx_ref, wts_ref):
    nt = (((1,), (1,)), ((), ()))
    xh, xl = _split_bf16(x_ref[...])
    wh, wl = _split_bf16(wt_ref[0])
    logits = lax.dot_general(wh, xh, nt, preferred_element_type=F32)
    logits = logits + lax.dot_general(wh, xl, nt, preferred_element_type=F32)
    logits = logits + lax.dot_general(wl, xh, nt, preferred_element_type=F32)
    scores = jax.nn.sigmoid(logits)
    biased = scores + bias_ref[0]
    tm = scores.shape[1]
    neg = -jnp.inf
    row_g = lax.broadcasted_iota(jnp.int32, (EXPERTS_PER_GROUP, tm), 0)
    grp = []
    for g in range(N_GROUPS):
        xg = biased[g * EXPERTS_PER_GROUP:(g + 1) * EXPERTS_PER_GROUP]
        m1, i1 = _first_argmax(xg, row_g, EXPERTS_PER_GROUP)
        m2 = jnp.max(jnp.where(row_g == i1, neg, xg), axis=0, keepdims=True)
        grp.append(m1 + m2)
    gs = jnp.concatenate(grp, axis=0)
    row_n = lax.broadcasted_iota(jnp.int32, (N_GROUPS, tm), 0)
    chosen = jnp.zeros((N_GROUPS, tm), F32)
    for _ in range(TOPK_GROUPS):
        _, i = _first_argmax(gs, row_n, N_GROUPS)
        hit = row_n == i
        chosen = jnp.where(hit, 1.0, chosen)
        gs = jnp.where(hit, neg, gs)
    emask = jnp.concatenate([jnp.broadcast_to(chosen[g:g + 1], (EXPERTS_PER_GROUP, tm))
                             for g in range(N_GROUPS)], axis=0)
    cand = jnp.where(emask > 0.0, biased, neg)
    row_e = lax.broadcasted_iota(jnp.int32, (N_EXPERTS, tm), 0)
    ids, ws = [], []
    for _ in range(TOP_K):
        _, i = _first_argmax(cand, row_e, N_EXPERTS)
        hit = row_e == i
        ids.append(i)
        ws.append(jnp.sum(jnp.where(hit, scores, 0.0), axis=0, keepdims=True))
        cand = jnp.where(hit, neg, cand)
    w = jnp.concatenate(ws, axis=0)
    idx_ref[...] = jnp.concatenate(ids, axis=0)
    wts_ref[...] = w / jnp.sum(w, axis=0, keepdims=True) * ROUTED_SCALE


def route(x, w_t, bias_col, layer, tm=512):
    M, K = x.shape
    E = w_t.shape[1]
    tm = min(tm, M)
    return pl.pallas_call(
        _router_body,
        grid=(M // tm,),
        in_specs=[pl.BlockSpec((tm, K), lambda i: (i, 0)),
                  pl.BlockSpec((1, E, K), lambda i: (layer, 0, 0)),
                  pl.BlockSpec((1, E, 1), lambda i: (layer, 0, 0))],
        out_specs=[pl.BlockSpec((TOP_K, tm), lambda i: (0, i))] * 2,
        out_shape=[jax.ShapeDtypeStruct((TOP_K, M), jnp.int32), jax.ShapeDtypeStruct((TOP_K, M), F32)],
        compiler_params=_params("parallel"),
        name="router",
    )(x, w_t, bias_col)


def _layer_norm_rows(z, g, b):
    mu = jnp.mean(z, axis=-1, keepdims=True)
    zc = z - mu
    var = jnp.mean(zc * zc, axis=-1, keepdims=True)
    return zc * lax.rsqrt(var + LN_EPS) * g + b


def _unpack_pair(w):
    lo = lax.bitcast_convert_type(w << 16, F32)
    hi = lax.bitcast_convert_type(w & jnp.uint32(0xFFFF0000), F32)
    return lo, hi


def _pack_pair(lo, hi):
    lo_bits = lax.bitcast_convert_type(lo.astype(BF16).astype(F32), jnp.uint32) >> 16
    hi_bits = lax.bitcast_convert_type(hi.astype(BF16).astype(F32), jnp.uint32) & jnp.uint32(0xFFFF0000)
    return hi_bits | lo_bits


TOK_ROWS = D_MODEL // 2 // LANES


def _load_token_rows(ref, s, tm):
    return ref[pl.ds(s, tm, stride=TOK_ROWS), :]


def _store_token_rows(ref, y, tm):
    half = y.shape[1] // 2
    for s in range(TOK_ROWS):
        words = _pack_pair(y[:, s * LANES:(s + 1) * LANES], y[:, half + s * LANES:half + (s + 1) * LANES])
        ref[pl.ds(s, tm, stride=TOK_ROWS), :] = words


def _ln_body(*refs, n_h, combine):
    x_ref = refs[0]
    h_refs = refs[1:1 + n_h]
    pos = 1 + n_h
    if combine:
        y_ref, w_ref = refs[pos], refs[pos + 1]
        pos += 2
    g_ref, b_ref, o_ref, ob_ref, op_ref = refs[pos:]
    tm, d = x_ref.shape
    half = d // 2
    z = DN_ALPHA * x_ref[...]
    for h_ref in h_refs:
        z = z + h_ref[...]
    if combine:
        wk = [jnp.broadcast_to(w_ref[:, k:k + 1], (tm, LANES)) for k in range(TOP_K)]
        cols_lo, cols_hi = [], []
        for s in range(TOK_ROWS):
            acc_lo = z[:, s * LANES:(s + 1) * LANES]
            acc_hi = z[:, half + s * LANES:half + (s + 1) * LANES]
            for k in range(TOP_K):
                lo, hi = _unpack_pair(_load_token_rows(y_ref.at[k], s, tm))
                acc_lo = acc_lo + wk[k] * lo
                acc_hi = acc_hi + wk[k] * hi
            cols_lo.append(acc_lo)
            cols_hi.append(acc_hi)
        z = jnp.concatenate(cols_lo + cols_hi, axis=1)
    y = _layer_norm_rows(z, g_ref[...], b_ref[...])
    o_ref[...] = y
    ob_ref[...] = y.astype(BF16)
    _store_token_rows(op_ref, y, tm)


def deepnorm(x, hs, g, b, combine=None, tm=256, name="deepnorm"):
    M, D = x.shape
    tm = min(tm, M)
    assert M % tm == 0
    row_spec = lambda width, off: pl.BlockSpec((tm, width), functools.partial(lambda i, o: (i + o, 0), o=off // tm))
    in_specs = [row_spec(D, 0)]
    args = [x]
    for h, off in hs:
        assert off % tm == 0
        in_specs.append(row_spec(D, off))
        args.append(h)
    if combine is not None:
        y, wts, off = combine
        assert off % tm == 0
        y_spec = pl.BlockSpec((TOP_K, tm * TOK_ROWS, LANES),
                              functools.partial(lambda i, o: (0, i + o, 0), o=off // tm))
        in_specs += [y_spec, row_spec(TOP_K, off)]
        args += [y, wts]
    in_specs += [pl.BlockSpec((1, D), lambda i: (0, 0))] * 2
    args += [g.reshape(1, D), b.reshape(1, D)]
    return pl.pallas_call(
        functools.partial(_ln_body, n_h=len(hs), combine=combine is not None),
        grid=(M // tm,),
        in_specs=in_specs,
        out_specs=[row_spec(D, 0), row_spec(D, 0), pl.BlockSpec((tm * TOK_ROWS, LANES), lambda i: (i, 0))],
        out_shape=[jax.ShapeDtypeStruct((M, D), F32), jax.ShapeDtypeStruct((M, D), BF16),
                   jax.ShapeDtypeStruct((M * TOK_ROWS, LANES), jnp.uint32)],
        compiler_params=_params("parallel"),
        name=name,
    )(*args)


def _conv_prompt_body(a_ref, g_ref, pa_ref, pg_ref, w_ref, bdw_ref, lng_ref, lnb_ref,
                      o_ref, st_ref, full_ref, y_ref, *, tl):
    i = pl.program_id(1)
    u = a_ref[0] * jax.nn.sigmoid(g_ref[0])
    up = pa_ref[0] * jax.nn.sigmoid(pg_ref[0])
    up = jnp.where(i > 0, up, 0.0)
    full_ref[0:CONV_HALO, :] = up
    full_ref[CONV_HALO:CONV_HALO + tl, :] = u
    base = CONV_HALO - (CONV_WIDTH - 1)
    n_col = u.shape[1] // LANES

    def col_step(c, carry):
        off = pl.multiple_of(c * LANES, LANES)
        acc = jnp.zeros((tl, LANES), F32)
        for j in range(CONV_WIDTH):
            acc = acc + w_ref[j:j + 1, pl.ds(off, LANES)] * full_ref[base + j:base + j + tl, pl.ds(off, LANES)]
        y_ref[:, pl.ds(off, LANES)] = acc
        return carry

    lax.fori_loop(0, n_col, col_step, 0)
    y = _layer_norm_rows(y_ref[...] + bdw_ref[...], lng_ref[...], lnb_ref[...])
    o_ref[0] = _silu(y).astype(o_ref.dtype)

    @pl.when(i == pl.num_programs(1) - 1)
    def _():
        st_ref[0] = full_ref[CONV_HALO + tl - (CONV_WIDTH - 1):CONV_HALO + tl, :]


def conv_prompt(proj3, w_dw, b_dw, ln_g, ln_b, tl=128):
    B, L, _ = proj3.shape
    C = CONV_CH
    nh = tl // CONV_HALO
    vec = lambda v: v.reshape(1, C)
    return pl.pallas_call(
        functools.partial(_conv_prompt_body, tl=tl),
        grid=(B, L // tl),
        in_specs=[
            pl.BlockSpec((1, tl, C), lambda b, i: (b, i, 0)),
            pl.BlockSpec((1, tl, C), lambda b, i: (b, i, 1)),
            pl.BlockSpec((1, CONV_HALO, C), lambda b, i: (b, jnp.maximum(i * nh - 1, 0), 0)),
            pl.BlockSpec((1, CONV_HALO, C), lambda b, i: (b, jnp.maximum(i * nh - 1, 0), 1)),
            pl.BlockSpec((CONV_WIDTH, C), lambda b, i: (0, 0)),
            pl.BlockSpec((1, C), lambda b, i: (0, 0)),
            pl.BlockSpec((1, C), lambda b, i: (0, 0)),
            pl.BlockSpec((1, C), lambda b, i: (0, 0)),
        ],
        out_specs=[
            pl.BlockSpec((1, tl, C), lambda b, i: (b, i, 0)),
            pl.BlockSpec((1, CONV_WIDTH - 1, C), lambda b, i: (b, 0, 0)),
        ],
        out_shape=[jax.ShapeDtypeStruct((B, L, C), BF16),
                   jax.ShapeDtypeStruct((B, CONV_WIDTH - 1, C), F32)],
        scratch_shapes=[pltpu.VMEM((CONV_HALO + tl, C), F32), pltpu.VMEM((tl, C), F32)],
        compiler_params=_params("parallel", "arbitrary"),
        name="conv_prompt",
    )(proj3, proj3, proj3, proj3, w_dw, vec(b_dw), vec(ln_g), vec(ln_b))


def _conv_sample_body(a_ref, g_ref, buf_ref, w_ref, bdw_ref, lng_ref, lnb_ref, o_ref, st_ref):
    u = a_ref[...] * jax.nn.sigmoid(g_ref[...])
    nb = CONV_WIDTH - 1
    acc = w_ref[nb:nb + 1, :] * u
    for j in range(nb):
        acc = acc + w_ref[j:j + 1, :] * buf_ref[:, j, :]
    y = _layer_norm_rows(acc + bdw_ref[...], lng_ref[...], lnb_ref[...])
    o_ref[...] = _silu(y).astype(o_ref.dtype)
    st_ref[:, 0:nb - 1, :] = buf_ref[:, 1:nb, :]
    st_ref[:, nb - 1, :] = u


def conv_sample(proj, buf, w_dw, b_dw, ln_g, ln_b, tb=8):
    B = proj.shape[0]
    C = CONV_CH
    nb = CONV_WIDTH - 1
    vec = lambda v: v.reshape(1, C)
    return pl.pallas_call(
        _conv_sample_body,
        grid=(B // tb,),
        in_specs=[
            pl.BlockSpec((tb, C), lambda b: (b, 0)),
            pl.BlockSpec((tb, C), lambda b: (b, 1)),
            pl.BlockSpec((tb, nb, C), lambda b: (b, 0, 0)),
            pl.BlockSpec((CONV_WIDTH, C), lambda b: (0, 0)),
            pl.BlockSpec((1, C), lambda b: (0, 0)),
            pl.BlockSpec((1, C), lambda b: (0, 0)),
            pl.BlockSpec((1, C), lambda b: (0, 0)),
        ],
        out_specs=[pl.BlockSpec((tb, C), lambda b: (b, 0)),
                   pl.BlockSpec((tb, nb, C), lambda b: (b, 0, 0))],
        out_shape=[jax.ShapeDtypeStruct((B, C), BF16), jax.ShapeDtypeStruct((B, nb, C), F32)],
        compiler_params=_params("parallel"),
        name="conv_sample",
    )(proj, proj, buf, w_dw, vec(b_dw), vec(ln_g), vec(ln_b))


def _ret_log_gamma():
    return np.log1p(-np.exp2(-5.0 - np.arange(RET_HEADS, dtype=np.float64)))


def _group_norm_gate(o, gate, gn_g, gn_b):
    mu = jnp.mean(o, axis=-1, keepdims=True)
    oc = o - mu
    var = jnp.mean(oc * oc, axis=-1, keepdims=True)
    return (oc * lax.rsqrt(var + LN_EPS) * gn_g + gn_b) * _silu(gate)


def _rope_256(x, cos, sin):
    half = RET_HEAD_DIM // 2
    x1, x2 = x[..., :half], x[..., half:]
    return jnp.concatenate([x1 * cos - x2 * sin, x2 * cos + x1 * sin], axis=-1)


def _ret_prompt_body(q_ref, k_ref, v_ref, g_ref, cos_ref, sin_ref, intra_ref, qd_ref, kd_ref, cd_ref,
                     gng_ref, gnb_ref, o_ref, s_ref):
    c = pl.program_id(2)

    @pl.when(c == 0)
    def _():
        s_ref[...] = jnp.zeros_like(s_ref)

    cw = RET_CHUNK
    for j in range(q_ref.shape[1] // cw):
        rows = slice(j * cw, (j + 1) * cw)
        cos, sin = cos_ref[rows, :], sin_ref[rows, :]
        q = _rope_256(q_ref[0, rows, :], cos, sin)
        k = _rope_256(k_ref[0, rows, :], cos, sin) * (RET_HEAD_DIM ** -0.5)
        vb = v_ref[0, rows, :].astype(BF16)
        s_prev = s_ref[0, 0]
        scores = lax.dot_general(q.astype(BF16), k.astype(BF16), (((1,), (1,)), ((), ())),
                                 preferred_element_type=F32) * intra_ref[0]
        o = jnp.dot(scores.astype(BF16), vb, preferred_element_type=F32)
        o = o + jnp.dot((q * qd_ref[0]).astype(BF16), s_prev.astype(BF16), preferred_element_type=F32)
        kdt = (k * kd_ref[0]).T.astype(BF16)
        kv = jnp.dot(kdt, vb, preferred_element_type=F32)
        s_ref[0, 0] = cd_ref[0] * s_prev + kv
        o_ref[0, rows, :] = _group_norm_gate(o, g_ref[0, rows, :], gng_ref[...], gnb_ref[...]).astype(o_ref.dtype)


def retention_prompt(proj3, cos, sin, gn_g, gn_b):
    B, L, _ = proj3.shape
    H, d, c = RET_HEADS, RET_HEAD_DIM, RET_CHUNK
    lg = _ret_log_gamma()
    idx = np.arange(c, dtype=np.float64)
    rel = idx[:, None] - idx[None, :]
    intra = np.where(rel[None] >= 0, np.exp(lg[:, None, None] * np.maximum(rel, 0.0)[None]), 0.0)
    ones = np.ones((1, 1, d))
    qd = np.exp(lg[:, None] * (idx + 1.0)[None, :])[:, :, None] * ones
    kd = np.exp(lg[:, None] * (c - 1.0 - idx)[None, :])[:, :, None] * ones
    cd = np.exp(lg * c)[:, None, None] * ones
    col0 = 2 * CONV_CH // d
    tl = min(RET_STEP_ROWS, L)
    assert L % tl == 0 and tl % c == 0
    head_spec = lambda off: pl.BlockSpec((1, tl, d), lambda b, h, n: (b, n, col0 + off * H + h))
    tab_spec = pl.BlockSpec((1, c, d), lambda b, h, n: (h, 0, 0))
    return pl.pallas_call(
        _ret_prompt_body,
        grid=(B, H, L // tl),
        in_specs=[
            head_spec(0), head_spec(1), head_spec(2), head_spec(3),
            pl.BlockSpec((tl, d // 2), lambda b, h, n: (n, 0)),
            pl.BlockSpec((tl, d // 2), lambda b, h, n: (n, 0)),
            pl.BlockSpec((1, c, c), lambda b, h, n: (h, 0, 0)),
            tab_spec, tab_spec,
            pl.BlockSpec((1, 1, d), lambda b, h, n: (h, 0, 0)),
            pl.BlockSpec((1, d), lambda b, h, n: (0, h)),
            pl.BlockSpec((1, d), lambda b, h, n: (0, h)),
        ],
        out_specs=[
            pl.BlockSpec((1, tl, d), lambda b, h, n: (b, n, h)),
            pl.BlockSpec((1, 1, d, d), lambda b, h, n: (b, h, 0, 0)),
        ],
        out_shape=[jax.ShapeDtypeStruct((B, L, H * d), BF16),
                   jax.ShapeDtypeStruct((B, H, d, d), F32)],
        compiler_params=_params("parallel", "parallel", "arbitrary"),
        name="retention_prompt",
    )(proj3, proj3, proj3, proj3, cos, sin,
      jnp.asarray(intra, F32), jnp.asarray(qd, F32), jnp.asarray(kd, F32), jnp.asarray(cd, F32),
      gn_g.reshape(1, H * d), gn_b.reshape(1, H * d))


def _ret_sample_body(qc_ref, kc_ref, v_ref, g_ref, s_ref, cosc_ref, sinc_ref, gam_ref, gng_ref, gnb_ref,
                     o_ref, so_ref):
    half = RET_HEAD_DIM // 2
    cos, sin = cosc_ref[...], sinc_ref[...]

    def rope_col(x):
        x1, x2 = x[:half], x[half:]
        return jnp.concatenate([x1 * cos - x2 * sin, x2 * cos + x1 * sin], axis=0)

    for h in range(RET_HEADS):
        lo, hi = h * RET_HEAD_DIM, (h + 1) * RET_HEAD_DIM
        q = rope_col(qc_ref[0, h])
        k = rope_col(kc_ref[0, h]) * (RET_HEAD_DIM ** -0.5)
        v = v_ref[0, :, lo:hi]
        s_new = gam_ref[h] * s_ref[0, h] + k * v
        so_ref[0, h] = s_new
        o = jnp.sum(q * s_new, axis=0, keepdims=True)
        o_ref[0, :, lo:hi] = _group_norm_gate(o, g_ref[0, :, lo:hi], gng_ref[:, lo:hi],
                                              gnb_ref[:, lo:hi]).astype(o_ref.dtype)


def retention_sample(q_col, k_col, v_row, g_row, state, cos_col, sin_col, gn_g, gn_b):
    B = state.shape[0]
    H, d = RET_HEADS, RET_HEAD_DIM
    gam = jnp.asarray(np.exp(_ret_log_gamma()), F32)
    col_spec = pl.BlockSpec((1, H, d, 1), lambda b: (b, 0, 0, 0))
    row_spec = pl.BlockSpec((1, 1, H * d), lambda b: (b, 0, 0))
    return pl.pallas_call(
        _ret_sample_body,
        grid=(B,),
        in_specs=[
            col_spec, col_spec, row_spec, row_spec,
            pl.BlockSpec((1, H, d, d), lambda b: (b, 0, 0, 0)),
            pl.BlockSpec((d // 2, 1), lambda b: (0, 0)),
            pl.BlockSpec((d // 2, 1), lambda b: (0, 0)),
            pl.BlockSpec(memory_space=pltpu.SMEM),
            pl.BlockSpec((1, H * d), lambda b: (0, 0)),
            pl.BlockSpec((1, H * d), lambda b: (0, 0)),
        ],
        out_specs=[row_spec, pl.BlockSpec((1, H, d, d), lambda b: (b, 0, 0, 0))],
        out_shape=[jax.ShapeDtypeStruct((B, 1, H * d), BF16), jax.ShapeDtypeStruct((B, H, d, d), F32)],
        compiler_params=_params("parallel"),
        name="retention_sample",
    )(q_col, k_col, v_row, g_row, state, cos_col, sin_col, gam, gn_g.reshape(1, H * d), gn_b.reshape(1, H * d))


def _pair_operand(x2, h, lane):
    col = x2[:, (h // 2) * LANES:(h // 2 + 1) * LANES]
    swapped = pltpu.roll(col, ATT_HEAD_DIM, 1)
    first, second = (col, swapped) if h % 2 == 0 else (swapped, col)
    return jnp.concatenate([jnp.where(lane < ATT_HEAD_DIM, first, 0.0),
                            jnp.where(lane >= ATT_HEAD_DIM, second, 0.0)], axis=0).astype(BF16)


def _swa_prompt_body(sink_ref, q_ref, kc_ref, kp_ref, vc_ref, vp_ref, o_ref):
    j = pl.program_id(1)
    W = WINDOW
    k2 = jnp.concatenate([kp_ref[0], kc_ref[0]], axis=0)
    v2 = jnp.concatenate([vp_ref[0], vc_ref[0]], axis=0)
    row = lax.broadcasted_iota(jnp.int32, (W, 2 * W), 0)
    col = lax.broadcasted_iota(jnp.int32, (W, 2 * W), 1)
    first_key = jnp.where(j > 0, 0, W)
    mask = (col >= jnp.maximum(row, first_key)) & (col <= W + row)
    lane_kv = lax.broadcasted_iota(jnp.int32, (2 * W, LANES), 1)
    lane_o = lax.broadcasted_iota(jnp.int32, (W, LANES), 1)
    scale = ATT_HEAD_DIM ** -0.5
    for h in range(N_KV_HEADS):
        kk = _pair_operand(k2, h, lane_kv)
        vv = _pair_operand(v2, h, lane_kv)
        for p in range(GROUP // 2):
            c0 = h * GROUP * ATT_HEAD_DIM + p * LANES
            qp = q_ref[0, :, c0:c0 + LANES].astype(BF16)
            s = lax.dot_general(qp, kk, (((1,), (1,)), ((), ())), preferred_element_type=F32) * scale
            probs, inv = [], []
            for t in range(2):
                sink = sink_ref[h * GROUP + 2 * p + t]
                st = jnp.where(mask, s[:, t * 2 * W:(t + 1) * 2 * W], -jnp.inf)
                m = jnp.maximum(jnp.max(st, axis=-1, keepdims=True), sink)
                e = jnp.exp(st - m)
                denom = jnp.sum(e, axis=-1, keepdims=True) + jnp.exp(sink - m)
                probs.append(e.astype(BF16))
                inv.append(1.0 / denom)
            o2 = jnp.dot(jnp.concatenate(probs, axis=1), vv, preferred_element_type=F32)
            o2 = o2 * jnp.where(lane_o < ATT_HEAD_DIM, inv[0], inv[1])
            o_ref[0, :, c0:c0 + LANES] = o2.astype(o_ref.dtype)


def swa_prompt(qkv3, sinks):
    B, L, _ = qkv3.shape
    W = WINDOW
    kcol = Q_WIDTH // KV_WIDTH
    return pl.pallas_call(
        _swa_prompt_body,
        grid_spec=pltpu.PrefetchScalarGridSpec(
            num_scalar_prefetch=0,
            grid=(B, L // W),
            in_specs=[
                pl.BlockSpec(memory_space=pltpu.SMEM),
                pl.BlockSpec((1, W, Q_WIDTH), lambda b, j: (b, j, 0)),
                pl.BlockSpec((1, W, KV_WIDTH), lambda b, j: (b, j, kcol)),
                pl.BlockSpec((1, W, KV_WIDTH), lambda b, j: (b, jnp.maximum(j - 1, 0), kcol)),
                pl.BlockSpec((1, W, KV_WIDTH), lambda b, j: (b, j, kcol + 1)),
                pl.BlockSpec((1, W, KV_WIDTH), lambda b, j: (b, jnp.maximum(j - 1, 0), kcol + 1)),
            ],
            out_specs=pl.BlockSpec((1, W, Q_WIDTH), lambda b, j: (b, j, 0)),
        ),
        out_shape=jax.ShapeDtypeStruct((B, L, Q_WIDTH), BF16),
        compiler_params=_params("parallel", "parallel"),
        name="swa_prompt",
    )(sinks, qkv3, qkv3, qkv3, qkv3, qkv3)


def _swa_sample_body(q_ref, kn_ref, vn_ref, knf_ref, vnf_ref, kb_ref, vb_ref, sink_ref, o_ref, ko_ref, vo_ref):
    W = WINDOW
    scale = ATT_HEAD_DIM ** -0.5
    kb = kb_ref[0]
    vb = vb_ref[0]
    for h in range(N_KV_HEADS):
        lo, hi = h * ATT_HEAD_DIM, (h + 1) * ATT_HEAD_DIM
        qh = q_ref[0, h * GROUP:(h + 1) * GROUP, :]
        kn = kn_ref[0, h:h + 1, :]
        vn = vn_ref[0, h:h + 1, :]
        sink = sink_ref[h * GROUP:(h + 1) * GROUP, :]
        s_buf = lax.dot_general(qh.astype(BF16), kb[:, lo:hi].astype(BF16), (((1,), (1,)), ((), ())),
                                preferred_element_type=F32) * scale
        s_new = jnp.sum(qh * kn, axis=-1, keepdims=True) * scale
        m = jnp.maximum(jnp.maximum(jnp.max(s_buf, axis=-1, keepdims=True), s_new), sink)
        e_buf = jnp.exp(s_buf - m)
        e_new = jnp.exp(s_new - m)
        denom = jnp.sum(e_buf, axis=-1, keepdims=True) + e_new + jnp.exp(sink - m)
        o = jnp.dot(e_buf.astype(BF16), vb[:, lo:hi].astype(BF16), preferred_element_type=F32)
        o = o + e_new * vn
        o_ref[0, h * GROUP:(h + 1) * GROUP, :] = (o / denom).astype(o_ref.dtype)
    ko_ref[0, 0:W - 1, :] = kb[1:W, :]
    ko_ref[0, W - 1:W, :] = knf_ref[0]
    vo_ref[0, 0:W - 1, :] = vb[1:W, :]
    vo_ref[0, W - 1:W, :] = vnf_ref[0]


def swa_sample(q3, kn3, vn3, k_buf, v_buf, sinks):
    B = q3.shape[0]
    W, d = WINDOW, ATT_HEAD_DIM
    head_spec = lambda n: pl.BlockSpec((1, n, d), lambda b: (b, 0, 0))
    flat_spec = pl.BlockSpec((1, 1, KV_WIDTH), lambda b: (b, 0, 0))
    buf_spec = pl.BlockSpec((1, W, KV_WIDTH), lambda b: (b, 0, 0))
    return pl.pallas_call(
        _swa_sample_body,
        grid=(B,),
        in_specs=[head_spec(N_HEADS), head_spec(N_KV_HEADS), head_spec(N_KV_HEADS), flat_spec, flat_spec,
                  buf_spec, buf_spec, pl.BlockSpec((N_HEADS, 1), lambda b: (0, 0))],
        out_specs=[head_spec(N_HEADS), buf_spec, buf_spec],
        out_shape=[jax.ShapeDtypeStruct((B, N_HEADS, d), BF16),
                   jax.ShapeDtypeStruct((B, W, KV_WIDTH), F32),
                   jax.ShapeDtypeStruct((B, W, KV_WIDTH), F32)],
        compiler_params=_params("parallel"),
        name="swa_sample",
    )(q3, kn3, vn3, kn3.reshape(B, 1, KV_WIDTH), vn3.reshape(B, 1, KV_WIDTH), k_buf, v_buf,
      sinks.reshape(N_HEADS, 1))


def _shared_up_body(x_ref, wg_ref, wu_ref, h_ref):
    x = x_ref[...]
    g = jnp.dot(x, wg_ref[0].astype(BF16), preferred_element_type=F32)
    u = jnp.dot(x, wu_ref[0].astype(BF16), preferred_element_type=F32)
    h_ref[...] = (_silu(g) * u).astype(h_ref.dtype)


def _shared_down_body(h_ref, wd_ref, y_ref):
    y_ref[...] = jnp.dot(h_ref[...], wd_ref[0].astype(BF16), preferred_element_type=F32)


def shared_ffn(xb, w_gate, w_up, w_down, layer, tm=512):
    M, D = xb.shape
    FF = w_gate.shape[-1]
    tm = min(tm, M)
    wsel = lambda i: (layer, 0, 0)
    h = pl.pallas_call(
        _shared_up_body,
        grid=(M // tm,),
        in_specs=[pl.BlockSpec((tm, D), lambda i: (i, 0)),
                  pl.BlockSpec((1, D, FF), wsel), pl.BlockSpec((1, D, FF), wsel)],
        out_specs=pl.BlockSpec((tm, FF), lambda i: (i, 0)),
        out_shape=jax.ShapeDtypeStruct((M, FF), BF16),
        compiler_params=_params("parallel"),
        name="shared_up",
    )(xb, w_gate, w_up)
    return pl.pallas_call(
        _shared_down_body,
        grid=(M // tm,),
        in_specs=[pl.BlockSpec((tm, FF), lambda i: (i, 0)), pl.BlockSpec((1, FF, D), wsel)],
        out_specs=pl.BlockSpec((tm, D), lambda i: (i, 0)),
        out_shape=jax.ShapeDtypeStruct((M, D), F32),
        compiler_params=_params("parallel"),
        name="shared_down",
    )(h, w_down)


def _expert_changed(be_ref, i):
    return (i == 0) | (be_ref[i] != be_ref[jnp.maximum(i - 1, 0)])


GATHER_SLOTS = 3
DMA_QUEUES = 2


def _stream_expert_weights(be_ref, nx_ref, i, w_hbm, stage, wsem, w_bf, layer):
    def copy(e, j):
        return pltpu.make_async_copy(w_hbm[j].at[layer, e], stage.at[j], wsem.at[j])

    @pl.when(i == 0)
    def _():
        for j in range(len(w_hbm)):
            copy(be_ref[0], j).start()

    @pl.when(_expert_changed(be_ref, i))
    def _():
        e, nxt = be_ref[i], nx_ref[i]
        for j in range(len(w_hbm)):
            copy(e, j).wait()
            w_bf[j][...] = stage[j].astype(BF16)

            @pl.when(nxt != e)
            def _():
                copy(nxt, j).start()


def _moe_up_body(be_ref, nx_ref, na_ref, tok_ref, x_hbm, wg_hbm, wu_hbm, h_ref, xbuf, sem, x_lhs, stage, wsem,
                 wg_bf, wu_bf, *, tm, layer):
    i = pl.program_id(0)
    na = na_ref[0]
    last_blk = pl.num_programs(0) - 1
    half = wg_bf.shape[0] // 2

    def issue(blk, slot):
        base = jnp.minimum(blk, last_blk) * tm
        for r in range(tm):
            src = pl.multiple_of(tok_ref[base + r], TOK_ROWS)
            pltpu.make_async_copy(x_hbm.at[pl.ds(src, TOK_ROWS)],
                                  xbuf.at[slot, pl.ds(r * TOK_ROWS, TOK_ROWS)],
                                  sem.at[slot]).start(priority=r % DMA_QUEUES)

    def wait(slot):
        pltpu.make_async_copy(x_hbm.at[pl.ds(0, tm * TOK_ROWS)], xbuf.at[slot], sem.at[slot]).wait()

    @pl.when(i == 0)
    def _():
        issue(0, 0)
        issue(1, 1)

    @pl.when(i >= na)
    def _():
        h_ref[...] = jnp.zeros_like(h_ref)

    @pl.when(i < na)
    def _():
        slot = i % GATHER_SLOTS
        _stream_expert_weights(be_ref, nx_ref, i, (wg_hbm, wu_hbm), stage, wsem, (wg_bf, wu_bf), layer)
        wait(slot)
        for s in range(TOK_ROWS):
            lo, hi = _unpack_pair(_load_token_rows(xbuf.at[slot], s, tm))
            x_lhs[:, s * LANES:(s + 1) * LANES] = lo.astype(BF16)
            x_lhs[:, half + s * LANES:half + (s + 1) * LANES] = hi.astype(BF16)
        issue(i + 2, (i + 2) % GATHER_SLOTS)
        x = x_lhs[...]
        g = jnp.dot(x, wg_bf[...], preferred_element_type=F32)
        u = jnp.dot(x, wu_bf[...], preferred_element_type=F32)
        h_ref[...] = (_silu(g) * u).astype(h_ref.dtype)

        @pl.when(i + 1 == na)
        def _():
            wait((i + 1) % GATHER_SLOTS)
            wait((i + 2) % GATHER_SLOTS)


def _moe_down_body(be_ref, nx_ref, na_ref, dst_ref, h_ref, wd_hbm, y_hbm, ybuf, sem, stage, wsem, wd_bf, *, tm,
                   layer):
    i = pl.program_id(0)
    na = na_ref[0]

    def wait(slot):
        pltpu.make_async_copy(ybuf.at[slot], y_hbm.at[pl.ds(0, tm * TOK_ROWS)], sem.at[slot]).wait()

    def scatter(blk, slot):
        base = blk * tm
        for r in range(tm):
            dst = pl.multiple_of(dst_ref[base + r], TOK_ROWS)
            pltpu.make_async_copy(ybuf.at[slot, pl.ds(r * TOK_ROWS, TOK_ROWS)],
                                  y_hbm.at[pl.ds(dst, TOK_ROWS)], sem.at[slot]).start(priority=r % DMA_QUEUES)

    def compute(slot):
        y = jnp.dot(h_ref[...], wd_bf[...], preferred_element_type=F32)
        _store_token_rows(ybuf.at[slot], y, tm)

    @pl.when(i == 0)
    def _():
        plane = y_hbm.shape[0] // TOP_K
        spare = 2 * tm // TOP_K * TOK_ROWS
        ybuf[0] = jnp.zeros(ybuf.shape[1:], ybuf.dtype)
        for k in range(TOP_K):
            fill = pltpu.make_async_copy(ybuf.at[0, pl.ds(0, spare)],
                                         y_hbm.at[pl.ds((k + 1) * plane - spare, spare)], sem.at[0])
            fill.start()
            fill.wait()

    @pl.when(i < na)
    def _():
        _stream_expert_weights(be_ref, nx_ref, i, (wd_hbm,), stage, wsem, (wd_bf,), layer)

        @pl.when(i == 0)
        def _():
            compute(0)

        @pl.when(i > 0)
        def _():
            slot = i % 2

            @pl.when(i >= 2)
            def _():
                wait(slot)

            scatter(i - 1, 1 - slot)
            compute(slot)

    @pl.when(i == pl.num_programs(0) - 1)
    def _():
        last = na - 1

        @pl.when(na >= 2)
        def _():
            wait(na % 2)

        scatter(last, last % 2)
        wait(last % 2)


def routed_ffn(x_packed, w_gate, w_up, w_down, layer, plan, tm):
    slot_token, slot_dest, block_e, next_e, n_active = plan
    T = x_packed.shape[0] // TOK_ROWS
    NB = block_e.shape[0]
    D, FF = w_gate.shape[-2], w_gate.shape[-1]
    t_pad = T + 2 * tm // TOP_K
    buf = lambda n: pltpu.VMEM((n, tm * TOK_ROWS, LANES), jnp.uint32)
    row = lambda i, be, nx, na, ix: (jnp.minimum(i, na[0] - 1), 0)
    hbm = pl.BlockSpec(memory_space=pl.ANY)
    params = pltpu.CompilerParams(dimension_semantics=("arbitrary",), vmem_limit_bytes=V7X_VMEM_LIMIT)
    h = pl.pallas_call(
        functools.partial(_moe_up_body, tm=tm, layer=layer),
        grid_spec=pltpu.PrefetchScalarGridSpec(
            num_scalar_prefetch=4, grid=(NB,),
            in_specs=[hbm, hbm, hbm],
            out_specs=pl.BlockSpec((tm, FF), lambda i, be, nx, na, ix: (i, 0)),
            scratch_shapes=[buf(GATHER_SLOTS), pltpu.SemaphoreType.DMA((GATHER_SLOTS,)),
                            pltpu.VMEM((tm, D), BF16),
                            pltpu.VMEM((2, D, FF), F32), pltpu.SemaphoreType.DMA((2,)),
                            pltpu.VMEM((D, FF), BF16), pltpu.VMEM((D, FF), BF16)]),
        out_shape=jax.ShapeDtypeStruct((NB * tm, FF), BF16),
        compiler_params=params,
        name="moe_up",
    )(block_e, next_e, n_active, slot_token * TOK_ROWS, x_packed, w_gate, w_up)
    y = pl.pallas_call(
        functools.partial(_moe_down_body, tm=tm, layer=layer),
        grid_spec=pltpu.PrefetchScalarGridSpec(
            num_scalar_prefetch=4, grid=(NB,),
            in_specs=[pl.BlockSpec((tm, FF), row), hbm],
            out_specs=hbm,
            scratch_shapes=[buf(2), pltpu.SemaphoreType.DMA((2,)),
                            pltpu.VMEM((1, FF, D), F32), pltpu.SemaphoreType.DMA((1,)),
                            pltpu.VMEM((FF, D), BF16)]),
        out_shape=jax.ShapeDtypeStruct((TOP_K * t_pad * TOK_ROWS, LANES), jnp.uint32),
        compiler_params=params,
        name="moe_down",
    )(block_e, next_e, n_active, slot_dest * TOK_ROWS, h, w_down)
    return y.reshape(TOP_K, t_pad * TOK_ROWS, LANES)


SLOT_MAP_CHUNK = 2048
SLOT_MAP_UNROLL = 32


def _slot_map_body(dest_ref, src_ref, *, clear_chunk):
    phase, i = pl.program_id(0), pl.program_id(1)

    @pl.when(phase == 0)
    def _():
        def clear(j, carry):
            src_ref[i * clear_chunk + j] = -1
            return carry

        lax.fori_loop(0, clear_chunk, clear, 0, unroll=SLOT_MAP_UNROLL)

    @pl.when(phase == 1)
    def _():
        def place(a, carry):
            src_ref[dest_ref[a]] = i * SLOT_MAP_CHUNK + a
            return carry

        lax.fori_loop(0, SLOT_MAP_CHUNK, place, 0, unroll=SLOT_MAP_UNROLL)


def _slot_map(dest, n_slots):
    A = dest.shape[0]
    n_chunks = -(-A // SLOT_MAP_CHUNK)
    clear_chunk = -(-(n_slots + 1) // (SLOT_MAP_UNROLL * n_chunks)) * SLOT_MAP_UNROLL
    dest = jnp.concatenate([dest, jnp.full((n_chunks * SLOT_MAP_CHUNK - A,), n_slots, jnp.int32)])
    src = pl.pallas_call(
        functools.partial(_slot_map_body, clear_chunk=clear_chunk),
        grid=(2, n_chunks),
        in_specs=[pl.BlockSpec((SLOT_MAP_CHUNK,), lambda ph, i: (i * ph,), memory_space=pltpu.SMEM)],
        out_specs=pl.BlockSpec(memory_space=pltpu.SMEM),
        out_shape=jax.ShapeDtypeStruct((n_chunks * clear_chunk,), jnp.int32),
        compiler_params=pltpu.CompilerParams(dimension_semantics=("arbitrary", "arbitrary")),
        name="slot_map",
    )(dest)
    return src[:n_slots]


def _dispatch_plan(idx, tm):
    T = idx.shape[0]
    A = T * TOP_K
    NB = -(-A // tm) + N_EXPERTS
    spare_t = 2 * tm // TOP_K
    t_pad = T + spare_t
    idx = idx.astype(jnp.int32)
    onehot = idx[:, :, None] == jnp.arange(N_EXPERTS, dtype=jnp.int32)[None, None, :]
    sel = onehot.any(axis=1).astype(jnp.int32)
    incl = jnp.cumsum(sel, axis=0)
    counts = incl[-1]
    padded = (counts + tm - 1) // tm * tm
    pad_end = jnp.cumsum(padded)
    pad_start = pad_end - padded
    n_active = (pad_end[-1] // tm).astype(jnp.int32)
    blk = jnp.arange(NB, dtype=jnp.int32)
    block_e = jnp.minimum(jnp.searchsorted(pad_end, jnp.minimum(blk, n_active - 1) * tm, side='right'),
                          N_EXPERTS - 1).astype(jnp.int32)
    e_ids = jnp.arange(N_EXPERTS, dtype=jnp.int32)
    later_used = lax.cummin(jnp.where(counts > 0, e_ids, N_EXPERTS), reverse=True)
    next_used = jnp.concatenate([later_used[1:], jnp.full((1,), N_EXPERTS, jnp.int32)])
    next_e = jnp.where(next_used[block_e] < N_EXPERTS, next_used[block_e], block_e).astype(jnp.int32)
    slot_of = (incl - sel + pad_start[None, :])[:, None, :]
    dest = jnp.sum(jnp.where(onehot, slot_of, 0), axis=-1).reshape(A)
    slot_src = _slot_map(dest, NB * tm)
    slot = jnp.arange(NB * tm, dtype=jnp.int32)
    valid = slot_src >= 0
    slot_token = jnp.where(valid, slot_src // TOP_K, 0)
    s2 = (slot // tm % 2) * tm + slot % tm
    spare = (s2 // spare_t) * t_pad + T + s2 % spare_t
    slot_dest = jnp.where(valid, (slot_src % TOP_K) * t_pad + slot_src // TOP_K, spare)
    return slot_token, slot_dest, block_e, next_e, n_active.reshape(1)


def moe_block(x_parts, xb_parts, xp_parts, p, layer):
    w_t = jnp.swapaxes(p['router_w'], 1, 2)
    bias_col = p['router_bias'][:, :, None]
    routed = [route(x, w_t, bias_col, layer) for x in x_parts]
    idx = jnp.concatenate([r[0] for r in routed], axis=1).T
    wts = jnp.concatenate([r[1] for r in routed], axis=1).T
    plan = _dispatch_plan(idx, MOE_TM)
    y = routed_ffn(jnp.concatenate(xp_parts, axis=0), p['exp_w_gate'], p['exp_w_up'], p['exp_w_down'],
                   layer, plan, MOE_TM)
    shared = [shared_ffn(xb, p['shared_w_gate'], p['shared_w_up'], p['shared_w_down'], layer)
              for xb in xb_parts]
    return y, wts, shared


def _rope_tables(pos, half):
    inv_freq = ROPE_THETA ** (-jnp.arange(half, dtype=F32) / half)
    ang = pos.astype(F32)[:, None] * inv_freq[None, :]
    return jnp.cos(ang), jnp.sin(ang)


def _even_prompt(x_bf, B, L, p, i):
    proj = matmul(x_bf, p['w_in_even'][i].astype(BF16), name="in_proj_even")
    proj3 = proj.reshape(B, L, -1)
    cos, sin = _rope_tables(jnp.arange(L, dtype=jnp.int32), RET_HEAD_DIM // 2)
    conv_out, conv_new = conv_prompt(proj3, p['conv_w'][i], p['conv_b'][i], p['conv_ln_g'][i], p['conv_ln_b'][i])
    ret_out, s_new = retention_prompt(proj3, cos, sin, p['ret_gn_g'][i], p['ret_gn_b'][i])
    mixed = jnp.concatenate([conv_out, ret_out], axis=-1).reshape(B * L, -1)
    return matmul(mixed, p['w_out_even'][i].astype(BF16), name="out_proj_even"), conv_new, s_new


def _even_sample(x_bf, conv_buf, ret_state, p, i):
    B = x_bf.shape[0]
    H, d = RET_HEADS, RET_HEAD_DIM
    proj = matmul(x_bf, p['w_in_even'][i].astype(BF16), name="in_proj_even")
    conv_out, conv_new = conv_sample(proj, conv_buf, p['conv_w'][i], p['conv_b'][i],
                                     p['conv_ln_g'][i], p['conv_ln_b'][i])
    o2 = 2 * CONV_CH
    q_col = proj[:, o2:o2 + RET_WIDTH].reshape(B, H, d, 1)
    k_col = proj[:, o2 + RET_WIDTH:o2 + 2 * RET_WIDTH].reshape(B, H, d, 1)
    v_row = proj[:, o2 + 2 * RET_WIDTH:o2 + 3 * RET_WIDTH].reshape(B, 1, RET_WIDTH)
    g_row = proj[:, o2 + 3 * RET_WIDTH:].reshape(B, 1, RET_WIDTH)
    cos, sin = _rope_tables(jnp.full((1,), PAST_LEN, jnp.int32), d // 2)
    ret_out, s_new = retention_sample(q_col, k_col, v_row, g_row, ret_state, cos.reshape(d // 2, 1),
                                      sin.reshape(d // 2, 1), p['ret_gn_g'][i], p['ret_gn_b'][i])
    mixed = jnp.concatenate([conv_out, ret_out.reshape(B, RET_WIDTH)], axis=-1)
    return matmul(mixed, p['w_out_even'][i].astype(BF16), name="out_proj_even"), conv_new, s_new


def _att_rope_tables(pos, rows):
    cos, sin = _rope_tables(pos, ATT_HEAD_DIM // 2)
    cos = jnp.tile(cos, (rows // cos.shape[0], LANES // cos.shape[1]))
    sin = jnp.tile(sin, (rows // sin.shape[0], LANES // sin.shape[1]))
    return cos, sin


def _odd_prompt(x_bf, B, L, p, i):
    cos, sin = _att_rope_tables(jnp.arange(L, dtype=jnp.int32), L)
    qkv = matmul(x_bf, p['w_qkv_odd'][i].astype(BF16), bias=p['b_qkv_odd'][i],
                 rope=(cos, sin, Q_WIDTH + KV_WIDTH), name="qkv_proj_odd")
    qkv3 = qkv.reshape(B, L, -1)
    o = swa_prompt(qkv3, p['sinks'][i])
    out = matmul(o.reshape(B * L, Q_WIDTH), p['w_out_odd'][i].astype(BF16), bias=p['b_out_odd'][i],
                 name="out_proj_odd")
    k_new = qkv3[:, L - WINDOW:, Q_WIDTH:Q_WIDTH + KV_WIDTH].reshape(B, WINDOW, N_KV_HEADS, ATT_HEAD_DIM)
    v_new = qkv3[:, L - WINDOW:, Q_WIDTH + KV_WIDTH:].reshape(B, WINDOW, N_KV_HEADS, ATT_HEAD_DIM)
    return out, k_new, v_new


def _odd_sample(x_bf, k_buf, v_buf, p, i):
    B = x_bf.shape[0]
    cos, sin = _att_rope_tables(jnp.full((1,), PAST_LEN, jnp.int32), B)
    qkv = matmul(x_bf, p['w_qkv_odd'][i].astype(BF16), bias=p['b_qkv_odd'][i],
                 rope=(cos, sin, Q_WIDTH + KV_WIDTH), name="qkv_proj_odd")
    q3 = qkv[:, :Q_WIDTH].reshape(B, N_HEADS, ATT_HEAD_DIM)
    kn3 = qkv[:, Q_WIDTH:Q_WIDTH + KV_WIDTH].reshape(B, N_KV_HEADS, ATT_HEAD_DIM)
    vn3 = qkv[:, Q_WIDTH + KV_WIDTH:].reshape(B, N_KV_HEADS, ATT_HEAD_DIM)
    o, k_new, v_new = swa_sample(q3, kn3, vn3, k_buf.reshape(B, WINDOW, KV_WIDTH),
                                 v_buf.reshape(B, WINDOW, KV_WIDTH), p['sinks'][i])
    out = matmul(o.reshape(B, Q_WIDTH), p['w_out_odd'][i].astype(BF16), bias=p['b_out_odd'][i],
                 name="out_proj_odd")
    shape = (B, WINDOW, N_KV_HEADS, ATT_HEAD_DIM)
    return out, k_new.reshape(shape), v_new.reshape(shape)


def kernel(x_prompt, x_sample, state_conv, state_ret, state_win_k, state_win_v,
           w_in_even, conv_w, conv_b, conv_ln_g, conv_ln_b, ret_gn_g, ret_gn_b, w_out_even,
           w_qkv_odd, b_qkv_odd, sinks, w_out_odd, b_out_odd, ln_g, ln_b,
           router_w, router_bias, exp_w_gate, exp_w_up, exp_w_down,
           shared_w_gate, shared_w_up, shared_w_down):
    p = dict(w_in_even=w_in_even, conv_w=conv_w, conv_b=conv_b, conv_ln_g=conv_ln_g, conv_ln_b=conv_ln_b,
             ret_gn_g=ret_gn_g, ret_gn_b=ret_gn_b, w_out_even=w_out_even,
             w_qkv_odd=w_qkv_odd, b_qkv_odd=b_qkv_odd, sinks=sinks, w_out_odd=w_out_odd, b_out_odd=b_out_odd,
             ln_g=ln_g, ln_b=ln_b, router_w=router_w, router_bias=router_bias,
             exp_w_gate=exp_w_gate, exp_w_up=exp_w_up, exp_w_down=exp_w_down,
             shared_w_gate=shared_w_gate, shared_w_up=shared_w_up, shared_w_down=shared_w_down)
    B, L, D = x_prompt.shape
    Bs = x_sample.shape[0]
    Tp = B * L
    xp = x_prompt.reshape(Tp, D)
    xs = x_sample.reshape(Bs, D)
    xp_bf, xs_bf = xp.astype(BF16), xs.astype(BF16)
    conv_p, conv_s, ret_p, ret_s, wk_p, wk_s, wv_p, wv_s = [], [], [], [], [], [], [], []
    for layer in range(DEPTH):
        i = layer // 2
        if layer % 2 == 0:
            hp, c_new, r_new = _even_prompt(xp_bf, B, L, p, i)
            conv_p.append(c_new)
            ret_p.append(r_new)
            hs, c_new, r_new = _even_sample(xs_bf, state_conv[i], state_ret[i], p, i)
            conv_s.append(c_new)
            ret_s.append(r_new)
        else:
            hp, k_new, v_new = _odd_prompt(xp_bf, B, L, p, i)
            wk_p.append(k_new)
            wv_p.append(v_new)
            hs, k_new, v_new = _odd_sample(xs_bf, state_win_k[i], state_win_v[i], p, i)
            wk_s.append(k_new)
            wv_s.append(v_new)
        xp, xp_bf, xp_pk = deepnorm(xp, [(hp, 0)], ln_g[layer, 0], ln_b[layer, 0], name="ln_mixer_prompt")
        xs, xs_bf, xs_pk = deepnorm(xs, [(hs, 0)], ln_g[layer, 0], ln_b[layer, 0], name="ln_mixer_sample")
        y, wts, shared = moe_block([xp, xs], [xp_bf, xs_bf], [xp_pk, xs_pk], p, layer)
        xp, xp_bf, _ = deepnorm(xp, [(shared[0], 0)], ln_g[layer, 1], ln_b[layer, 1], combine=(y, wts, 0),
                                tm=64, name="ln_moe_prompt")
        xs, xs_bf, _ = deepnorm(xs, [(shared[1], 0)], ln_g[layer, 1], ln_b[layer, 1], combine=(y, wts, Tp),
                                tm=32, name="ln_moe_sample")
    return (xp.reshape(B, L, D), xs.reshape(Bs, 1, D),
            jnp.stack(conv_p), jnp.stack(conv_s), jnp.stack(ret_p), jnp.stack(ret_s),
            jnp.stack(wk_p), jnp.stack(wk_s), jnp.stack(wv_p), jnp.stack(wv_s))
```

```python
import functools

import jax
import jax.numpy as jnp
import numpy as np
from jax import lax
from jax.experimental import pallas as pl
from jax.experimental.pallas import tpu as pltpu

F32 = jnp.float32
BF16 = jnp.bfloat16

D_MODEL = 4096
DEPTH = 2
PAST_LEN = 16384
CONV_CH = D_MODEL // 2
CONV_WIDTH = 31
CONV_HALO = 32
RET_HEADS = 8
RET_HEAD_DIM = 256
RET_WIDTH = RET_HEADS * RET_HEAD_DIM
RET_CHUNK = 128
RET_STEP_ROWS = 512
ATT_HEAD_DIM = 64
N_HEADS = 64
N_KV_HEADS = 8
GROUP = N_HEADS // N_KV_HEADS
WINDOW = 128
Q_WIDTH = N_HEADS * ATT_HEAD_DIM
KV_WIDTH = N_KV_HEADS * ATT_HEAD_DIM
ROPE_THETA = 10000.0
N_EXPERTS = 128
TOP_K = 8
N_GROUPS = 8
TOPK_GROUPS = 4
EXPERTS_PER_GROUP = N_EXPERTS // N_GROUPS
EXPERT_FF = 512
ROUTED_SCALE = 2.5
DN_ALPHA = (2.0 * DEPTH) ** 0.25
LN_EPS = 1e-5

LANES = 128
V7X_VMEM_LIMIT = 52 * 1024 * 1024
MOE_TM = 256


def _params(*sem):
    return pltpu.CompilerParams(dimension_semantics=sem, vmem_limit_bytes=V7X_VMEM_LIMIT)


def _silu(x):
    return x * jax.nn.sigmoid(x)


def _rot_half_64(x):
    n = x.shape[-1]
    lane = lax.broadcasted_iota(jnp.int32, x.shape, x.ndim - 1)
    fwd = pltpu.roll(x, n - ATT_HEAD_DIM // 2, x.ndim - 1)
    bwd = pltpu.roll(x, ATT_HEAD_DIM // 2, x.ndim - 1)
    return jnp.where(lane % ATT_HEAD_DIM < ATT_HEAD_DIM // 2, -fwd, bwd)


def _mm_body(*refs, has_bias, rope_tiles):
    x_ref, w_ref = refs[0], refs[1]
    pos = 2
    b_ref = None
    if has_bias:
        b_ref = refs[pos]
        pos += 1
    if rope_tiles:
        cos_ref, sin_ref = refs[pos], refs[pos + 1]
        pos += 2
    o_ref = refs[pos]
    acc = jnp.dot(x_ref[...].astype(BF16), w_ref[...].astype(BF16), preferred_element_type=F32)
    if has_bias:
        acc = acc + b_ref[...]
    if rope_tiles:
        reps = acc.shape[1] // LANES
        cos = jnp.concatenate([cos_ref[...]] * reps, axis=1)
        sin = jnp.concatenate([sin_ref[...]] * reps, axis=1)
        @pl.when(pl.program_id(1) < rope_tiles)
        def _():
            o_ref[...] = (acc * cos + _rot_half_64(acc) * sin).astype(o_ref.dtype)

        @pl.when(pl.program_id(1) >= rope_tiles)
        def _():
            o_ref[...] = acc.astype(o_ref.dtype)
    else:
        o_ref[...] = acc.astype(o_ref.dtype)


def matmul(x, w, bias=None, rope=None, out_dtype=F32, tm=1024, tn=512, name="matmul"):
    M, K = x.shape
    N = w.shape[1]
    tm = min(tm, M)
    tn = min(tn, N)
    assert M % tm == 0 and N % tn == 0
    in_specs = [pl.BlockSpec((tm, K), lambda i, j: (i, 0)),
                pl.BlockSpec((K, tn), lambda i, j: (0, j))]
    args = [x, w]
    if bias is not None:
        in_specs.append(pl.BlockSpec((1, tn), lambda i, j: (0, j)))
        args.append(bias.reshape(1, N))
    rope_tiles = 0
    if rope is not None:
        cos, sin, n_cols = rope
        assert n_cols % tn == 0 and cos.shape[0] % tm == 0
        rope_tiles = n_cols // tn
        nrb = cos.shape[0] // tm
        in_specs += [pl.BlockSpec((tm, LANES), lambda i, j: (i % nrb, 0))] * 2
        args += [cos, sin]
    return pl.pallas_call(
        functools.partial(_mm_body, has_bias=bias is not None, rope_tiles=rope_tiles),
        grid=(M // tm, N // tn),
        in_specs=in_specs,
        out_specs=pl.BlockSpec((tm, tn), lambda i, j: (i, j)),
        out_shape=jax.ShapeDtypeStruct((M, N), out_dtype),
        compiler_params=_params("parallel", "arbitrary"),
        name=name,
    )(*args)


def _split_bf16(x):
    hi = x.astype(BF16)
    lo = (x - hi.astype(F32)).astype(BF16)
    return hi, lo


def _first_argmax(x, row, n):
    m = jnp.max(x, axis=0, keepdims=True)
    i = jnp.min(jnp.where(x == m, row, n), axis=0, keepdims=True)
    return m, i


def _router_body(x_ref, wt_ref, bias_ref, idx_ref, wts_ref):
    nt = (((1,), (1,)), ((), ()))
    xh, xl = _split_bf16(x_ref[...])
    wh, wl = _split_bf16(wt_ref[0])
    logits = lax.dot_general(wh, xh, nt, preferred_element_type=F32)
    logits = logits + lax.dot_general(wh, xl, nt, preferred_element_type=F32)
    logits = logits + lax.dot_general(wl, xh, nt, preferred_element_type=F32)
    scores = jax.nn.sigmoid(logits)
    biased = scores + bias_ref[0]
    tm = scores.shape[1]
    neg = -jnp.inf
    row_g = lax.broadcasted_iota(jnp.int32, (EXPERTS_PER_GROUP, tm), 0)
    grp = []
    for g in range(N_GROUPS):
        xg = biased[g * EXPERTS_PER_GROUP:(g + 1) * EXPERTS_PER_GROUP]
        m1, i1 = _first_argmax(xg, row_g, EXPERTS_PER_GROUP)
        m2 = jnp.max(jnp.where(row_g == i1, neg, xg), axis=0, keepdims=True)
        grp.append(m1 + m2)
    gs = jnp.concatenate(grp, axis=0)
    row_n = lax.broadcasted_iota(jnp.int32, (N_GROUPS, tm), 0)
    chosen = jnp.zeros((N_GROUPS, tm), F32)
    for _ in range(TOPK_GROUPS):
        _, i = _first_argmax(gs, row_n, N_GROUPS)
        hit = row_n == i
        chosen = jnp.where(hit, 1.0, chosen)
        gs = jnp.where(hit, neg, gs)
    emask = jnp.concatenate([jnp.broadcast_to(chosen[g:g + 1], (EXPERTS_PER_GROUP, tm))
                             for g in range(N_GROUPS)], axis=0)
    cand = jnp.where(emask > 0.0, biased, neg)
    row_e = lax.broadcasted_iota(jnp.int32, (N_EXPERTS, tm), 0)
    ids, ws = [], []
    for _ in range(TOP_K):
        _, i = _first_argmax(cand, row_e, N_EXPERTS)
        hit = row_e == i
        ids.append(i)
        ws.append(jnp.sum(jnp.where(hit, scores, 0.0), axis=0, keepdims=True))
        cand = jnp.where(hit, neg, cand)
    w = jnp.concatenate(ws, axis=0)
    idx_ref[...] = jnp.concatenate(ids, axis=0)
    wts_ref[...] = w / jnp.sum(w, axis=0, keepdims=True) * ROUTED_SCALE


def route(x, w_t, bias_col, layer, tm=512):
    M, K = x.shape
    E = w_t.shape[1]
    tm = min(tm, M)
    return pl.pallas_call(
        _router_body,
        grid=(M // tm,),
        in_specs=[pl.BlockSpec((tm, K), lambda i: (i, 0)),
                  pl.BlockSpec((1, E, K), lambda i: (layer, 0, 0)),
                  pl.BlockSpec((1, E, 1), lambda i: (layer, 0, 0))],
        out_specs=[pl.BlockSpec((TOP_K, tm), lambda i: (0, i))] * 2,
        out_shape=[jax.ShapeDtypeStruct((TOP_K, M), jnp.int32), jax.ShapeDtypeStruct((TOP_K, M), F32)],
        compiler_params=_params("parallel"),
        name="router",
    )(x, w_t, bias_col)


def _layer_norm_rows(z, g, b):
    mu = jnp.mean(z, axis=-1, keepdims=True)
    zc = z - mu
    var = jnp.mean(zc * zc, axis=-1, keepdims=True)
    return zc * lax.rsqrt(var + LN_EPS) * g + b


def _unpack_pair(w):
    lo = lax.bitcast_convert_type(w << 16, F32)
    hi = lax.bitcast_convert_type(w & jnp.uint32(0xFFFF0000), F32)
    return lo, hi


def _pack_pair(lo, hi):
    lo_bits = lax.bitcast_convert_type(lo.astype(BF16).astype(F32), jnp.uint32) >> 16
    hi_bits = lax.bitcast_convert_type(hi.astype(BF16).astype(F32), jnp.uint32) & jnp.uint32(0xFFFF0000)
    return hi_bits | lo_bits


TOK_ROWS = D_MODEL // 2 // LANES


def _load_token_rows(ref, s, tm):
    return ref[pl.ds(s, tm, stride=TOK_ROWS), :]


def _store_token_rows(ref, y, tm):
    half = y.shape[1] // 2
    for s in range(TOK_ROWS):
        words = _pack_pair(y[:, s * LANES:(s + 1) * LANES], y[:, half + s * LANES:half + (s + 1) * LANES])
        ref[pl.ds(s, tm, stride=TOK_ROWS), :] = words


def _ln_body(*refs, n_h, combine):
    x_ref = refs[0]
    h_refs = refs[1:1 + n_h]
    pos = 1 + n_h
    if combine:
        y_ref, w_ref = refs[pos], refs[pos + 1]
        pos += 2
    g_ref, b_ref, o_ref, ob_ref, op_ref = refs[pos:]
    tm, d = x_ref.shape
    half = d // 2
    z = DN_ALPHA * x_ref[...]
    for h_ref in h_refs:
        z = z + h_ref[...]
    if combine:
        wk = [jnp.broadcast_to(w_ref[:, k:k + 1], (tm, LANES)) for k in range(TOP_K)]
        cols_lo, cols_hi = [], []
        for s in range(TOK_ROWS):
            acc_lo = z[:, s * LANES:(s + 1) * LANES]
            acc_hi = z[:, half + s * LANES:half + (s + 1) * LANES]
            for k in range(TOP_K):
                lo, hi = _unpack_pair(_load_token_rows(y_ref.at[k], s, tm))
                acc_lo = acc_lo + wk[k] * lo
                acc_hi = acc_hi + wk[k] * hi
            cols_lo.append(acc_lo)
            cols_hi.append(acc_hi)
        z = jnp.concatenate(cols_lo + cols_hi, axis=1)
    y = _layer_norm_rows(z, g_ref[...], b_ref[...])
    o_ref[...] = y
    ob_ref[...] = y.astype(BF16)
    _store_token_rows(op_ref, y, tm)


def deepnorm(x, hs, g, b, combine=None, tm=256, name="deepnorm"):
    M, D = x.shape
    tm = min(tm, M)
    assert M % tm == 0
    row_spec = lambda width, off: pl.BlockSpec((tm, width), functools.partial(lambda i, o: (i + o, 0), o=off // tm))
    in_specs = [row_spec(D, 0)]
    args = [x]
    for h, off in hs:
        assert off % tm == 0
        in_specs.append(row_spec(D, off))
        args.append(h)
    if combine is not None:
        y, wts, off = combine
        assert off % tm == 0
        y_spec = pl.BlockSpec((TOP_K, tm * TOK_ROWS, LANES),
                              functools.partial(lambda i, o: (0, i + o, 0), o=off // tm))
        in_specs += [y_spec, row_spec(TOP_K, off)]
        args += [y, wts]
    in_specs += [pl.BlockSpec((1, D), lambda i: (0, 0))] * 2
    args += [g.reshape(1, D), b.reshape(1, D)]
    return pl.pallas_call(
        functools.partial(_ln_body, n_h=len(hs), combine=combine is not None),
        grid=(M // tm,),
        in_specs=in_specs,
        out_specs=[row_spec(D, 0), row_spec(D, 0), pl.BlockSpec((tm * TOK_ROWS, LANES), lambda i: (i, 0))],
        out_shape=[jax.ShapeDtypeStruct((M, D), F32), jax.ShapeDtypeStruct((M, D), BF16),
                   jax.ShapeDtypeStruct((M * TOK_ROWS, LANES), jnp.uint32)],
        compiler_params=_params("parallel"),
        name=name,
    )(*args)


def _conv_prompt_body(a_ref, g_ref, pa_ref, pg_ref, w_ref, bdw_ref, lng_ref, lnb_ref,
                      o_ref, st_ref, full_ref, y_ref, *, tl):
    i = pl.program_id(1)
    u = a_ref[0] * jax.nn.sigmoid(g_ref[0])
    up = pa_ref[0] * jax.nn.sigmoid(pg_ref[0])
    up = jnp.where(i > 0, up, 0.0)
    full_ref[0:CONV_HALO, :] = up
    full_ref[CONV_HALO:CONV_HALO + tl, :] = u
    base = CONV_HALO - (CONV_WIDTH - 1)
    n_col = u.shape[1] // LANES

    def col_step(c, carry):
        off = pl.multiple_of(c * LANES, LANES)
        acc = jnp.zeros((tl, LANES), F32)
        for j in range(CONV_WIDTH):
            acc = acc + w_ref[j:j + 1, pl.ds(off, LANES)] * full_ref[base + j:base + j + tl, pl.ds(off, LANES)]
        y_ref[:, pl.ds(off, LANES)] = acc
        return carry

    lax.fori_loop(0, n_col, col_step, 0)
    y = _layer_norm_rows(y_ref[...] + bdw_ref[...], lng_ref[...], lnb_ref[...])
    o_ref[0] = _silu(y).astype(o_ref.dtype)

    @pl.when(i == pl.num_programs(1) - 1)
    def _():
        st_ref[0] = full_ref[CONV_HALO + tl - (CONV_WIDTH - 1):CONV_HALO + tl, :]


def conv_prompt(proj3, w_dw, b_dw, ln_g, ln_b, tl=128):
    B, L, _ = proj3.shape
    C = CONV_CH
    nh = tl // CONV_HALO
    vec = lambda v: v.reshape(1, C)
    return pl.pallas_call(
        functools.partial(_conv_prompt_body, tl=tl),
        grid=(B, L // tl),
        in_specs=[
            pl.BlockSpec((1, tl, C), lambda b, i: (b, i, 0)),
            pl.BlockSpec((1, tl, C), lambda b, i: (b, i, 1)),
            pl.BlockSpec((1, CONV_HALO, C), lambda b, i: (b, jnp.maximum(i * nh - 1, 0), 0)),
            pl.BlockSpec((1, CONV_HALO, C), lambda b, i: (b, jnp.maximum(i * nh - 1, 0), 1)),
            pl.BlockSpec((CONV_WIDTH, C), lambda b, i: (0, 0)),
            pl.BlockSpec((1, C), lambda b, i: (0, 0)),
            pl.BlockSpec((1, C), lambda b, i: (0, 0)),
            pl.BlockSpec((1, C), lambda b, i: (0, 0)),
        ],
        out_specs=[
            pl.BlockSpec((1, tl, C), lambda b, i: (b, i, 0)),
            pl.BlockSpec((1, CONV_WIDTH - 1, C), lambda b, i: (b, 0, 0)),
        ],
        out_shape=[jax.ShapeDtypeStruct((B, L, C), BF16),
                   jax.ShapeDtypeStruct((B, CONV_WIDTH - 1, C), F32)],
        scratch_shapes=[pltpu.VMEM((CONV_HALO + tl, C), F32), pltpu.VMEM((tl, C), F32)],
        compiler_params=_params("parallel", "arbitrary"),
        name="conv_prompt",
    )(proj3, proj3, proj3, proj3, w_dw, vec(b_dw), vec(ln_g), vec(ln_b))


def _conv_sample_body(a_ref, g_ref, buf_ref, w_ref, bdw_ref, lng_ref, lnb_ref, o_ref, st_ref):
    u = a_ref[...] * jax.nn.sigmoid(g_ref[...])
    nb = CONV_WIDTH - 1
    acc = w_ref[nb:nb + 1, :] * u
    for j in range(nb):
        acc = acc + w_ref[j:j + 1, :] * buf_ref[:, j, :]
    y = _layer_norm_rows(acc + bdw_ref[...], lng_ref[...], lnb_ref[...])
    o_ref[...] = _silu(y).astype(o_ref.dtype)
    st_ref[:, 0:nb - 1, :] = buf_ref[:, 1:nb, :]
    st_ref[:, nb - 1, :] = u


def conv_sample(proj, buf, w_dw, b_dw, ln_g, ln_b, tb=8):
    B = proj.shape[0]
    C = CONV_CH
    nb = CONV_WIDTH - 1
    vec = lambda v: v.reshape(1, C)
    return pl.pallas_call(
        _conv_sample_body,
        grid=(B // tb,),
        in_specs=[
            pl.BlockSpec((tb, C), lambda b: (b, 0)),
            pl.BlockSpec((tb, C), lambda b: (b, 1)),
            pl.BlockSpec((tb, nb, C), lambda b: (b, 0, 0)),
            pl.BlockSpec((CONV_WIDTH, C), lambda b: (0, 0)),
            pl.BlockSpec((1, C), lambda b: (0, 0)),
            pl.BlockSpec((1, C), lambda b: (0, 0)),
            pl.BlockSpec((1, C), lambda b: (0, 0)),
        ],
        out_specs=[pl.BlockSpec((tb, C), lambda b: (b, 0)),
                   pl.BlockSpec((tb, nb, C), lambda b: (b, 0, 0))],
        out_shape=[jax.ShapeDtypeStruct((B, C), BF16), jax.ShapeDtypeStruct((B, nb, C), F32)],
        compiler_params=_params("parallel"),
        name="conv_sample",
    )(proj, proj, buf, w_dw, vec(b_dw), vec(ln_g), vec(ln_b))


def _ret_log_gamma():
    return np.log1p(-np.exp2(-5.0 - np.arange(RET_HEADS, dtype=np.float64)))


def _group_norm_gate(o, gate, gn_g, gn_b):
    mu = jnp.mean(o, axis=-1, keepdims=True)
    oc = o - mu
    var = jnp.mean(oc * oc, axis=-1, keepdims=True)
    return (oc * lax.rsqrt(var + LN_EPS) * gn_g + gn_b) * _silu(gate)


def _rope_256(x, cos, sin):
    half = RET_HEAD_DIM // 2
    x1, x2 = x[..., :half], x[..., half:]
    return jnp.concatenate([x1 * cos - x2 * sin, x2 * cos + x1 * sin], axis=-1)


def _ret_prompt_body(q_ref, k_ref, v_ref, g_ref, cos_ref, sin_ref, intra_ref, qd_ref, kd_ref, cd_ref,
                     gng_ref, gnb_ref, o_ref, s_ref):
    c = pl.program_id(2)

    @pl.when(c == 0)
    def _():
        s_ref[...] = jnp.zeros_like(s_ref)

    cw = RET_CHUNK
    for j in range(q_ref.shape[1] // cw):
        rows = slice(j * cw, (j + 1) * cw)
        cos, sin = cos_ref[rows, :], sin_ref[rows, :]
        q = _rope_256(q_ref[0, rows, :], cos, sin)
        k = _rope_256(k_ref[0, rows, :], cos, sin) * (RET_HEAD_DIM ** -0.5)
        vb = v_ref[0, rows, :].astype(BF16)
        s_prev = s_ref[0, 0]
        scores = lax.dot_general(q.astype(BF16), k.astype(BF16), (((1,), (1,)), ((), ())),
                                 preferred_element_type=F32) * intra_ref[0]
        o = jnp.dot(scores.astype(BF16), vb, preferred_element_type=F32)
        o = o + jnp.dot((q * qd_ref[0]).astype(BF16), s_prev.astype(BF16), preferred_element_type=F32)
        kdt = (k * kd_ref[0]).T.astype(BF16)
        kv = jnp.dot(kdt, vb, preferred_element_type=F32)
        s_ref[0, 0] = cd_ref[0] * s_prev + kv
        o_ref[0, rows, :] = _group_norm_gate(o, g_ref[0, rows, :], gng_ref[...], gnb_ref[...]).astype(o_ref.dtype)


def retention_prompt(proj3, cos, sin, gn_g, gn_b):
    B, L, _ = proj3.shape
    H, d, c = RET_HEADS, RET_HEAD_DIM, RET_CHUNK
    lg = _ret_log_gamma()
    idx = np.arange(c, dtype=np.float64)
    rel = idx[:, None] - idx[None, :]
    intra = np.where(rel[None] >= 0, np.exp(lg[:, None, None] * np.maximum(rel, 0.0)[None]), 0.0)
    ones = np.ones((1, 1, d))
    qd = np.exp(lg[:, None] * (idx + 1.0)[None, :])[:, :, None] * ones
    kd = np.exp(lg[:, None] * (c - 1.0 - idx)[None, :])[:, :, None] * ones
    cd = np.exp(lg * c)[:, None, None] * ones
    col0 = 2 * CONV_CH // d
    tl = min(RET_STEP_ROWS, L)
    assert L % tl == 0 and tl % c == 0
    head_spec = lambda off: pl.BlockSpec((1, tl, d), lambda b, h, n: (b, n, col0 + off * H + h))
    tab_spec = pl.BlockSpec((1, c, d), lambda b, h, n: (h, 0, 0))
    return pl.pallas_call(
        _ret_prompt_body,
        grid=(B, H, L // tl),
        in_specs=[
            head_spec(0), head_spec(1), head_spec(2), head_spec(3),
            pl.BlockSpec((tl, d // 2), lambda b, h, n: (n, 0)),
            pl.BlockSpec((tl, d // 2), lambda b, h, n: (n, 0)),
            pl.BlockSpec((1, c, c), lambda b, h, n: (h, 0, 0)),
            tab_spec, tab_spec,
            pl.BlockSpec((1, 1, d), lambda b, h, n: (h, 0, 0)),
            pl.BlockSpec((1, d), lambda b, h, n: (0, h)),
            pl.BlockSpec((1, d), lambda b, h, n: (0, h)),
        ],
        out_specs=[
            pl.BlockSpec((1, tl, d), lambda b, h, n: (b, n, h)),
            pl.BlockSpec((1, 1, d, d), lambda b, h, n: (b, h, 0, 0)),
        ],
        out_shape=[jax.ShapeDtypeStruct((B, L, H * d), BF16),
                   jax.ShapeDtypeStruct((B, H, d, d), F32)],
        compiler_params=_params("parallel", "parallel", "arbitrary"),
        name="retention_prompt",
    )(proj3, proj3, proj3, proj3, cos, sin,
      jnp.asarray(intra, F32), jnp.asarray(qd, F32), jnp.asarray(kd, F32), jnp.asarray(cd, F32),
      gn_g.reshape(1, H * d), gn_b.reshape(1, H * d))


def _ret_sample_body(qc_ref, kc_ref, v_ref, g_ref, s_ref, cosc_ref, sinc_ref, gam_ref, gng_ref, gnb_ref,
                     o_ref, so_ref):
    half = RET_HEAD_DIM // 2
    cos, sin = cosc_ref[...], sinc_ref[...]

    def rope_col(x):
        x1, x2 = x[:half], x[half:]
        return jnp.concatenate([x1 * cos - x2 * sin, x2 * cos + x1 * sin], axis=0)

    for h in range(RET_HEADS):
        lo, hi = h * RET_HEAD_DIM, (h + 1) * RET_HEAD_DIM
        q = rope_col(qc_ref[0, h])
        k = rope_col(kc_ref[0, h]) * (RET_HEAD_DIM ** -0.5)
        v = v_ref[0, :, lo:hi]
        s_new = gam_ref[h] * s_ref[0, h] + k * v
        so_ref[0, h] = s_new
        o = jnp.sum(q * s_new, axis=0, keepdims=True)
        o_ref[0, :, lo:hi] = _group_norm_gate(o, g_ref[0, :, lo:hi], gng_ref[:, lo:hi],
                                              gnb_ref[:, lo:hi]).astype(o_ref.dtype)


def retention_sample(q_col, k_col, v_row, g_row, state, cos_col, sin_col, gn_g, gn_b):
    B = state.shape[0]
    H, d = RET_HEADS, RET_HEAD_DIM
    gam = jnp.asarray(np.exp(_ret_log_gamma()), F32)
    col_spec = pl.BlockSpec((1, H, d, 1), lambda b: (b, 0, 0, 0))
    row_spec = pl.BlockSpec((1, 1, H * d), lambda b: (b, 0, 0))
    return pl.pallas_call(
        _ret_sample_body,
        grid=(B,),
        in_specs=[
            col_spec, col_spec, row_spec, row_spec,
            pl.BlockSpec((1, H, d, d), lambda b: (b, 0, 0, 0)),
            pl.BlockSpec((d // 2, 1), lambda b: (0, 0)),
            pl.BlockSpec((d // 2, 1), lambda b: (0, 0)),
            pl.BlockSpec(memory_space=pltpu.SMEM),
            pl.BlockSpec((1, H * d), lambda b: (0, 0)),
            pl.BlockSpec((1, H * d), lambda b: (0, 0)),
        ],
        out_specs=[row_spec, pl.BlockSpec((1, H, d, d), lambda b: (b, 0, 0, 0))],
        out_shape=[jax.ShapeDtypeStruct((B, 1, H * d), BF16), jax.ShapeDtypeStruct((B, H, d, d), F32)],
        compiler_params=_params("parallel"),
        name="retention_sample",
    )(q_col, k_col, v_row, g_row, state, cos_col, sin_col, gam, gn_g.reshape(1, H * d), gn_b.reshape(1, H * d))


def _pair_operand(x2, h, lane):
    col = x2[:, (h // 2) * LANES:(h // 2 + 1) * LANES]
    swapped = pltpu.roll(col, ATT_HEAD_DIM, 1)
    first, second = (col, swapped) if h % 2 == 0 else (swapped, col)
    return jnp.concatenate([jnp.where(lane < ATT_HEAD_DIM, first, 0.0),
                            jnp.where(lane >= ATT_HEAD_DIM, second, 0.0)], axis=0).astype(BF16)


def _swa_prompt_body(sink_ref, q_ref, kc_ref, kp_ref, vc_ref, vp_ref, o_ref):
    j = pl.program_id(1)
    W = WINDOW
    k2 = jnp.concatenate([kp_ref[0], kc_ref[0]], axis=0)
    v2 = jnp.concatenate([vp_ref[0], vc_ref[0]], axis=0)
    row = lax.broadcasted_iota(jnp.int32, (W, 2 * W), 0)
    col = lax.broadcasted_iota(jnp.int32, (W, 2 * W), 1)
    first_key = jnp.where(j > 0, 0, W)
    mask = (col >= jnp.maximum(row, first_key)) & (col <= W + row)
    lane_kv = lax.broadcasted_iota(jnp.int32, (2 * W, LANES), 1)
    lane_o = lax.broadcasted_iota(jnp.int32, (W, LANES), 1)
    scale = ATT_HEAD_DIM ** -0.5
    for h in range(N_KV_HEADS):
        kk = _pair_operand(k2, h, lane_kv)
        vv = _pair_operand(v2, h, lane_kv)
        for p in range(GROUP // 2):
            c0 = h * GROUP * ATT_HEAD_DIM + p * LANES
            qp = q_ref[0, :, c0:c0 + LANES].astype(BF16)
            s = lax.dot_general(qp, kk, (((1,), (1,)), ((), ())), preferred_element_type=F32) * scale
            probs, inv = [], []
            for t in range(2):
                sink = sink_ref[h * GROUP + 2 * p + t]
                st = jnp.where(mask, s[:, t * 2 * W:(t + 1) * 2 * W], -jnp.inf)
                m = jnp.maximum(jnp.max(st, axis=-1, keepdims=True), sink)
                e = jnp.exp(st - m)
                denom = jnp.sum(e, axis=-1, keepdims=True) + jnp.exp(sink - m)
                probs.append(e.astype(BF16))
                inv.append(1.0 / denom)
            o2 = jnp.dot(jnp.concatenate(probs, axis=1), vv, preferred_element_type=F32)
            o2 = o2 * jnp.where(lane_o < ATT_HEAD_DIM, inv[0], inv[1])
            o_ref[0, :, c0:c0 + LANES] = o2.astype(o_ref.dtype)


def swa_prompt(qkv3, sinks):
    B, L, _ = qkv3.shape
    W = WINDOW
    kcol = Q_WIDTH // KV_WIDTH
    return pl.pallas_call(
        _swa_prompt_body,
        grid_spec=pltpu.PrefetchScalarGridSpec(
            num_scalar_prefetch=0,
            grid=(B, L // W),
            in_specs=[
                pl.BlockSpec(memory_space=pltpu.SMEM),
                pl.BlockSpec((1, W, Q_WIDTH), lambda b, j: (b, j, 0)),
                pl.BlockSpec((1, W, KV_WIDTH), lambda b, j: (b, j, kcol)),
                pl.BlockSpec((1, W, KV_WIDTH), lambda b, j: (b, jnp.maximum(j - 1, 0), kcol)),
                pl.BlockSpec((1, W, KV_WIDTH), lambda b, j: (b, j, kcol + 1)),
                pl.BlockSpec((1, W, KV_WIDTH), lambda b, j: (b, jnp.maximum(j - 1, 0), kcol + 1)),
            ],
            out_specs=pl.BlockSpec((1, W, Q_WIDTH), lambda b, j: (b, j, 0)),
        ),
        out_shape=jax.ShapeDtypeStruct((B, L, Q_WIDTH), BF16),
        compiler_params=_params("parallel", "parallel"),
        name="swa_prompt",
    )(sinks, qkv3, qkv3, qkv3, qkv3, qkv3)


def _swa_sample_body(q_ref, kn_ref, vn_ref, knf_ref, vnf_ref, kb_ref, vb_ref, sink_ref, o_ref, ko_ref, vo_ref):
    W = WINDOW
    scale = ATT_HEAD_DIM ** -0.5
    kb = kb_ref[0]
    vb = vb_ref[0]
    for h in range(N_KV_HEADS):
        lo, hi = h * ATT_HEAD_DIM, (h + 1) * ATT_HEAD_DIM
        qh = q_ref[0, h * GROUP:(h + 1) * GROUP, :]
        kn = kn_ref[0, h:h + 1, :]
        vn = vn_ref[0, h:h + 1, :]
        sink = sink_ref[h * GROUP:(h + 1) * GROUP, :]
        s_buf = lax.dot_general(qh.astype(BF16), kb[:, lo:hi].astype(BF16), (((1,), (1,)), ((), ())),
                                preferred_element_type=F32) * scale
        s_new = jnp.sum(qh * kn, axis=-1, keepdims=True) * scale
        m = jnp.maximum(jnp.maximum(jnp.max(s_buf, axis=-1, keepdims=True), s_new), sink)
        e_buf = jnp.exp(s_buf - m)
        e_new = jnp.exp(s_new - m)
        denom = jnp.sum(e_buf, axis=-1, keepdims=True) + e_new + jnp.exp(sink - m)
        o = jnp.dot(e_buf.astype(BF16), vb[:, lo:hi].astype(BF16), preferred_element_type=F32)
        o = o + e_new * vn
        o_ref[0, h * GROUP:(h + 1) * GROUP, :] = (o / denom).astype(o_ref.dtype)
    ko_ref[0, 0:W - 1, :] = kb[1:W, :]
    ko_ref[0, W - 1:W, :] = knf_ref[0]
    vo_ref[0, 0:W - 1, :] = vb[1:W, :]
    vo_ref[0, W - 1:W, :] = vnf_ref[0]


def swa_sample(q3, kn3, vn3, k_buf, v_buf, sinks):
    B = q3.shape[0]
    W, d = WINDOW, ATT_HEAD_DIM
    head_spec = lambda n: pl.BlockSpec((1, n, d), lambda b: (b, 0, 0))
    flat_spec = pl.BlockSpec((1, 1, KV_WIDTH), lambda b: (b, 0, 0))
    buf_spec = pl.BlockSpec((1, W, KV_WIDTH), lambda b: (b, 0, 0))
    return pl.pallas_call(
        _swa_sample_body,
        grid=(B,),
        in_specs=[head_spec(N_HEADS), head_spec(N_KV_HEADS), head_spec(N_KV_HEADS), flat_spec, flat_spec,
                  buf_spec, buf_spec, pl.BlockSpec((N_HEADS, 1), lambda b: (0, 0))],
        out_specs=[head_spec(N_HEADS), buf_spec, buf_spec],
        out_shape=[jax.ShapeDtypeStruct((B, N_HEADS, d), BF16),
                   jax.ShapeDtypeStruct((B, W, KV_WIDTH), F32),
                   jax.ShapeDtypeStruct((B, W, KV_WIDTH), F32)],
        compiler_params=_params("parallel"),
        name="swa_sample",
    )(q3, kn3, vn3, kn3.reshape(B, 1, KV_WIDTH), vn3.reshape(B, 1, KV_WIDTH), k_buf, v_buf,
      sinks.reshape(N_HEADS, 1))


def _shared_up_body(x_ref, wg_ref, wu_ref, h_ref):
    x = x_ref[...]
    g = jnp.dot(x, wg_ref[0].astype(BF16), preferred_element_type=F32)
    u = jnp.dot(x, wu_ref[0].astype(BF16), preferred_element_type=F32)
    h_ref[...] = (_silu(g) * u).astype(h_ref.dtype)


def _shared_down_body(h_ref, wd_ref, y_ref):
    y_ref[...] = jnp.dot(h_ref[...], wd_ref[0].astype(BF16), preferred_element_type=F32)


def shared_ffn(xb, w_gate, w_up, w_down, layer, tm=512):
    M, D = xb.shape
    FF = w_gate.shape[-1]
    tm = min(tm, M)
    wsel = lambda i: (layer, 0, 0)
    h = pl.pallas_call(
        _shared_up_body,
        grid=(M // tm,),
        in_specs=[pl.BlockSpec((tm, D), lambda i: (i, 0)),
                  pl.BlockSpec((1, D, FF), wsel), pl.BlockSpec((1, D, FF), wsel)],
        out_specs=pl.BlockSpec((tm, FF), lambda i: (i, 0)),
        out_shape=jax.ShapeDtypeStruct((M, FF), BF16),
        compiler_params=_params("parallel"),
        name="shared_up",
    )(xb, w_gate, w_up)
    return pl.pallas_call(
        _shared_down_body,
        grid=(M // tm,),
        in_specs=[pl.BlockSpec((tm, FF), lambda i: (i, 0)), pl.BlockSpec((1, FF, D), wsel)],
        out_specs=pl.BlockSpec((tm, D), lambda i: (i, 0)),
        out_shape=jax.ShapeDtypeStruct((M, D), F32),
        compiler_params=_params("parallel"),
        name="shared_down",
    )(h, w_down)


def _expert_changed(be_ref, i):
    return (i == 0) | (be_ref[i] != be_ref[jnp.maximum(i - 1, 0)])


GATHER_SLOTS = 3


def _stream_expert_weights(be_ref, nx_ref, i, w_hbm, stage, wsem, w_bf, layer):
    def copy(e, j):
        return pltpu.make_async_copy(w_hbm[j].at[layer, e], stage.at[j], wsem.at[j])

    @pl.when(i == 0)
    def _():
        for j in range(len(w_hbm)):
            copy(be_ref[0], j).start()

    @pl.when(_expert_changed(be_ref, i))
    def _():
        e, nxt = be_ref[i], nx_ref[i]
        for j in range(len(w_hbm)):
            copy(e, j).wait()
            w_bf[j][...] = stage[j].astype(BF16)

            @pl.when(nxt != e)
            def _():
                copy(nxt, j).start()


def _moe_up_body(be_ref, nx_ref, nv_ref, tok_ref, x_hbm, wg_hbm, wu_hbm, h_ref, xbuf, sem, x_lhs, stage, wsem,
                 wg_bf, wu_bf, *, tm, layer):
    i = pl.program_id(0)
    last_blk = pl.num_programs(0) - 1
    half = wg_bf.shape[0] // 2

    def issue(blk, slot):
        n = jnp.where(blk > last_blk, 0, nv_ref[jnp.minimum(blk, last_blk)])
        base = blk * tm

        def body(r, carry):
            src = pl.multiple_of(tok_ref[base + r], TOK_ROWS)
            dst = pl.multiple_of(r * TOK_ROWS, TOK_ROWS)
            pltpu.make_async_copy(x_hbm.at[pl.ds(src, TOK_ROWS)], xbuf.at[slot, pl.ds(dst, TOK_ROWS)],
                                  sem.at[slot]).start()
            return carry

        lax.fori_loop(0, n, body, 0)

    def wait(blk, slot):
        rows = pl.multiple_of(nv_ref[blk] * TOK_ROWS, TOK_ROWS)
        pltpu.make_async_copy(x_hbm.at[pl.ds(0, rows)], xbuf.at[slot, pl.ds(0, rows)], sem.at[slot]).wait()

    @pl.when(i == 0)
    def _():
        xbuf[...] = jnp.zeros_like(xbuf)
        issue(0, 0)
        issue(1, 1)

    @pl.when(nv_ref[i] == 0)
    def _():
        h_ref[...] = jnp.zeros_like(h_ref)

    @pl.when(nv_ref[i] > 0)
    def _():
        slot = i % GATHER_SLOTS
        _stream_expert_weights(be_ref, nx_ref, i, (wg_hbm, wu_hbm), stage, wsem, (wg_bf, wu_bf), layer)
        wait(i, slot)
        for s in range(TOK_ROWS):
            lo, hi = _unpack_pair(_load_token_rows(xbuf.at[slot], s, tm))
            x_lhs[:, s * LANES:(s + 1) * LANES] = lo.astype(BF16)
            x_lhs[:, half + s * LANES:half + (s + 1) * LANES] = hi.astype(BF16)
        issue(i + 2, (i + 2) % GATHER_SLOTS)
        x = x_lhs[...]
        g = jnp.dot(x, wg_bf[...], preferred_element_type=F32)
        u = jnp.dot(x, wu_bf[...], preferred_element_type=F32)
        h_ref[...] = (_silu(g) * u).astype(h_ref.dtype)


def _moe_down_body(be_ref, nx_ref, nv_ref, dst_ref, h_ref, wd_hbm, y_hbm, ybuf, sem, stage, wsem, wd_bf, *, tm,
                   layer):
    i = pl.program_id(0)
    last_blk = pl.num_programs(0) - 1

    def wait(blk, slot):
        rows = pl.multiple_of(nv_ref[blk] * TOK_ROWS, TOK_ROWS)
        pltpu.make_async_copy(ybuf.at[slot, pl.ds(0, rows)], y_hbm.at[pl.ds(0, rows)], sem.at[slot]).wait()

    def scatter(blk, slot):
        base = blk * tm

        def body(r, carry):
            dst = pl.multiple_of(dst_ref[base + r], TOK_ROWS)
            src = pl.multiple_of(r * TOK_ROWS, TOK_ROWS)
            pltpu.make_async_copy(ybuf.at[slot, pl.ds(src, TOK_ROWS)], y_hbm.at[pl.ds(dst, TOK_ROWS)],
                                  sem.at[slot]).start()
            return carry

        lax.fori_loop(0, nv_ref[blk], body, 0)

    @pl.when(i >= 2)
    def _():
        @pl.when(nv_ref[jnp.maximum(i - 2, 0)] > 0)
        def _():
            wait(i - 2, i % 2)

    @pl.when(i >= 1)
    def _():
        scatter(jnp.maximum(i - 1, 0), (i + 1) % 2)

    @pl.when(nv_ref[i] > 0)
    def _():
        _stream_expert_weights(be_ref, nx_ref, i, (wd_hbm,), stage, wsem, (wd_bf,), layer)
        y = jnp.dot(h_ref[...], wd_bf[...], preferred_element_type=F32)
        _store_token_rows(ybuf.at[i % 2], y, tm)

    @pl.when(i == last_blk)
    def _():
        @pl.when(nv_ref[i] > 0)
        def _():
            scatter(i, i % 2)
            wait(i, i % 2)

        @pl.when((i >= 1) & (nv_ref[jnp.maximum(i - 1, 0)] > 0))
        def _():
            wait(i - 1, (i + 1) % 2)


def routed_ffn(x_packed, w_gate, w_up, w_down, layer, plan, tm):
    slot_token, slot_dest, block_e, next_e, n_valid = plan
    T = x_packed.shape[0] // TOK_ROWS
    NB = block_e.shape[0]
    D, FF = w_gate.shape[-2], w_gate.shape[-1]
    buf = lambda n: pltpu.VMEM((n, tm * TOK_ROWS, LANES), jnp.uint32)
    hbm = pl.BlockSpec(memory_space=pl.ANY)
    params = pltpu.CompilerParams(dimension_semantics=("arbitrary",), vmem_limit_bytes=V7X_VMEM_LIMIT)
    h = pl.pallas_call(
        functools.partial(_moe_up_body, tm=tm, layer=layer),
        grid_spec=pltpu.PrefetchScalarGridSpec(
            num_scalar_prefetch=4, grid=(NB,),
            in_specs=[hbm, hbm, hbm],
            out_specs=pl.BlockSpec((tm, FF), lambda i, be, nx, nv, ix: (i, 0)),
            scratch_shapes=[buf(GATHER_SLOTS), pltpu.SemaphoreType.DMA((GATHER_SLOTS,)),
                            pltpu.VMEM((tm, D), BF16),
                            pltpu.VMEM((2, D, FF), F32), pltpu.SemaphoreType.DMA((2,)),
                            pltpu.VMEM((D, FF), BF16), pltpu.VMEM((D, FF), BF16)]),
        out_shape=jax.ShapeDtypeStruct((NB * tm, FF), BF16),
        compiler_params=params,
        name="moe_up",
    )(block_e, next_e, n_valid, slot_token * TOK_ROWS, x_packed, w_gate, w_up)
    y = pl.pallas_call(
        functools.partial(_moe_down_body, tm=tm, layer=layer),
        grid_spec=pltpu.PrefetchScalarGridSpec(
            num_scalar_prefetch=4, grid=(NB,),
            in_specs=[pl.BlockSpec((tm, FF), lambda i, be, nx, nv, ix: (i, 0)), hbm],
            out_specs=hbm,
            scratch_shapes=[buf(2), pltpu.SemaphoreType.DMA((2,)),
                            pltpu.VMEM((1, FF, D), F32), pltpu.SemaphoreType.DMA((1,)),
                            pltpu.VMEM((FF, D), BF16)]),
        out_shape=jax.ShapeDtypeStruct((TOP_K * T * TOK_ROWS, LANES), jnp.uint32),
        compiler_params=params,
        name="moe_down",
    )(block_e, next_e, n_valid, slot_dest * TOK_ROWS, h, w_down)
    return y.reshape(TOP_K, T * TOK_ROWS, LANES)


SLOT_MAP_CHUNK = 2048
SLOT_MAP_UNROLL = 32


def _slot_map_body(dest_ref, src_ref, *, clear_chunk):
    phase, i = pl.program_id(0), pl.program_id(1)

    @pl.when(phase == 0)
    def _():
        def clear(j, carry):
            src_ref[i * clear_chunk + j] = -1
            return carry

        lax.fori_loop(0, clear_chunk, clear, 0, unroll=SLOT_MAP_UNROLL)

    @pl.when(phase == 1)
    def _():
        def place(a, carry):
            src_ref[dest_ref[a]] = i * SLOT_MAP_CHUNK + a
            return carry

        lax.fori_loop(0, SLOT_MAP_CHUNK, place, 0, unroll=SLOT_MAP_UNROLL)


def _slot_map(dest, n_slots):
    A = dest.shape[0]
    n_chunks = -(-A // SLOT_MAP_CHUNK)
    clear_chunk = -(-(n_slots + 1) // (SLOT_MAP_UNROLL * n_chunks)) * SLOT_MAP_UNROLL
    dest = jnp.concatenate([dest, jnp.full((n_chunks * SLOT_MAP_CHUNK - A,), n_slots, jnp.int32)])
    src = pl.pallas_call(
        functools.partial(_slot_map_body, clear_chunk=clear_chunk),
        grid=(2, n_chunks),
        in_specs=[pl.BlockSpec((SLOT_MAP_CHUNK,), lambda ph, i: (i * ph,), memory_space=pltpu.SMEM)],
        out_specs=pl.BlockSpec(memory_space=pltpu.SMEM),
        out_shape=jax.ShapeDtypeStruct((n_chunks * clear_chunk,), jnp.int32),
        compiler_params=pltpu.CompilerParams(dimension_semantics=("arbitrary", "arbitrary")),
        name="slot_map",
    )(dest)
    return src[:n_slots]


def _dispatch_plan(idx, tm):
    T = idx.shape[0]
    A = T * TOP_K
    NB = -(-A // tm) + N_EXPERTS
    idx = idx.astype(jnp.int32)
    onehot = idx[:, :, None] == jnp.arange(N_EXPERTS, dtype=jnp.int32)[None, None, :]
    sel = onehot.any(axis=1).astype(jnp.int32)
    incl = jnp.cumsum(sel, axis=0)
    counts = incl[-1]
    padded = (counts + tm - 1) // tm * tm
    pad_end = jnp.cumsum(padded)
    pad_start = pad_end - padded
    n_active = (pad_end[-1] // tm).astype(jnp.int32)
    blk = jnp.arange(NB, dtype=jnp.int32)
    block_e = jnp.minimum(jnp.searchsorted(pad_end, jnp.minimum(blk, n_active - 1) * tm, side='right'),
                          N_EXPERTS - 1).astype(jnp.int32)
    e_ids = jnp.arange(N_EXPERTS, dtype=jnp.int32)
    later_used = lax.cummin(jnp.where(counts > 0, e_ids, N_EXPERTS), reverse=True)
    next_used = jnp.concatenate([later_used[1:], jnp.full((1,), N_EXPERTS, jnp.int32)])
    next_e = jnp.where(next_used[block_e] < N_EXPERTS, next_used[block_e], block_e).astype(jnp.int32)
    slot_of = (incl - sel + pad_start[None, :])[:, None, :]
    dest = jnp.sum(jnp.where(onehot, slot_of, 0), axis=-1).reshape(A)
    slot_src = _slot_map(dest, NB * tm)
    valid = slot_src >= 0
    slot_token = jnp.where(valid, slot_src // TOP_K, 0)
    slot_dest = jnp.where(valid, (slot_src % TOP_K) * T + slot_src // TOP_K, 0)
    n_valid = jnp.clip(counts[block_e] - (blk * tm - pad_start[block_e]), 0, tm)
    n_valid = jnp.where(blk < n_active, n_valid, 0).astype(jnp.int32)
    return slot_token, slot_dest, block_e, next_e, n_valid


def moe_block(x_parts, xb_parts, xp_parts, p, layer):
    w_t = jnp.swapaxes(p['router_w'], 1, 2)
    bias_col = p['router_bias'][:, :, None]
    routed = [route(x, w_t, bias_col, layer) for x in x_parts]
    idx = jnp.concatenate([r[0] for r in routed], axis=1).T
    wts = jnp.concatenate([r[1] for r in routed], axis=1).T
    plan = _dispatch_plan(idx, MOE_TM)
    y = routed_ffn(jnp.concatenate(xp_parts, axis=0), p['exp_w_gate'], p['exp_w_up'], p['exp_w_down'],
                   layer, plan, MOE_TM)
    shared = [shared_ffn(xb, p['shared_w_gate'], p['shared_w_up'], p['shared_w_down'], layer)
              for xb in xb_parts]
    return y, wts, shared


def _rope_tables(pos, half):
    inv_freq = ROPE_THETA ** (-jnp.arange(half, dtype=F32) / half)
    ang = pos.astype(F32)[:, None] * inv_freq[None, :]
    return jnp.cos(ang), jnp.sin(ang)


def _even_prompt(x_bf, B, L, p, i):
    proj = matmul(x_bf, p['w_in_even'][i].astype(BF16), name="in_proj_even")
    proj3 = proj.reshape(B, L, -1)
    cos, sin = _rope_tables(jnp.arange(L, dtype=jnp.int32), RET_HEAD_DIM // 2)
    conv_out, conv_new = conv_prompt(proj3, p['conv_w'][i], p['conv_b'][i], p['conv_ln_g'][i], p['conv_ln_b'][i])
    ret_out, s_new = retention_prompt(proj3, cos, sin, p['ret_gn_g'][i], p['ret_gn_b'][i])
    mixed = jnp.concatenate([conv_out, ret_out], axis=-1).reshape(B * L, -1)
    return matmul(mixed, p['w_out_even'][i].astype(BF16), name="out_proj_even"), conv_new, s_new


def _even_sample(x_bf, conv_buf, ret_state, p, i):
    B = x_bf.shape[0]
    H, d = RET_HEADS, RET_HEAD_DIM
    proj = matmul(x_bf, p['w_in_even'][i].astype(BF16), name="in_proj_even")
    conv_out, conv_new = conv_sample(proj, conv_buf, p['conv_w'][i], p['conv_b'][i],
                                     p['conv_ln_g'][i], p['conv_ln_b'][i])
    o2 = 2 * CONV_CH
    q_col = proj[:, o2:o2 + RET_WIDTH].reshape(B, H, d, 1)
    k_col = proj[:, o2 + RET_WIDTH:o2 + 2 * RET_WIDTH].reshape(B, H, d, 1)
    v_row = proj[:, o2 + 2 * RET_WIDTH:o2 + 3 * RET_WIDTH].reshape(B, 1, RET_WIDTH)
    g_row = proj[:, o2 + 3 * RET_WIDTH:].reshape(B, 1, RET_WIDTH)
    cos, sin = _rope_tables(jnp.full((1,), PAST_LEN, jnp.int32), d // 2)
    ret_out, s_new = retention_sample(q_col, k_col, v_row, g_row, ret_state, cos.reshape(d // 2, 1),
                                      sin.reshape(d // 2, 1), p['ret_gn_g'][i], p['ret_gn_b'][i])
    mixed = jnp.concatenate([conv_out, ret_out.reshape(B, RET_WIDTH)], axis=-1)
    return matmul(mixed, p['w_out_even'][i].astype(BF16), name="out_proj_even"), conv_new, s_new


def _att_rope_tables(pos, rows):
    cos, sin = _rope_tables(pos, ATT_HEAD_DIM // 2)
    cos = jnp.tile(cos, (rows // cos.shape[0], LANES // cos.shape[1]))
    sin = jnp.tile(sin, (rows // sin.shape[0], LANES // sin.shape[1]))
    return cos, sin


def _odd_prompt(x_bf, B, L, p, i):
    cos, sin = _att_rope_tables(jnp.arange(L, dtype=jnp.int32), L)
    qkv = matmul(x_bf, p['w_qkv_odd'][i].astype(BF16), bias=p['b_qkv_odd'][i],
                 rope=(cos, sin, Q_WIDTH + KV_WIDTH), name="qkv_proj_odd")
    qkv3 = qkv.reshape(B, L, -1)
    o = swa_prompt(qkv3, p['sinks'][i])
    out = matmul(o.reshape(B * L, Q_WIDTH), p['w_out_odd'][i].astype(BF16), bias=p['b_out_odd'][i],
                 name="out_proj_odd")
    k_new = qkv3[:, L - WINDOW:, Q_WIDTH:Q_WIDTH + KV_WIDTH].reshape(B, WINDOW, N_KV_HEADS, ATT_HEAD_DIM)
    v_new = qkv3[:, L - WINDOW:, Q_WIDTH + KV_WIDTH:].reshape(B, WINDOW, N_KV_HEADS, ATT_HEAD_DIM)
    return out, k_new, v_new


def _odd_sample(x_bf, k_buf, v_buf, p, i):
    B = x_bf.shape[0]
    cos, sin = _att_rope_tables(jnp.full((1,), PAST_LEN, jnp.int32), B)
    qkv = matmul(x_bf, p['w_qkv_odd'][i].astype(BF16), bias=p['b_qkv_odd'][i],
                 rope=(cos, sin, Q_WIDTH + KV_WIDTH), name="qkv_proj_odd")
    q3 = qkv[:, :Q_WIDTH].reshape(B, N_HEADS, ATT_HEAD_DIM)
    kn3 = qkv[:, Q_WIDTH:Q_WIDTH + KV_WIDTH].reshape(B, N_KV_HEADS, ATT_HEAD_DIM)
    vn3 = qkv[:, Q_WIDTH + KV_WIDTH:].reshape(B, N_KV_HEADS, ATT_HEAD_DIM)
    o, k_new, v_new = swa_sample(q3, kn3, vn3, k_buf.reshape(B, WINDOW, KV_WIDTH),
                                 v_buf.reshape(B, WINDOW, KV_WIDTH), p['sinks'][i])
    out = matmul(o.reshape(B, Q_WIDTH), p['w_out_odd'][i].astype(BF16), bias=p['b_out_odd'][i],
                 name="out_proj_odd")
    shape = (B, WINDOW, N_KV_HEADS, ATT_HEAD_DIM)
    return out, k_new.reshape(shape), v_new.reshape(shape)


def kernel(x_prompt, x_sample, state_conv, state_ret, state_win_k, state_win_v,
           w_in_even, conv_w, conv_b, conv_ln_g, conv_ln_b, ret_gn_g, ret_gn_b, w_out_even,
           w_qkv_odd, b_qkv_odd, sinks, w_out_odd, b_out_odd, ln_g, ln_b,
           router_w, router_bias, exp_w_gate, exp_w_up, exp_w_down,
           shared_w_gate, shared_w_up, shared_w_down):
    p = dict(w_in_even=w_in_even, conv_w=conv_w, conv_b=conv_b, conv_ln_g=conv_ln_g, conv_ln_b=conv_ln_b,
             ret_gn_g=ret_gn_g, ret_gn_b=ret_gn_b, w_out_even=w_out_even,
             w_qkv_odd=w_qkv_odd, b_qkv_odd=b_qkv_odd, sinks=sinks, w_out_odd=w_out_odd, b_out_odd=b_out_odd,
             ln_g=ln_g, ln_b=ln_b, router_w=router_w, router_bias=router_bias,
             exp_w_gate=exp_w_gate, exp_w_up=exp_w_up, exp_w_down=exp_w_down,
             shared_w_gate=shared_w_gate, shared_w_up=shared_w_up, shared_w_down=shared_w_down)
    B, L, D = x_prompt.shape
    Bs = x_sample.shape[0]
    Tp = B * L
    xp = x_prompt.reshape(Tp, D)
    xs = x_sample.reshape(Bs, D)
    xp_bf, xs_bf = xp.astype(BF16), xs.astype(BF16)
    conv_p, conv_s, ret_p, ret_s, wk_p, wk_s, wv_p, wv_s = [], [], [], [], [], [], [], []
    for layer in range(DEPTH):
        i = layer // 2
        if layer % 2 == 0:
            hp, c_new, r_new = _even_prompt(xp_bf, B, L, p, i)
            conv_p.append(c_new)
            ret_p.append(r_new)
            hs, c_new, r_new = _even_sample(xs_bf, state_conv[i], state_ret[i], p, i)
            conv_s.append(c_new)
            ret_s.append(r_new)
        else:
            hp, k_new, v_new = _odd_prompt(xp_bf, B, L, p, i)
            wk_p.append(k_new)
            wv_p.append(v_new)
            hs, k_new, v_new = _odd_sample(xs_bf, state_win_k[i], state_win_v[i], p, i)
            wk_s.append(k_new)
            wv_s.append(v_new)
        xp, xp_bf, xp_pk = deepnorm(xp, [(hp, 0)], ln_g[layer, 0], ln_b[layer, 0], name="ln_mixer_prompt")
        xs, xs_bf, xs_pk = deepnorm(xs, [(hs, 0)], ln_g[layer, 0], ln_b[layer, 0], name="ln_mixer_sample")
        y, wts, shared = moe_block([xp, xs], [xp_bf, xs_bf], [xp_pk, xs_pk], p, layer)
        xp, xp_bf, _ = deepnorm(xp, [(shared[0], 0)], ln_g[layer, 1], ln_b[layer, 1], combine=(y, wts, 0),
                                tm=64, name="ln_moe_prompt")
        xs, xs_bf, _ = deepnorm(xs, [(shared[1], 0)], ln_g[layer, 1], ln_b[layer, 1], combine=(y, wts, Tp),
                                tm=32, name="ln_moe_sample")
    return (xp.reshape(B, L, D), xs.reshape(Bs, 1, D),
            jnp.stack(conv_p), jnp.stack(conv_s), jnp.stack(ret_p), jnp.stack(ret_s),
            jnp.stack(wk_p), jnp.stack(wk_s), jnp.stack(wv_p), jnp.stack(wv_s))
```

```python
import functools

import jax
import jax.numpy as jnp
import numpy as np
from jax import lax
from jax.experimental import pallas as pl
from jax.experimental.pallas import tpu as pltpu

F32 = jnp.float32
BF16 = jnp.bfloat16

D_MODEL = 4096
DEPTH = 2
PAST_LEN = 16384
CONV_CH = D_MODEL // 2
CONV_WIDTH = 31
CONV_HALO = 32
RET_HEADS = 8
RET_HEAD_DIM = 256
RET_WIDTH = RET_HEADS * RET_HEAD_DIM
RET_CHUNK = 128
RET_STEP_ROWS = 512
ATT_HEAD_DIM = 64
N_HEADS = 64
N_KV_HEADS = 8
GROUP = N_HEADS // N_KV_HEADS
WINDOW = 128
Q_WIDTH = N_HEADS * ATT_HEAD_DIM
KV_WIDTH = N_KV_HEADS * ATT_HEAD_DIM
ROPE_THETA = 10000.0
N_EXPERTS = 128
TOP_K = 8
N_GROUPS = 8
TOPK_GROUPS = 4
EXPERTS_PER_GROUP = N_EXPERTS // N_GROUPS
EXPERT_FF = 512
ROUTED_SCALE = 2.5
DN_ALPHA = (2.0 * DEPTH) ** 0.25
LN_EPS = 1e-5

LANES = 128
V7X_VMEM_LIMIT = 52 * 1024 * 1024
MOE_TM = 256


def _params(*sem):
    return pltpu.CompilerParams(dimension_semantics=sem, vmem_limit_bytes=V7X_VMEM_LIMIT)


def _silu(x):
    return x * jax.nn.sigmoid(x)


def _rot_half_64(x):
    n = x.shape[-1]
    lane = lax.broadcasted_iota(jnp.int32, x.shape, x.ndim - 1)
    fwd = pltpu.roll(x, n - ATT_HEAD_DIM // 2, x.ndim - 1)
    bwd = pltpu.roll(x, ATT_HEAD_DIM // 2, x.ndim - 1)
    return jnp.where(lane % ATT_HEAD_DIM < ATT_HEAD_DIM // 2, -fwd, bwd)


def _mm_body(*refs, has_bias, rope_tiles):
    x_ref, w_ref = refs[0], refs[1]
    pos = 2
    b_ref = None
    if has_bias:
        b_ref = refs[pos]
        pos += 1
    if rope_tiles:
        cos_ref, sin_ref = refs[pos], refs[pos + 1]
        pos += 2
    o_ref = refs[pos]
    acc = jnp.dot(x_ref[...].astype(BF16), w_ref[...].astype(BF16), preferred_element_type=F32)
    if has_bias:
        acc = acc + b_ref[...]
    if rope_tiles:
        reps = acc.shape[1] // LANES
        cos = jnp.concatenate([cos_ref[...]] * reps, axis=1)
        sin = jnp.concatenate([sin_ref[...]] * reps, axis=1)
        @pl.when(pl.program_id(1) < rope_tiles)
        def _():
            o_ref[...] = (acc * cos + _rot_half_64(acc) * sin).astype(o_ref.dtype)

        @pl.when(pl.program_id(1) >= rope_tiles)
        def _():
            o_ref[...] = acc.astype(o_ref.dtype)
    else:
        o_ref[...] = acc.astype(o_ref.dtype)


def matmul(x, w, bias=None, rope=None, out_dtype=F32, tm=1024, tn=512, name="matmul"):
    M, K = x.shape
    N = w.shape[1]
    tm = min(tm, M)
    tn = min(tn, N)
    assert M % tm == 0 and N % tn == 0
    in_specs = [pl.BlockSpec((tm, K), lambda i, j: (i, 0)),
                pl.BlockSpec((K, tn), lambda i, j: (0, j))]
    args = [x, w]
    if bias is not None:
        in_specs.append(pl.BlockSpec((1, tn), lambda i, j: (0, j)))
        args.append(bias.reshape(1, N))
    rope_tiles = 0
    if rope is not None:
        cos, sin, n_cols = rope
        assert n_cols % tn == 0 and cos.shape[0] % tm == 0
        rope_tiles = n_cols // tn
        nrb = cos.shape[0] // tm
        in_specs += [pl.BlockSpec((tm, LANES), lambda i, j: (i % nrb, 0))] * 2
        args += [cos, sin]
    return pl.pallas_call(
        functools.partial(_mm_body, has_bias=bias is not None, rope_tiles=rope_tiles),
        grid=(M // tm, N // tn),
        in_specs=in_specs,
        out_specs=pl.BlockSpec((tm, tn), lambda i, j: (i, j)),
        out_shape=jax.ShapeDtypeStruct((M, N), out_dtype),
        compiler_params=_params("parallel", "arbitrary"),
        name=name,
    )(*args)


def _split_bf16(x):
    hi = x.astype(BF16)
    lo = (x - hi.astype(F32)).astype(BF16)
    return hi, lo


def _first_argmax(x, row, n):
    m = jnp.max(x, axis=0, keepdims=True)
    i = jnp.min(jnp.where(x == m, row, n), axis=0, keepdims=True)
    return m, i


def _router_body(x_ref, wt_ref, bias_ref, idx_ref, wts_ref):
    nt = (((1,), (1,)), ((), ()))
    xh, xl = _split_bf16(x_ref[...])
    wh, wl = _split_bf16(wt_ref[0])
    logits = lax.dot_general(wh, xh, nt, preferred_element_type=F32)
    logits = logits + lax.dot_general(wh, xl, nt, preferred_element_type=F32)
    logits = logits + lax.dot_general(wl, xh, nt, preferred_element_type=F32)
    scores = jax.nn.sigmoid(logits)
    biased = scores + bias_ref[0]
    tm = scores.shape[1]
    neg = -jnp.inf
    row_g = lax.broadcasted_iota(jnp.int32, (EXPERTS_PER_GROUP, tm), 0)
    grp = []
    for g in range(N_GROUPS):
        xg = biased[g * EXPERTS_PER_GROUP:(g + 1) * EXPERTS_PER_GROUP]
        m1, i1 = _first_argmax(xg, row_g, EXPERTS_PER_GROUP)
        m2 = jnp.max(jnp.where(row_g == i1, neg, xg), axis=0, keepdims=True)
        grp.append(m1 + m2)
    gs = jnp.concatenate(grp, axis=0)
    row_n = lax.broadcasted_iota(jnp.int32, (N_GROUPS, tm), 0)
    chosen = jnp.zeros((N_GROUPS, tm), F32)
    for _ in range(TOPK_GROUPS):
        _, i = _first_argmax(gs, row_n, N_GROUPS)
        hit = row_n == i
        chosen = jnp.where(hit, 1.0, chosen)
        gs = jnp.where(hit, neg, gs)
    emask = jnp.concatenate([jnp.broadcast_to(chosen[g:g + 1], (EXPERTS_PER_GROUP, tm))
                             for g in range(N_GROUPS)], axis=0)
    cand = jnp.where(emask > 0.0, biased, neg)
    row_e = lax.broadcasted_iota(jnp.int32, (N_EXPERTS, tm), 0)
    ids, ws = [], []
    for _ in range(TOP_K):
        _, i = _first_argmax(cand, row_e, N_EXPERTS)
        hit = row_e == i
        ids.append(i)
        ws.append(jnp.sum(jnp.where(hit, scores, 0.0), axis=0, keepdims=True))
        cand = jnp.where(hit, neg, cand)
    w = jnp.concatenate(ws, axis=0)
    idx_ref[...] = jnp.concatenate(ids, axis=0)
    wts_ref[...] = w / jnp.sum(w, axis=0, keepdims=True) * ROUTED_SCALE


def route(x, w_t, bias_col, layer, tm=512):
    M, K = x.shape
    E = w_t.shape[1]
    tm = min(tm, M)
    return pl.pallas_call(
        _router_body,
        grid=(M // tm,),
        in_specs=[pl.BlockSpec((tm, K), lambda i: (i, 0)),
                  pl.BlockSpec((1, E, K), lambda i: (layer, 0, 0)),
                  pl.BlockSpec((1, E, 1), lambda i: (layer, 0, 0))],
        out_specs=[pl.BlockSpec((TOP_K, tm), lambda i: (0, i))] * 2,
        out_shape=[jax.ShapeDtypeStruct((TOP_K, M), jnp.int32), jax.ShapeDtypeStruct((TOP_K, M), F32)],
        compiler_params=_params("parallel"),
        name="router",
    )(x, w_t, bias_col)


def _layer_norm_rows(z, g, b):
    mu = jnp.mean(z, axis=-1, keepdims=True)
    zc = z - mu
    var = jnp.mean(zc * zc, axis=-1, keepdims=True)
    return zc * lax.rsqrt(var + LN_EPS) * g + b


def _unpack_pair(w):
    lo = lax.bitcast_convert_type(w << 16, F32)
    hi = lax.bitcast_convert_type(w & jnp.uint32(0xFFFF0000), F32)
    return lo, hi


def _pack_pair(lo, hi):
    lo_bits = lax.bitcast_convert_type(lo.astype(BF16).astype(F32), jnp.uint32) >> 16
    hi_bits = lax.bitcast_convert_type(hi.astype(BF16).astype(F32), jnp.uint32) & jnp.uint32(0xFFFF0000)
    return hi_bits | lo_bits


TOK_ROWS = D_MODEL // 2 // LANES


def _load_token_rows(ref, s, tm):
    return ref[pl.ds(s, tm, stride=TOK_ROWS), :]


def _store_token_rows(ref, y, tm):
    half = y.shape[1] // 2
    for s in range(TOK_ROWS):
        words = _pack_pair(y[:, s * LANES:(s + 1) * LANES], y[:, half + s * LANES:half + (s + 1) * LANES])
        ref[pl.ds(s, tm, stride=TOK_ROWS), :] = words


def _ln_body(*refs, n_h, combine):
    x_ref = refs[0]
    h_refs = refs[1:1 + n_h]
    pos = 1 + n_h
    if combine:
        y_ref, w_ref = refs[pos], refs[pos + 1]
        pos += 2
    g_ref, b_ref, o_ref, ob_ref, op_ref = refs[pos:]
    tm, d = x_ref.shape
    half = d // 2
    z = DN_ALPHA * x_ref[...]
    for h_ref in h_refs:
        z = z + h_ref[...]
    if combine:
        wk = [jnp.broadcast_to(w_ref[:, k:k + 1], (tm, LANES)) for k in range(TOP_K)]
        cols_lo, cols_hi = [], []
        for s in range(TOK_ROWS):
            acc_lo = z[:, s * LANES:(s + 1) * LANES]
            acc_hi = z[:, half + s * LANES:half + (s + 1) * LANES]
            for k in range(TOP_K):
                lo, hi = _unpack_pair(_load_token_rows(y_ref.at[k], s, tm))
                acc_lo = acc_lo + wk[k] * lo
                acc_hi = acc_hi + wk[k] * hi
            cols_lo.append(acc_lo)
            cols_hi.append(acc_hi)
        z = jnp.concatenate(cols_lo + cols_hi, axis=1)
    y = _layer_norm_rows(z, g_ref[...], b_ref[...])
    o_ref[...] = y
    ob_ref[...] = y.astype(BF16)
    _store_token_rows(op_ref, y, tm)


def deepnorm(x, hs, g, b, combine=None, tm=256, name="deepnorm"):
    M, D = x.shape
    tm = min(tm, M)
    assert M % tm == 0
    row_spec = lambda width, off: pl.BlockSpec((tm, width), functools.partial(lambda i, o: (i + o, 0), o=off // tm))
    in_specs = [row_spec(D, 0)]
    args = [x]
    for h, off in hs:
        assert off % tm == 0
        in_specs.append(row_spec(D, off))
        args.append(h)
    if combine is not None:
        y, wts, off = combine
        assert off % tm == 0
        y_spec = pl.BlockSpec((TOP_K, tm * TOK_ROWS, LANES),
                              functools.partial(lambda i, o: (0, i + o, 0), o=off // tm))
        in_specs += [y_spec, row_spec(TOP_K, off)]
        args += [y, wts]
    in_specs += [pl.BlockSpec((1, D), lambda i: (0, 0))] * 2
    args += [g.reshape(1, D), b.reshape(1, D)]
    return pl.pallas_call(
        functools.partial(_ln_body, n_h=len(hs), combine=combine is not None),
        grid=(M // tm,),
        in_specs=in_specs,
        out_specs=[row_spec(D, 0), row_spec(D, 0), pl.BlockSpec((tm * TOK_ROWS, LANES), lambda i: (i, 0))],
        out_shape=[jax.ShapeDtypeStruct((M, D), F32), jax.ShapeDtypeStruct((M, D), BF16),
                   jax.ShapeDtypeStruct((M * TOK_ROWS, LANES), jnp.uint32)],
        compiler_params=_params("parallel"),
        name=name,
    )(*args)


def _conv_prompt_body(a_ref, g_ref, pa_ref, pg_ref, w_ref, bdw_ref, lng_ref, lnb_ref,
                      o_ref, st_ref, full_ref, y_ref, *, tl):
    i = pl.program_id(1)
    u = a_ref[0] * jax.nn.sigmoid(g_ref[0])
    up = pa_ref[0] * jax.nn.sigmoid(pg_ref[0])
    up = jnp.where(i > 0, up, 0.0)
    full_ref[0:CONV_HALO, :] = up
    full_ref[CONV_HALO:CONV_HALO + tl, :] = u
    base = CONV_HALO - (CONV_WIDTH - 1)
    n_col = u.shape[1] // LANES

    def col_step(c, carry):
        off = pl.multiple_of(c * LANES, LANES)
        acc = jnp.zeros((tl, LANES), F32)
        for j in range(CONV_WIDTH):
            acc = acc + w_ref[j:j + 1, pl.ds(off, LANES)] * full_ref[base + j:base + j + tl, pl.ds(off, LANES)]
        y_ref[:, pl.ds(off, LANES)] = acc
        return carry

    lax.fori_loop(0, n_col, col_step, 0)
    y = _layer_norm_rows(y_ref[...] + bdw_ref[...], lng_ref[...], lnb_ref[...])
    o_ref[0] = _silu(y).astype(o_ref.dtype)

    @pl.when(i == pl.num_programs(1) - 1)
    def _():
        st_ref[0] = full_ref[CONV_HALO + tl - (CONV_WIDTH - 1):CONV_HALO + tl, :]


def conv_prompt(proj3, w_dw, b_dw, ln_g, ln_b, tl=128):
    B, L, _ = proj3.shape
    C = CONV_CH
    nh = tl // CONV_HALO
    vec = lambda v: v.reshape(1, C)
    return pl.pallas_call(
        functools.partial(_conv_prompt_body, tl=tl),
        grid=(B, L // tl),
        in_specs=[
            pl.BlockSpec((1, tl, C), lambda b, i: (b, i, 0)),
            pl.BlockSpec((1, tl, C), lambda b, i: (b, i, 1)),
            pl.BlockSpec((1, CONV_HALO, C), lambda b, i: (b, jnp.maximum(i * nh - 1, 0), 0)),
            pl.BlockSpec((1, CONV_HALO, C), lambda b, i: (b, jnp.maximum(i * nh - 1, 0), 1)),
            pl.BlockSpec((CONV_WIDTH, C), lambda b, i: (0, 0)),
            pl.BlockSpec((1, C), lambda b, i: (0, 0)),
            pl.BlockSpec((1, C), lambda b, i: (0, 0)),
            pl.BlockSpec((1, C), lambda b, i: (0, 0)),
        ],
        out_specs=[
            pl.BlockSpec((1, tl, C), lambda b, i: (b, i, 0)),
            pl.BlockSpec((1, CONV_WIDTH - 1, C), lambda b, i: (b, 0, 0)),
        ],
        out_shape=[jax.ShapeDtypeStruct((B, L, C), BF16),
                   jax.ShapeDtypeStruct((B, CONV_WIDTH - 1, C), F32)],
        scratch_shapes=[pltpu.VMEM((CONV_HALO + tl, C), F32), pltpu.VMEM((tl, C), F32)],
        compiler_params=_params("parallel", "arbitrary"),
        name="conv_prompt",
    )(proj3, proj3, proj3, proj3, w_dw, vec(b_dw), vec(ln_g), vec(ln_b))


def _conv_sample_body(a_ref, g_ref, buf_ref, w_ref, bdw_ref, lng_ref, lnb_ref, o_ref, st_ref):
    u = a_ref[...] * jax.nn.sigmoid(g_ref[...])
    nb = CONV_WIDTH - 1
    acc = w_ref[nb:nb + 1, :] * u
    for j in range(nb):
        acc = acc + w_ref[j:j + 1, :] * buf_ref[:, j, :]
    y = _layer_norm_rows(acc + bdw_ref[...], lng_ref[...], lnb_ref[...])
    o_ref[...] = _silu(y).astype(o_ref.dtype)
    st_ref[:, 0:nb - 1, :] = buf_ref[:, 1:nb, :]
    st_ref[:, nb - 1, :] = u


def conv_sample(proj, buf, w_dw, b_dw, ln_g, ln_b, tb=8):
    B = proj.shape[0]
    C = CONV_CH
    nb = CONV_WIDTH - 1
    vec = lambda v: v.reshape(1, C)
    return pl.pallas_call(
        _conv_sample_body,
        grid=(B // tb,),
        in_specs=[
            pl.BlockSpec((tb, C), lambda b: (b, 0)),
            pl.BlockSpec((tb, C), lambda b: (b, 1)),
            pl.BlockSpec((tb, nb, C), lambda b: (b, 0, 0)),
            pl.BlockSpec((CONV_WIDTH, C), lambda b: (0, 0)),
            pl.BlockSpec((1, C), lambda b: (0, 0)),
            pl.BlockSpec((1, C), lambda b: (0, 0)),
            pl.BlockSpec((1, C), lambda b: (0, 0)),
        ],
        out_specs=[pl.BlockSpec((tb, C), lambda b: (b, 0)),
                   pl.BlockSpec((tb, nb, C), lambda b: (b, 0, 0))],
        out_shape=[jax.ShapeDtypeStruct((B, C), BF16), jax.ShapeDtypeStruct((B, nb, C), F32)],
        compiler_params=_params("parallel"),
        name="conv_sample",
    )(proj, proj, buf, w_dw, vec(b_dw), vec(ln_g), vec(ln_b))


def _ret_log_gamma():
    return np.log1p(-np.exp2(-5.0 - np.arange(RET_HEADS, dtype=np.float64)))


def _group_norm_gate(o, gate, gn_g, gn_b):
    mu = jnp.mean(o, axis=-1, keepdims=True)
    oc = o - mu
    var = jnp.mean(oc * oc, axis=-1, keepdims=True)
    return (oc * lax.rsqrt(var + LN_EPS) * gn_g + gn_b) * _silu(gate)


def _rope_256(x, cos, sin):
    half = RET_HEAD_DIM // 2
    x1, x2 = x[..., :half], x[..., half:]
    return jnp.concatenate([x1 * cos - x2 * sin, x2 * cos + x1 * sin], axis=-1)


def _ret_prompt_body(q_ref, k_ref, v_ref, g_ref, cos_ref, sin_ref, intra_ref, qd_ref, kd_ref, cd_ref,
                     gng_ref, gnb_ref, o_ref, s_ref):
    c = pl.program_id(2)

    @pl.when(c == 0)
    def _():
        s_ref[...] = jnp.zeros_like(s_ref)

    cw = RET_CHUNK
    for j in range(q_ref.shape[1] // cw):
        rows = slice(j * cw, (j + 1) * cw)
        cos, sin = cos_ref[rows, :], sin_ref[rows, :]
        q = _rope_256(q_ref[0, rows, :], cos, sin)
        k = _rope_256(k_ref[0, rows, :], cos, sin) * (RET_HEAD_DIM ** -0.5)
        vb = v_ref[0, rows, :].astype(BF16)
        s_prev = s_ref[0, 0]
        scores = lax.dot_general(q.astype(BF16), k.astype(BF16), (((1,), (1,)), ((), ())),
                                 preferred_element_type=F32) * intra_ref[0]
        o = jnp.dot(scores.astype(BF16), vb, preferred_element_type=F32)
        o = o + jnp.dot((q * qd_ref[0]).astype(BF16), s_prev.astype(BF16), preferred_element_type=F32)
        kdt = (k * kd_ref[0]).T.astype(BF16)
        kv = jnp.dot(kdt, vb, preferred_element_type=F32)
        s_ref[0, 0] = cd_ref[0] * s_prev + kv
        o_ref[0, rows, :] = _group_norm_gate(o, g_ref[0, rows, :], gng_ref[...], gnb_ref[...]).astype(o_ref.dtype)


def retention_prompt(proj3, cos, sin, gn_g, gn_b):
    B, L, _ = proj3.shape
    H, d, c = RET_HEADS, RET_HEAD_DIM, RET_CHUNK
    lg = _ret_log_gamma()
    idx = np.arange(c, dtype=np.float64)
    rel = idx[:, None] - idx[None, :]
    intra = np.where(rel[None] >= 0, np.exp(lg[:, None, None] * np.maximum(rel, 0.0)[None]), 0.0)
    ones = np.ones((1, 1, d))
    qd = np.exp(lg[:, None] * (idx + 1.0)[None, :])[:, :, None] * ones
    kd = np.exp(lg[:, None] * (c - 1.0 - idx)[None, :])[:, :, None] * ones
    cd = np.exp(lg * c)[:, None, None] * ones
    col0 = 2 * CONV_CH // d
    tl = min(RET_STEP_ROWS, L)
    assert L % tl == 0 and tl % c == 0
    head_spec = lambda off: pl.BlockSpec((1, tl, d), lambda b, h, n: (b, n, col0 + off * H + h))
    tab_spec = pl.BlockSpec((1, c, d), lambda b, h, n: (h, 0, 0))
    return pl.pallas_call(
        _ret_prompt_body,
        grid=(B, H, L // tl),
        in_specs=[
            head_spec(0), head_spec(1), head_spec(2), head_spec(3),
            pl.BlockSpec((tl, d // 2), lambda b, h, n: (n, 0)),
            pl.BlockSpec((tl, d // 2), lambda b, h, n: (n, 0)),
            pl.BlockSpec((1, c, c), lambda b, h, n: (h, 0, 0)),
            tab_spec, tab_spec,
            pl.BlockSpec((1, 1, d), lambda b, h, n: (h, 0, 0)),
            pl.BlockSpec((1, d), lambda b, h, n: (0, h)),
            pl.BlockSpec((1, d), lambda b, h, n: (0, h)),
        ],
        out_specs=[
            pl.BlockSpec((1, tl, d), lambda b, h, n: (b, n, h)),
            pl.BlockSpec((1, 1, d, d), lambda b, h, n: (b, h, 0, 0)),
        ],
        out_shape=[jax.ShapeDtypeStruct((B, L, H * d), BF16),
                   jax.ShapeDtypeStruct((B, H, d, d), F32)],
        compiler_params=_params("parallel", "parallel", "arbitrary"),
        name="retention_prompt",
    )(proj3, proj3, proj3, proj3, cos, sin,
      jnp.asarray(intra, F32), jnp.asarray(qd, F32), jnp.asarray(kd, F32), jnp.asarray(cd, F32),
      gn_g.reshape(1, H * d), gn_b.reshape(1, H * d))


def _ret_sample_body(qc_ref, kc_ref, v_ref, g_ref, s_ref, cosc_ref, sinc_ref, gam_ref, gng_ref, gnb_ref,
                     o_ref, so_ref):
    half = RET_HEAD_DIM // 2
    cos, sin = cosc_ref[...], sinc_ref[...]

    def rope_col(x):
        x1, x2 = x[:half], x[half:]
        return jnp.concatenate([x1 * cos - x2 * sin, x2 * cos + x1 * sin], axis=0)

    for h in range(RET_HEADS):
        lo, hi = h * RET_HEAD_DIM, (h + 1) * RET_HEAD_DIM
        q = rope_col(qc_ref[0, h])
        k = rope_col(kc_ref[0, h]) * (RET_HEAD_DIM ** -0.5)
        v = v_ref[0, :, lo:hi]
        s_new = gam_ref[h] * s_ref[0, h] + k * v
        so_ref[0, h] = s_new
        o = jnp.sum(q * s_new, axis=0, keepdims=True)
        o_ref[0, :, lo:hi] = _group_norm_gate(o, g_ref[0, :, lo:hi], gng_ref[:, lo:hi],
                                              gnb_ref[:, lo:hi]).astype(o_ref.dtype)


def retention_sample(q_col, k_col, v_row, g_row, state, cos_col, sin_col, gn_g, gn_b):
    B = state.shape[0]
    H, d = RET_HEADS, RET_HEAD_DIM
    gam = jnp.asarray(np.exp(_ret_log_gamma()), F32)
    col_spec = pl.BlockSpec((1, H, d, 1), lambda b: (b, 0, 0, 0))
    row_spec = pl.BlockSpec((1, 1, H * d), lambda b: (b, 0, 0))
    return pl.pallas_call(
        _ret_sample_body,
        grid=(B,),
        in_specs=[
            col_spec, col_spec, row_spec, row_spec,
            pl.BlockSpec((1, H, d, d), lambda b: (b, 0, 0, 0)),
            pl.BlockSpec((d // 2, 1), lambda b: (0, 0)),
            pl.BlockSpec((d // 2, 1), lambda b: (0, 0)),
            pl.BlockSpec(memory_space=pltpu.SMEM),
            pl.BlockSpec((1, H * d), lambda b: (0, 0)),
            pl.BlockSpec((1, H * d), lambda b: (0, 0)),
        ],
        out_specs=[row_spec, pl.BlockSpec((1, H, d, d), lambda b: (b, 0, 0, 0))],
        out_shape=[jax.ShapeDtypeStruct((B, 1, H * d), BF16), jax.ShapeDtypeStruct((B, H, d, d), F32)],
        compiler_params=_params("parallel"),
        name="retention_sample",
    )(q_col, k_col, v_row, g_row, state, cos_col, sin_col, gam, gn_g.reshape(1, H * d), gn_b.reshape(1, H * d))


def _pair_operand(x2, h, lane):
    col = x2[:, (h // 2) * LANES:(h // 2 + 1) * LANES]
    swapped = pltpu.roll(col, ATT_HEAD_DIM, 1)
    first, second = (col, swapped) if h % 2 == 0 else (swapped, col)
    return jnp.concatenate([jnp.where(lane < ATT_HEAD_DIM, first, 0.0),
                            jnp.where(lane >= ATT_HEAD_DIM, second, 0.0)], axis=0).astype(BF16)


def _swa_prompt_body(sink_ref, q_ref, kc_ref, kp_ref, vc_ref, vp_ref, o_ref):
    j = pl.program_id(1)
    W = WINDOW
    k2 = jnp.concatenate([kp_ref[0], kc_ref[0]], axis=0)
    v2 = jnp.concatenate([vp_ref[0], vc_ref[0]], axis=0)
    row = lax.broadcasted_iota(jnp.int32, (W, 2 * W), 0)
    col = lax.broadcasted_iota(jnp.int32, (W, 2 * W), 1)
    first_key = jnp.where(j > 0, 0, W)
    mask = (col >= jnp.maximum(row, first_key)) & (col <= W + row)
    lane_kv = lax.broadcasted_iota(jnp.int32, (2 * W, LANES), 1)
    lane_o = lax.broadcasted_iota(jnp.int32, (W, LANES), 1)
    scale = ATT_HEAD_DIM ** -0.5
    for h in range(N_KV_HEADS):
        kk = _pair_operand(k2, h, lane_kv)
        vv = _pair_operand(v2, h, lane_kv)
        for p in range(GROUP // 2):
            c0 = h * GROUP * ATT_HEAD_DIM + p * LANES
            qp = q_ref[0, :, c0:c0 + LANES].astype(BF16)
            s = lax.dot_general(qp, kk, (((1,), (1,)), ((), ())), preferred_element_type=F32) * scale
            probs, inv = [], []
            for t in range(2):
                sink = sink_ref[h * GROUP + 2 * p + t]
                st = jnp.where(mask, s[:, t * 2 * W:(t + 1) * 2 * W], -jnp.inf)
                m = jnp.maximum(jnp.max(st, axis=-1, keepdims=True), sink)
                e = jnp.exp(st - m)
                denom = jnp.sum(e, axis=-1, keepdims=True) + jnp.exp(sink - m)
                probs.append(e.astype(BF16))
                inv.append(1.0 / denom)
            o2 = jnp.dot(jnp.concatenate(probs, axis=1), vv, preferred_element_type=F32)
            o2 = o2 * jnp.where(lane_o < ATT_HEAD_DIM, inv[0], inv[1])
            o_ref[0, :, c0:c0 + LANES] = o2.astype(o_ref.dtype)


def swa_prompt(qkv3, sinks):
    B, L, _ = qkv3.shape
    W = WINDOW
    kcol = Q_WIDTH // KV_WIDTH
    return pl.pallas_call(
        _swa_prompt_body,
        grid_spec=pltpu.PrefetchScalarGridSpec(
            num_scalar_prefetch=0,
            grid=(B, L // W),
            in_specs=[
                pl.BlockSpec(memory_space=pltpu.SMEM),
                pl.BlockSpec((1, W, Q_WIDTH), lambda b, j: (b, j, 0)),
                pl.BlockSpec((1, W, KV_WIDTH), lambda b, j: (b, j, kcol)),
                pl.BlockSpec((1, W, KV_WIDTH), lambda b, j: (b, jnp.maximum(j - 1, 0), kcol)),
                pl.BlockSpec((1, W, KV_WIDTH), lambda b, j: (b, j, kcol + 1)),
                pl.BlockSpec((1, W, KV_WIDTH), lambda b, j: (b, jnp.maximum(j - 1, 0), kcol + 1)),
            ],
            out_specs=pl.BlockSpec((1, W, Q_WIDTH), lambda b, j: (b, j, 0)),
        ),
        out_shape=jax.ShapeDtypeStruct((B, L, Q_WIDTH), BF16),
        compiler_params=_params("parallel", "parallel"),
        name="swa_prompt",
    )(sinks, qkv3, qkv3, qkv3, qkv3, qkv3)


def _swa_sample_body(q_ref, kn_ref, vn_ref, knf_ref, vnf_ref, kb_ref, vb_ref, sink_ref, o_ref, ko_ref, vo_ref):
    W = WINDOW
    scale = ATT_HEAD_DIM ** -0.5
    kb = kb_ref[0]
    vb = vb_ref[0]
    for h in range(N_KV_HEADS):
        lo, hi = h * ATT_HEAD_DIM, (h + 1) * ATT_HEAD_DIM
        qh = q_ref[0, h * GROUP:(h + 1) * GROUP, :]
        kn = kn_ref[0, h:h + 1, :]
        vn = vn_ref[0, h:h + 1, :]
        sink = sink_ref[h * GROUP:(h + 1) * GROUP, :]
        s_buf = lax.dot_general(qh.astype(BF16), kb[:, lo:hi].astype(BF16), (((1,), (1,)), ((), ())),
                                preferred_element_type=F32) * scale
        s_new = jnp.sum(qh * kn, axis=-1, keepdims=True) * scale
        m = jnp.maximum(jnp.maximum(jnp.max(s_buf, axis=-1, keepdims=True), s_new), sink)
        e_buf = jnp.exp(s_buf - m)
        e_new = jnp.exp(s_new - m)
        denom = jnp.sum(e_buf, axis=-1, keepdims=True) + e_new + jnp.exp(sink - m)
        o = jnp.dot(e_buf.astype(BF16), vb[:, lo:hi].astype(BF16), preferred_element_type=F32)
        o = o + e_new * vn
        o_ref[0, h * GROUP:(h + 1) * GROUP, :] = (o / denom).astype(o_ref.dtype)
    ko_ref[0, 0:W - 1, :] = kb[1:W, :]
    ko_ref[0, W - 1:W, :] = knf_ref[0]
    vo_ref[0, 0:W - 1, :] = vb[1:W, :]
    vo_ref[0, W - 1:W, :] = vnf_ref[0]


def swa_sample(q3, kn3, vn3, k_buf, v_buf, sinks):
    B = q3.shape[0]
    W, d = WINDOW, ATT_HEAD_DIM
    head_spec = lambda n: pl.BlockSpec((1, n, d), lambda b: (b, 0, 0))
    flat_spec = pl.BlockSpec((1, 1, KV_WIDTH), lambda b: (b, 0, 0))
    buf_spec = pl.BlockSpec((1, W, KV_WIDTH), lambda b: (b, 0, 0))
    return pl.pallas_call(
        _swa_sample_body,
        grid=(B,),
        in_specs=[head_spec(N_HEADS), head_spec(N_KV_HEADS), head_spec(N_KV_HEADS), flat_spec, flat_spec,
                  buf_spec, buf_spec, pl.BlockSpec((N_HEADS, 1), lambda b: (0, 0))],
        out_specs=[head_spec(N_HEADS), buf_spec, buf_spec],
        out_shape=[jax.ShapeDtypeStruct((B, N_HEADS, d), BF16),
                   jax.ShapeDtypeStruct((B, W, KV_WIDTH), F32),
                   jax.ShapeDtypeStruct((B, W, KV_WIDTH), F32)],
        compiler_params=_params("parallel"),
        name="swa_sample",
    )(q3, kn3, vn3, kn3.reshape(B, 1, KV_WIDTH), vn3.reshape(B, 1, KV_WIDTH), k_buf, v_buf,
      sinks.reshape(N_HEADS, 1))


def _shared_up_body(x_ref, wg_ref, wu_ref, h_ref):
    x = x_ref[...]
    g = jnp.dot(x, wg_ref[0].astype(BF16), preferred_element_type=F32)
    u = jnp.dot(x, wu_ref[0].astype(BF16), preferred_element_type=F32)
    h_ref[...] = (_silu(g) * u).astype(h_ref.dtype)


def _shared_down_body(h_ref, wd_ref, y_ref):
    y_ref[...] = jnp.dot(h_ref[...], wd_ref[0].astype(BF16), preferred_element_type=F32)


def shared_ffn(xb, w_gate, w_up, w_down, layer, tm=512):
    M, D = xb.shape
    FF = w_gate.shape[-1]
    tm = min(tm, M)
    wsel = lambda i: (layer, 0, 0)
    h = pl.pallas_call(
        _shared_up_body,
        grid=(M // tm,),
        in_specs=[pl.BlockSpec((tm, D), lambda i: (i, 0)),
                  pl.BlockSpec((1, D, FF), wsel), pl.BlockSpec((1, D, FF), wsel)],
        out_specs=pl.BlockSpec((tm, FF), lambda i: (i, 0)),
        out_shape=jax.ShapeDtypeStruct((M, FF), BF16),
        compiler_params=_params("parallel"),
        name="shared_up",
    )(xb, w_gate, w_up)
    return pl.pallas_call(
        _shared_down_body,
        grid=(M // tm,),
        in_specs=[pl.BlockSpec((tm, FF), lambda i: (i, 0)), pl.BlockSpec((1, FF, D), wsel)],
        out_specs=pl.BlockSpec((tm, D), lambda i: (i, 0)),
        out_shape=jax.ShapeDtypeStruct((M, D), F32),
        compiler_params=_params("parallel"),
        name="shared_down",
    )(h, w_down)


def _expert_changed(be_ref, i):
    return (i == 0) | (be_ref[i] != be_ref[jnp.maximum(i - 1, 0)])


GATHER_SLOTS = 3
DMA_QUEUES = 2
WEIGHT_DMA_PARTS = 8


def _stream_expert_weights(be_ref, nx_ref, i, w_hbm, stage, wsem, w_bf, layer):
    rows = stage.shape[1] // WEIGHT_DMA_PARTS

    def start(e, j):
        for c in range(WEIGHT_DMA_PARTS):
            part = pl.ds(c * rows, rows)
            pltpu.make_async_copy(w_hbm[j].at[layer, e, part], stage.at[j, part],
                                  wsem.at[j]).start(priority=c % DMA_QUEUES)

    def wait(e, j):
        pltpu.make_async_copy(w_hbm[j].at[layer, e], stage.at[j], wsem.at[j]).wait()

    @pl.when(i == 0)
    def _():
        for j in range(len(w_hbm)):
            start(be_ref[0], j)

    @pl.when(_expert_changed(be_ref, i))
    def _():
        e, nxt = be_ref[i], nx_ref[i]
        for j in range(len(w_hbm)):
            wait(e, j)
            w_bf[j][...] = stage[j].astype(BF16)

            @pl.when(nxt != e)
            def _():
                start(nxt, j)


def _moe_up_body(be_ref, nx_ref, na_ref, tok_ref, x_hbm, wg_hbm, wu_hbm, h_ref, xbuf, sem, x_lhs, stage, wsem,
                 wg_bf, wu_bf, *, tm, layer):
    i = pl.program_id(0)
    na = na_ref[0]
    last_blk = pl.num_programs(0) - 1
    half = wg_bf.shape[0] // 2

    def issue(blk, slot):
        base = jnp.minimum(blk, last_blk) * tm
        for r in range(tm):
            src = pl.multiple_of(tok_ref[base + r], TOK_ROWS)
            pltpu.make_async_copy(x_hbm.at[pl.ds(src, TOK_ROWS)],
                                  xbuf.at[slot, pl.ds(r * TOK_ROWS, TOK_ROWS)],
                                  sem.at[slot]).start(priority=r % DMA_QUEUES)

    def wait(slot):
        pltpu.make_async_copy(x_hbm.at[pl.ds(0, tm * TOK_ROWS)], xbuf.at[slot], sem.at[slot]).wait()

    @pl.when(i == 0)
    def _():
        issue(0, 0)
        issue(1, 1)

    @pl.when(i >= na)
    def _():
        h_ref[...] = jnp.zeros_like(h_ref)

    @pl.when(i < na)
    def _():
        slot = i % GATHER_SLOTS
        _stream_expert_weights(be_ref, nx_ref, i, (wg_hbm, wu_hbm), stage, wsem, (wg_bf, wu_bf), layer)
        wait(slot)
        for s in range(TOK_ROWS):
            lo, hi = _unpack_pair(_load_token_rows(xbuf.at[slot], s, tm))
            x_lhs[:, s * LANES:(s + 1) * LANES] = lo.astype(BF16)
            x_lhs[:, half + s * LANES:half + (s + 1) * LANES] = hi.astype(BF16)
        issue(i + 2, (i + 2) % GATHER_SLOTS)
        x = x_lhs[...]
        g = jnp.dot(x, wg_bf[...], preferred_element_type=F32)
        u = jnp.dot(x, wu_bf[...], preferred_element_type=F32)
        h_ref[...] = (_silu(g) * u).astype(h_ref.dtype)

        @pl.when(i + 1 == na)
        def _():
            wait((i + 1) % GATHER_SLOTS)
            wait((i + 2) % GATHER_SLOTS)


def _moe_down_body(be_ref, nx_ref, na_ref, dst_ref, h_ref, wd_hbm, y_hbm, ybuf, sem, stage, wsem, wd_bf, *, tm,
                   layer):
    i = pl.program_id(0)
    na = na_ref[0]

    def wait(slot):
        pltpu.make_async_copy(ybuf.at[slot], y_hbm.at[pl.ds(0, tm * TOK_ROWS)], sem.at[slot]).wait()

    def scatter(blk, slot):
        base = blk * tm
        for r in range(tm):
            dst = pl.multiple_of(dst_ref[base + r], TOK_ROWS)
            pltpu.make_async_copy(ybuf.at[slot, pl.ds(r * TOK_ROWS, TOK_ROWS)],
                                  y_hbm.at[pl.ds(dst, TOK_ROWS)], sem.at[slot]).start(priority=r % DMA_QUEUES)

    def compute(slot):
        y = jnp.dot(h_ref[...], wd_bf[...], preferred_element_type=F32)
        _store_token_rows(ybuf.at[slot], y, tm)

    @pl.when(i == 0)
    def _():
        plane = y_hbm.shape[0] // TOP_K
        spare = 2 * tm // TOP_K * TOK_ROWS
        ybuf[0] = jnp.zeros(ybuf.shape[1:], ybuf.dtype)
        for k in range(TOP_K):
            fill = pltpu.make_async_copy(ybuf.at[0, pl.ds(0, spare)],
                                         y_hbm.at[pl.ds((k + 1) * plane - spare, spare)], sem.at[0])
            fill.start()
            fill.wait()

    @pl.when(i < na)
    def _():
        _stream_expert_weights(be_ref, nx_ref, i, (wd_hbm,), stage, wsem, (wd_bf,), layer)

        @pl.when(i == 0)
        def _():
            compute(0)

        @pl.when(i > 0)
        def _():
            slot = i % 2

            @pl.when(i >= 2)
            def _():
                wait(slot)

            scatter(i - 1, 1 - slot)
            compute(slot)

    @pl.when(i == pl.num_programs(0) - 1)
    def _():
        last = na - 1

        @pl.when(na >= 2)
        def _():
            wait(na % 2)

        scatter(last, last % 2)
        wait(last % 2)


def routed_ffn(x_packed, w_gate, w_up, w_down, layer, plan, tm):
    slot_token, slot_dest, block_e, next_e, n_active = plan
    T = x_packed.shape[0] // TOK_ROWS
    NB = block_e.shape[0]
    D, FF = w_gate.shape[-2], w_gate.shape[-1]
    t_pad = T + 2 * tm // TOP_K
    buf = lambda n: pltpu.VMEM((n, tm * TOK_ROWS, LANES), jnp.uint32)
    row = lambda i, be, nx, na, ix: (jnp.minimum(i, na[0] - 1), 0)
    hbm = pl.BlockSpec(memory_space=pl.ANY)
    params = pltpu.CompilerParams(dimension_semantics=("arbitrary",), vmem_limit_bytes=V7X_VMEM_LIMIT)
    h = pl.pallas_call(
        functools.partial(_moe_up_body, tm=tm, layer=layer),
        grid_spec=pltpu.PrefetchScalarGridSpec(
            num_scalar_prefetch=4, grid=(NB,),
            in_specs=[hbm, hbm, hbm],
            out_specs=pl.BlockSpec((tm, FF), lambda i, be, nx, nv, ix: (i, 0)),
            scratch_shapes=[buf(GATHER_SLOTS), pltpu.SemaphoreType.DMA((GATHER_SLOTS,)),
                            pltpu.VMEM((tm, D), BF16),
                            pltpu.VMEM((2, D, FF), F32), pltpu.SemaphoreType.DMA((2,)),
                            pltpu.VMEM((D, FF), BF16), pltpu.VMEM((D, FF), BF16)]),
        out_shape=jax.ShapeDtypeStruct((NB * tm, FF), BF16),
        compiler_params=params,
        name="moe_up",
    )(block_e, next_e, n_active, slot_token * TOK_ROWS, x_packed, w_gate, w_up)
    y = pl.pallas_call(
        functools.partial(_moe_down_body, tm=tm, layer=layer),
        grid_spec=pltpu.PrefetchScalarGridSpec(
            num_scalar_prefetch=4, grid=(NB,),
            in_specs=[pl.BlockSpec((tm, FF), row), hbm],
            out_specs=hbm,
            scratch_shapes=[buf(2), pltpu.SemaphoreType.DMA((2,)),
                            pltpu.VMEM((1, FF, D), F32), pltpu.SemaphoreType.DMA((1,)),
                            pltpu.VMEM((FF, D), BF16)]),
        out_shape=jax.ShapeDtypeStruct((TOP_K * t_pad * TOK_ROWS, LANES), jnp.uint32),
        compiler_params=params,
        name="moe_down",
    )(block_e, next_e, n_active, slot_dest * TOK_ROWS, h, w_down)
    return y.reshape(TOP_K, t_pad * TOK_ROWS, LANES)


SLOT_MAP_CHUNK = 2048
SLOT_MAP_UNROLL = 32


def _slot_map_body(dest_ref, src_ref, *, clear_chunk):
    phase, i = pl.program_id(0), pl.program_id(1)

    @pl.when(phase == 0)
    def _():
        def clear(j, carry):
            src_ref[i * clear_chunk + j] = -1
            return carry

        lax.fori_loop(0, clear_chunk, clear, 0, unroll=SLOT_MAP_UNROLL)

    @pl.when(phase == 1)
    def _():
        def place(a, carry):
            src_ref[dest_ref[a]] = i * SLOT_MAP_CHUNK + a
            return carry

        lax.fori_loop(0, SLOT_MAP_CHUNK, place, 0, unroll=SLOT_MAP_UNROLL)


def _slot_map(dest, n_slots):
    A = dest.shape[0]
    n_chunks = -(-A // SLOT_MAP_CHUNK)
    clear_chunk = -(-(n_slots + 1) // (SLOT_MAP_UNROLL * n_chunks)) * SLOT_MAP_UNROLL
    dest = jnp.concatenate([dest, jnp.full((n_chunks * SLOT_MAP_CHUNK - A,), n_slots, jnp.int32)])
    src = pl.pallas_call(
        functools.partial(_slot_map_body, clear_chunk=clear_chunk),
        grid=(2, n_chunks),
        in_specs=[pl.BlockSpec((SLOT_MAP_CHUNK,), lambda ph, i: (i * ph,), memory_space=pltpu.SMEM)],
        out_specs=pl.BlockSpec(memory_space=pltpu.SMEM),
        out_shape=jax.ShapeDtypeStruct((n_chunks * clear_chunk,), jnp.int32),
        compiler_params=pltpu.CompilerParams(dimension_semantics=("arbitrary", "arbitrary")),
        name="slot_map",
    )(dest)
    return src[:n_slots]


def _dispatch_plan(idx, tm):
    T = idx.shape[0]
    A = T * TOP_K
    NB = -(-A // tm) + N_EXPERTS
    spare_t = 2 * tm // TOP_K
    t_pad = T + spare_t
    idx = idx.astype(jnp.int32)
    onehot = idx[:, :, None] == jnp.arange(N_EXPERTS, dtype=jnp.int32)[None, None, :]
    sel = onehot.any(axis=1).astype(jnp.int32)
    incl = jnp.cumsum(sel, axis=0)
    counts = incl[-1]
    padded = (counts + tm - 1) // tm * tm
    pad_end = jnp.cumsum(padded)
    pad_start = pad_end - padded
    n_active = (pad_end[-1] // tm).astype(jnp.int32)
    blk = jnp.arange(NB, dtype=jnp.int32)
    block_e = jnp.minimum(jnp.searchsorted(pad_end, jnp.minimum(blk, n_active - 1) * tm, side='right'),
                          N_EXPERTS - 1).astype(jnp.int32)
    e_ids = jnp.arange(N_EXPERTS, dtype=jnp.int32)
    later_used = lax.cummin(jnp.where(counts > 0, e_ids, N_EXPERTS), reverse=True)
    next_used = jnp.concatenate([later_used[1:], jnp.full((1,), N_EXPERTS, jnp.int32)])
    next_e = jnp.where(next_used[block_e] < N_EXPERTS, next_used[block_e], block_e).astype(jnp.int32)
    slot_of = (incl - sel + pad_start[None, :])[:, None, :]
    dest = jnp.sum(jnp.where(onehot, slot_of, 0), axis=-1).reshape(A)
    slot_src = _slot_map(dest, NB * tm)
    slot = jnp.arange(NB * tm, dtype=jnp.int32)
    valid = slot_src >= 0
    slot_token = jnp.where(valid, slot_src // TOP_K, 0)
    s2 = (slot // tm % 2) * tm + slot % tm
    spare = (s2 // spare_t) * t_pad + T + s2 % spare_t
    slot_dest = jnp.where(valid, (slot_src % TOP_K) * t_pad + slot_src // TOP_K, spare)
    return slot_token, slot_dest, block_e, next_e, n_active.reshape(1)


def moe_block(x_parts, xb_parts, xp_parts, p, layer):
    w_t = jnp.swapaxes(p['router_w'], 1, 2)
    bias_col = p['router_bias'][:, :, None]
    routed = [route(x, w_t, bias_col, layer) for x in x_parts]
    idx = jnp.concatenate([r[0] for r in routed], axis=1).T
    wts = jnp.concatenate([r[1] for r in routed], axis=1).T
    plan = _dispatch_plan(idx, MOE_TM)
    y = routed_ffn(jnp.concatenate(xp_parts, axis=0), p['exp_w_gate'], p['exp_w_up'], p['exp_w_down'],
                   layer, plan, MOE_TM)
    shared = [shared_ffn(xb, p['shared_w_gate'], p['shared_w_up'], p['shared_w_down'], layer)
              for xb in xb_parts]
    return y, wts, shared


def _rope_tables(pos, half):
    inv_freq = ROPE_THETA ** (-jnp.arange(half, dtype=F32) / half)
    ang = pos.astype(F32)[:, None] * inv_freq[None, :]
    return jnp.cos(ang), jnp.sin(ang)


def _even_prompt(x_bf, B, L, p, i):
    proj = matmul(x_bf, p['w_in_even'][i].astype(BF16), name="in_proj_even")
    proj3 = proj.reshape(B, L, -1)
    cos, sin = _rope_tables(jnp.arange(L, dtype=jnp.int32), RET_HEAD_DIM // 2)
    conv_out, conv_new = conv_prompt(proj3, p['conv_w'][i], p['conv_b'][i], p['conv_ln_g'][i], p['conv_ln_b'][i])
    ret_out, s_new = retention_prompt(proj3, cos, sin, p['ret_gn_g'][i], p['ret_gn_b'][i])
    mixed = jnp.concatenate([conv_out, ret_out], axis=-1).reshape(B * L, -1)
    return matmul(mixed, p['w_out_even'][i].astype(BF16), name="out_proj_even"), conv_new, s_new


def _even_sample(x_bf, conv_buf, ret_state, p, i):
    B = x_bf.shape[0]
    H, d = RET_HEADS, RET_HEAD_DIM
    proj = matmul(x_bf, p['w_in_even'][i].astype(BF16), name="in_proj_even")
    conv_out, conv_new = conv_sample(proj, conv_buf, p['conv_w'][i], p['conv_b'][i],
                                     p['conv_ln_g'][i], p['conv_ln_b'][i])
    o2 = 2 * CONV_CH
    q_col = proj[:, o2:o2 + RET_WIDTH].reshape(B, H, d, 1)
    k_col = proj[:, o2 + RET_WIDTH:o2 + 2 * RET_WIDTH].reshape(B, H, d, 1)
    v_row = proj[:, o2 + 2 * RET_WIDTH:o2 + 3 * RET_WIDTH].reshape(B, 1, RET_WIDTH)
    g_row = proj[:, o2 + 3 * RET_WIDTH:].reshape(B, 1, RET_WIDTH)
    cos, sin = _rope_tables(jnp.full((1,), PAST_LEN, jnp.int32), d // 2)
    ret_out, s_new = retention_sample(q_col, k_col, v_row, g_row, ret_state, cos.reshape(d // 2, 1),
                                      sin.reshape(d // 2, 1), p['ret_gn_g'][i], p['ret_gn_b'][i])
    mixed = jnp.concatenate([conv_out, ret_out.reshape(B, RET_WIDTH)], axis=-1)
    return matmul(mixed, p['w_out_even'][i].astype(BF16), name="out_proj_even"), conv_new, s_new


def _att_rope_tables(pos, rows):
    cos, sin = _rope_tables(pos, ATT_HEAD_DIM // 2)
    cos = jnp.tile(cos, (rows // cos.shape[0], LANES // cos.shape[1]))
    sin = jnp.tile(sin, (rows // sin.shape[0], LANES // sin.shape[1]))
    return cos, sin


def _odd_prompt(x_bf, B, L, p, i):
    cos, sin = _att_rope_tables(jnp.arange(L, dtype=jnp.int32), L)
    qkv = matmul(x_bf, p['w_qkv_odd'][i].astype(BF16), bias=p['b_qkv_odd'][i],
                 rope=(cos, sin, Q_WIDTH + KV_WIDTH), name="qkv_proj_odd")
    qkv3 = qkv.reshape(B, L, -1)
    o = swa_prompt(qkv3, p['sinks'][i])
    out = matmul(o.reshape(B * L, Q_WIDTH), p['w_out_odd'][i].astype(BF16), bias=p['b_out_odd'][i],
                 name="out_proj_odd")
    k_new = qkv3[:, L - WINDOW:, Q_WIDTH:Q_WIDTH + KV_WIDTH].reshape(B, WINDOW, N_KV_HEADS, ATT_HEAD_DIM)
    v_new = qkv3[:, L - WINDOW:, Q_WIDTH + KV_WIDTH:].reshape(B, WINDOW, N_KV_HEADS, ATT_HEAD_DIM)
    return out, k_new, v_new


def _odd_sample(x_bf, k_buf, v_buf, p, i):
    B = x_bf.shape[0]
    cos, sin = _att_rope_tables(jnp.full((1,), PAST_LEN, jnp.int32), B)
    qkv = matmul(x_bf, p['w_qkv_odd'][i].astype(BF16), bias=p['b_qkv_odd'][i],
                 rope=(cos, sin, Q_WIDTH + KV_WIDTH), name="qkv_proj_odd")
    q3 = qkv[:, :Q_WIDTH].reshape(B, N_HEADS, ATT_HEAD_DIM)
    kn3 = qkv[:, Q_WIDTH:Q_WIDTH + KV_WIDTH].reshape(B, N_KV_HEADS, ATT_HEAD_DIM)
    vn3 = qkv[:, Q_WIDTH + KV_WIDTH:].reshape(B, N_KV_HEADS, ATT_HEAD_DIM)
    o, k_new, v_new = swa_sample(q3, kn3, vn3, k_buf.reshape(B, WINDOW, KV_WIDTH),
                                 v_buf.reshape(B, WINDOW, KV_WIDTH), p['sinks'][i])
    out = matmul(o.reshape(B, Q_WIDTH), p['w_out_odd'][i].astype(BF16), bias=p['b_out_odd'][i],
                 name="out_proj_odd")
    shape = (B, WINDOW, N_KV_HEADS, ATT_HEAD_DIM)
    return out, k_new.reshape(shape), v_new.reshape(shape)


def kernel(x_prompt, x_sample, state_conv, state_ret, state_win_k, state_win_v,
           w_in_even, conv_w, conv_b, conv_ln_g, conv_ln_b, ret_gn_g, ret_gn_b, w_out_even,
           w_qkv_odd, b_qkv_odd, sinks, w_out_odd, b_out_odd, ln_g, ln_b,
           router_w, router_bias, exp_w_gate, exp_w_up, exp_w_down,
           shared_w_gate, shared_w_up, shared_w_down):
    p = dict(w_in_even=w_in_even, conv_w=conv_w, conv_b=conv_b, conv_ln_g=conv_ln_g, conv_ln_b=conv_ln_b,
             ret_gn_g=ret_gn_g, ret_gn_b=ret_gn_b, w_out_even=w_out_even,
             w_qkv_odd=w_qkv_odd, b_qkv_odd=b_qkv_odd, sinks=sinks, w_out_odd=w_out_odd, b_out_odd=b_out_odd,
             ln_g=ln_g, ln_b=ln_b, router_w=router_w, router_bias=router_bias,
             exp_w_gate=exp_w_gate, exp_w_up=exp_w_up, exp_w_down=exp_w_down,
             shared_w_gate=shared_w_gate, shared_w_up=shared_w_up, shared_w_down=shared_w_down)
    B, L, D = x_prompt.shape
    Bs = x_sample.shape[0]
    Tp = B * L
    xp = x_prompt.reshape(Tp, D)
    xs = x_sample.reshape(Bs, D)
    xp_bf, xs_bf = xp.astype(BF16), xs.astype(BF16)
    conv_p, conv_s, ret_p, ret_s, wk_p, wk_s, wv_p, wv_s = [], [], [], [], [], [], [], []
    for layer in range(DEPTH):
        i = layer // 2
        if layer % 2 == 0:
            hp, c_new, r_new = _even_prompt(xp_bf, B, L, p, i)
            conv_p.append(c_new)
            ret_p.append(r_new)
            hs, c_new, r_new = _even_sample(xs_bf, state_conv[i], state_ret[i], p, i)
            conv_s.append(c_new)
            ret_s.append(r_new)
        else:
            hp, k_new, v_new = _odd_prompt(xp_bf, B, L, p, i)
            wk_p.append(k_new)
            wv_p.append(v_new)
            hs, k_new, v_new = _odd_sample(xs_bf, state_win_k[i], state_win_v[i], p, i)
            wk_s.append(k_new)
            wv_s.append(v_new)
        xp, xp_bf, xp_pk = deepnorm(xp, [(hp, 0)], ln_g[layer, 0], ln_b[layer, 0], name="ln_mixer_prompt")
        xs, xs_bf, xs_pk = deepnorm(xs, [(hs, 0)], ln_g[layer, 0], ln_b[layer, 0], name="ln_mixer_sample")
        y, wts, shared = moe_block([xp, xs], [xp_bf, xs_bf], [xp_pk, xs_pk], p, layer)
        xp, xp_bf, _ = deepnorm(xp, [(shared[0], 0)], ln_g[layer, 1], ln_b[layer, 1], combine=(y, wts, 0),
                                tm=64, name="ln_moe_prompt")
        xs, xs_bf, _ = deepnorm(xs, [(shared[1], 0)], ln_g[layer, 1], ln_b[layer, 1], combine=(y, wts, Tp),
                                tm=32, name="ln_moe_sample")
    return (xp.reshape(B, L, D), xs.reshape(Bs, 1, D),
            jnp.stack(conv_p), jnp.stack(conv_s), jnp.stack(ret_p), jnp.stack(ret_s),
            jnp.stack(wk_p), jnp.stack(wk_s), jnp.stack(wv_p), jnp.stack(wv_s))
```

```python
import functools

import jax
import jax.numpy as jnp
import numpy as np
from jax import lax
from jax.experimental import pallas as pl
from jax.experimental.pallas import tpu as pltpu

F32 = jnp.float32
BF16 = jnp.bfloat16

D_MODEL = 4096
DEPTH = 2
PAST_LEN = 16384
CONV_CH = D_MODEL // 2
CONV_WIDTH = 31
CONV_HALO = 32
RET_HEADS = 8
RET_HEAD_DIM = 256
RET_WIDTH = RET_HEADS * RET_HEAD_DIM
RET_CHUNK = 128
RET_STEP_ROWS = 512
ATT_HEAD_DIM = 64
N_HEADS = 64
N_KV_HEADS = 8
GROUP = N_HEADS // N_KV_HEADS
WINDOW = 128
Q_WIDTH = N_HEADS * ATT_HEAD_DIM
KV_WIDTH = N_KV_HEADS * ATT_HEAD_DIM
ROPE_THETA = 10000.0
N_EXPERTS = 128
TOP_K = 8
N_GROUPS = 8
TOPK_GROUPS = 4
EXPERTS_PER_GROUP = N_EXPERTS // N_GROUPS
EXPERT_FF = 512
ROUTED_SCALE = 2.5
DN_ALPHA = (2.0 * DEPTH) ** 0.25
LN_EPS = 1e-5

LANES = 128
V7X_VMEM_LIMIT = 52 * 1024 * 1024
MOE_TM = 512


def _params(*sem):
    return pltpu.CompilerParams(dimension_semantics=sem, vmem_limit_bytes=V7X_VMEM_LIMIT)


def _silu(x):
    return x * jax.nn.sigmoid(x)


def _rot_half_64(x):
    n = x.shape[-1]
    lane = lax.broadcasted_iota(jnp.int32, x.shape, x.ndim - 1)
    fwd = pltpu.roll(x, n - ATT_HEAD_DIM // 2, x.ndim - 1)
    bwd = pltpu.roll(x, ATT_HEAD_DIM // 2, x.ndim - 1)
    return jnp.where(lane % ATT_HEAD_DIM < ATT_HEAD_DIM // 2, -fwd, bwd)


def _mm_body(*refs, has_bias, rope_tiles):
    x_ref, w_ref = refs[0], refs[1]
    pos = 2
    b_ref = None
    if has_bias:
        b_ref = refs[pos]
        pos += 1
    if rope_tiles:
        cos_ref, sin_ref = refs[pos], refs[pos + 1]
        pos += 2
    o_ref = refs[pos]
    acc = jnp.dot(x_ref[...].astype(BF16), w_ref[...].astype(BF16), preferred_element_type=F32)
    if has_bias:
        acc = acc + b_ref[...]
    if rope_tiles:
        reps = acc.shape[1] // LANES
        cos = jnp.concatenate([cos_ref[...]] * reps, axis=1)
        sin = jnp.concatenate([sin_ref[...]] * reps, axis=1)
        @pl.when(pl.program_id(1) < rope_tiles)
        def _():
            o_ref[...] = (acc * cos + _rot_half_64(acc) * sin).astype(o_ref.dtype)

        @pl.when(pl.program_id(1) >= rope_tiles)
        def _():
            o_ref[...] = acc.astype(o_ref.dtype)
    else:
        o_ref[...] = acc.astype(o_ref.dtype)


def matmul(x, w, bias=None, rope=None, out_dtype=F32, tm=1024, tn=512, name="matmul"):
    M, K = x.shape
    N = w.shape[1]
    tm = min(tm, M)
    tn = min(tn, N)
    assert M % tm == 0 and N % tn == 0
    in_specs = [pl.BlockSpec((tm, K), lambda i, j: (i, 0)),
                pl.BlockSpec((K, tn), lambda i, j: (0, j))]
    args = [x, w]
    if bias is not None:
        in_specs.append(pl.BlockSpec((1, tn), lambda i, j: (0, j)))
        args.append(bias.reshape(1, N))
    rope_tiles = 0
    if rope is not None:
        cos, sin, n_cols = rope
        assert n_cols % tn == 0 and cos.shape[0] % tm == 0
        rope_tiles = n_cols // tn
        nrb = cos.shape[0] // tm
        in_specs += [pl.BlockSpec((tm, LANES), lambda i, j: (i % nrb, 0))] * 2
        args += [cos, sin]
    return pl.pallas_call(
        functools.partial(_mm_body, has_bias=bias is not None, rope_tiles=rope_tiles),
        grid=(M // tm, N // tn),
        in_specs=in_specs,
        out_specs=pl.BlockSpec((tm, tn), lambda i, j: (i, j)),
        out_shape=jax.ShapeDtypeStruct((M, N), out_dtype),
        compiler_params=_params("parallel", "arbitrary"),
        name=name,
    )(*args)


def _split_bf16(x):
    hi = x.astype(BF16)
    lo = (x - hi.astype(F32)).astype(BF16)
    return hi, lo


def _first_argmax(x, row, n):
    m = jnp.max(x, axis=0, keepdims=True)
    i = jnp.min(jnp.where(x == m, row, n), axis=0, keepdims=True)
    return m, i


def _router_body(x_ref, wt_ref, bias_ref, idx_ref, wts_ref):
    nt = (((1,), (1,)), ((), ()))
    xh, xl = _split_bf16(x_ref[...])
    wh, wl = _split_bf16(wt_ref[0])
    logits = lax.dot_general(wh, xh, nt, preferred_element_type=F32)
    logits = logits + lax.dot_general(wh, xl, nt, preferred_element_type=F32)
    logits = logits + lax.dot_general(wl, xh, nt, preferred_element_type=F32)
    scores = jax.nn.sigmoid(logits)
    biased = scores + bias_ref[0]
    tm = scores.shape[1]
    neg = -jnp.inf
    row_g = lax.broadcasted_iota(jnp.int32, (EXPERTS_PER_GROUP, tm), 0)
    grp = []
    for g in range(N_GROUPS):
        xg = biased[g * EXPERTS_PER_GROUP:(g + 1) * EXPERTS_PER_GROUP]
        m1, i1 = _first_argmax(xg, row_g, EXPERTS_PER_GROUP)
        m2 = jnp.max(jnp.where(row_g == i1, neg, xg), axis=0, keepdims=True)
        grp.append(m1 + m2)
    gs = jnp.concatenate(grp, axis=0)
    row_n = lax.broadcasted_iota(jnp.int32, (N_GROUPS, tm), 0)
    chosen = jnp.zeros((N_GROUPS, tm), F32)
    for _ in range(TOPK_GROUPS):
        _, i = _first_argmax(gs, row_n, N_GROUPS)
        hit = row_n == i
        chosen = jnp.where(hit, 1.0, chosen)
        gs = jnp.where(hit, neg, gs)
    emask = jnp.concatenate([jnp.broadcast_to(chosen[g:g + 1], (EXPERTS_PER_GROUP, tm))
                             for g in range(N_GROUPS)], axis=0)
    cand = jnp.where(emask > 0.0, biased, neg)
    row_e = lax.broadcasted_iota(jnp.int32, (N_EXPERTS, tm), 0)
    ids, ws = [], []
    for _ in range(TOP_K):
        _, i = _first_argmax(cand, row_e, N_EXPERTS)
        hit = row_e == i
        ids.append(i)
        ws.append(jnp.sum(jnp.where(hit, scores, 0.0), axis=0, keepdims=True))
        cand = jnp.where(hit, neg, cand)
    w = jnp.concatenate(ws, axis=0)
    idx_ref[...] = jnp.concatenate(ids, axis=0)
    wts_ref[...] = w / jnp.sum(w, axis=0, keepdims=True) * ROUTED_SCALE


def route(x, w_t, bias_col, layer, tm=512):
    M, K = x.shape
    E = w_t.shape[1]
    tm = min(tm, M)
    return pl.pallas_call(
        _router_body,
        grid=(M // tm,),
        in_specs=[pl.BlockSpec((tm, K), lambda i: (i, 0)),
                  pl.BlockSpec((1, E, K), lambda i: (layer, 0, 0)),
                  pl.BlockSpec((1, E, 1), lambda i: (layer, 0, 0))],
        out_specs=[pl.BlockSpec((TOP_K, tm), lambda i: (0, i))] * 2,
        out_shape=[jax.ShapeDtypeStruct((TOP_K, M), jnp.int32), jax.ShapeDtypeStruct((TOP_K, M), F32)],
        compiler_params=_params("parallel"),
        name="router",
    )(x, w_t, bias_col)


def _layer_norm_rows(z, g, b):
    mu = jnp.mean(z, axis=-1, keepdims=True)
    zc = z - mu
    var = jnp.mean(zc * zc, axis=-1, keepdims=True)
    return zc * lax.rsqrt(var + LN_EPS) * g + b


def _unpack_pair(w):
    lo = lax.bitcast_convert_type(w << 16, F32)
    hi = lax.bitcast_convert_type(w & jnp.uint32(0xFFFF0000), F32)
    return lo, hi


def _pack_pair(lo, hi):
    lo_bits = lax.bitcast_convert_type(lo.astype(BF16).astype(F32), jnp.uint32) >> 16
    hi_bits = lax.bitcast_convert_type(hi.astype(BF16).astype(F32), jnp.uint32) & jnp.uint32(0xFFFF0000)
    return hi_bits | lo_bits


TOK_ROWS = D_MODEL // 2 // LANES


def _load_token_rows(ref, s, tm):
    return ref[pl.ds(s, tm, stride=TOK_ROWS), :]


def _store_token_rows(ref, y, tm):
    half = y.shape[1] // 2
    for s in range(TOK_ROWS):
        words = _pack_pair(y[:, s * LANES:(s + 1) * LANES], y[:, half + s * LANES:half + (s + 1) * LANES])
        ref[pl.ds(s, tm, stride=TOK_ROWS), :] = words


def _ln_body(*refs, n_h, combine):
    x_ref = refs[0]
    h_refs = refs[1:1 + n_h]
    pos = 1 + n_h
    if combine:
        y_ref, w_ref = refs[pos], refs[pos + 1]
        pos += 2
    g_ref, b_ref, o_ref, ob_ref, op_ref = refs[pos:]
    tm, d = x_ref.shape
    half = d // 2
    z = DN_ALPHA * x_ref[...]
    for h_ref in h_refs:
        z = z + h_ref[...]
    if combine:
        wk = [jnp.broadcast_to(w_ref[:, k:k + 1], (tm, LANES)) for k in range(TOP_K)]
        cols_lo, cols_hi = [], []
        for s in range(TOK_ROWS):
            acc_lo = z[:, s * LANES:(s + 1) * LANES]
            acc_hi = z[:, half + s * LANES:half + (s + 1) * LANES]
            for k in range(TOP_K):
                lo, hi = _unpack_pair(_load_token_rows(y_ref.at[k], s, tm))
                acc_lo = acc_lo + wk[k] * lo
                acc_hi = acc_hi + wk[k] * hi
            cols_lo.append(acc_lo)
            cols_hi.append(acc_hi)
        z = jnp.concatenate(cols_lo + cols_hi, axis=1)
    y = _layer_norm_rows(z, g_ref[...], b_ref[...])
    o_ref[...] = y
    ob_ref[...] = y.astype(BF16)
    _store_token_rows(op_ref, y, tm)


def deepnorm(x, hs, g, b, combine=None, tm=256, name="deepnorm"):
    M, D = x.shape
    tm = min(tm, M)
    assert M % tm == 0
    row_spec = lambda width, off: pl.BlockSpec((tm, width), functools.partial(lambda i, o: (i + o, 0), o=off // tm))
    in_specs = [row_spec(D, 0)]
    args = [x]
    for h, off in hs:
        assert off % tm == 0
        in_specs.append(row_spec(D, off))
        args.append(h)
    if combine is not None:
        y, wts, off = combine
        assert off % tm == 0
        y_spec = pl.BlockSpec((TOP_K, tm * TOK_ROWS, LANES),
                              functools.partial(lambda i, o: (0, i + o, 0), o=off // tm))
        in_specs += [y_spec, row_spec(TOP_K, off)]
        args += [y, wts]
    in_specs += [pl.BlockSpec((1, D), lambda i: (0, 0))] * 2
    args += [g.reshape(1, D), b.reshape(1, D)]
    return pl.pallas_call(
        functools.partial(_ln_body, n_h=len(hs), combine=combine is not None),
        grid=(M // tm,),
        in_specs=in_specs,
        out_specs=[row_spec(D, 0), row_spec(D, 0), pl.BlockSpec((tm * TOK_ROWS, LANES), lambda i: (i, 0))],
        out_shape=[jax.ShapeDtypeStruct((M, D), F32), jax.ShapeDtypeStruct((M, D), BF16),
                   jax.ShapeDtypeStruct((M * TOK_ROWS, LANES), jnp.uint32)],
        compiler_params=_params("parallel"),
        name=name,
    )(*args)


def _conv_prompt_body(a_ref, g_ref, pa_ref, pg_ref, w_ref, bdw_ref, lng_ref, lnb_ref,
                      o_ref, st_ref, full_ref, y_ref, *, tl):
    i = pl.program_id(1)
    u = a_ref[0] * jax.nn.sigmoid(g_ref[0])
    up = pa_ref[0] * jax.nn.sigmoid(pg_ref[0])
    up = jnp.where(i > 0, up, 0.0)
    full_ref[0:CONV_HALO, :] = up
    full_ref[CONV_HALO:CONV_HALO + tl, :] = u
    base = CONV_HALO - (CONV_WIDTH - 1)
    n_col = u.shape[1] // LANES

    def col_step(c, carry):
        off = pl.multiple_of(c * LANES, LANES)
        acc = jnp.zeros((tl, LANES), F32)
        for j in range(CONV_WIDTH):
            acc = acc + w_ref[j:j + 1, pl.ds(off, LANES)] * full_ref[base + j:base + j + tl, pl.ds(off, LANES)]
        y_ref[:, pl.ds(off, LANES)] = acc
        return carry

    lax.fori_loop(0, n_col, col_step, 0)
    y = _layer_norm_rows(y_ref[...] + bdw_ref[...], lng_ref[...], lnb_ref[...])
    o_ref[0] = _silu(y).astype(o_ref.dtype)

    @pl.when(i == pl.num_programs(1) - 1)
    def _():
        st_ref[0] = full_ref[CONV_HALO + tl - (CONV_WIDTH - 1):CONV_HALO + tl, :]


def conv_prompt(proj3, w_dw, b_dw, ln_g, ln_b, tl=128):
    B, L, _ = proj3.shape
    C = CONV_CH
    nh = tl // CONV_HALO
    vec = lambda v: v.reshape(1, C)
    return pl.pallas_call(
        functools.partial(_conv_prompt_body, tl=tl),
        grid=(B, L // tl),
        in_specs=[
            pl.BlockSpec((1, tl, C), lambda b, i: (b, i, 0)),
            pl.BlockSpec((1, tl, C), lambda b, i: (b, i, 1)),
            pl.BlockSpec((1, CONV_HALO, C), lambda b, i: (b, jnp.maximum(i * nh - 1, 0), 0)),
            pl.BlockSpec((1, CONV_HALO, C), lambda b, i: (b, jnp.maximum(i * nh - 1, 0), 1)),
            pl.BlockSpec((CONV_WIDTH, C), lambda b, i: (0, 0)),
            pl.BlockSpec((1, C), lambda b, i: (0, 0)),
            pl.BlockSpec((1, C), lambda b, i: (0, 0)),
            pl.BlockSpec((1, C), lambda b, i: (0, 0)),
        ],
        out_specs=[
            pl.BlockSpec((1, tl, C), lambda b, i: (b, i, 0)),
            pl.BlockSpec((1, CONV_WIDTH - 1, C), lambda b, i: (b, 0, 0)),
        ],
        out_shape=[jax.ShapeDtypeStruct((B, L, C), BF16),
                   jax.ShapeDtypeStruct((B, CONV_WIDTH - 1, C), F32)],
        scratch_shapes=[pltpu.VMEM((CONV_HALO + tl, C), F32), pltpu.VMEM((tl, C), F32)],
        compiler_params=_params("parallel", "arbitrary"),
        name="conv_prompt",
    )(proj3, proj3, proj3, proj3, w_dw, vec(b_dw), vec(ln_g), vec(ln_b))


def _conv_sample_body(a_ref, g_ref, buf_ref, w_ref, bdw_ref, lng_ref, lnb_ref, o_ref, st_ref):
    u = a_ref[...] * jax.nn.sigmoid(g_ref[...])
    nb = CONV_WIDTH - 1
    acc = w_ref[nb:nb + 1, :] * u
    for j in range(nb):
        acc = acc + w_ref[j:j + 1, :] * buf_ref[:, j, :]
    y = _layer_norm_rows(acc + bdw_ref[...], lng_ref[...], lnb_ref[...])
    o_ref[...] = _silu(y).astype(o_ref.dtype)
    st_ref[:, 0:nb - 1, :] = buf_ref[:, 1:nb, :]
    st_ref[:, nb - 1, :] = u


def conv_sample(proj, buf, w_dw, b_dw, ln_g, ln_b, tb=8):
    B = proj.shape[0]
    C = CONV_CH
    nb = CONV_WIDTH - 1
    vec = lambda v: v.reshape(1, C)
    return pl.pallas_call(
        _conv_sample_body,
        grid=(B // tb,),
        in_specs=[
            pl.BlockSpec((tb, C), lambda b: (b, 0)),
            pl.BlockSpec((tb, C), lambda b: (b, 1)),
            pl.BlockSpec((tb, nb, C), lambda b: (b, 0, 0)),
            pl.BlockSpec((CONV_WIDTH, C), lambda b: (0, 0)),
            pl.BlockSpec((1, C), lambda b: (0, 0)),
            pl.BlockSpec((1, C), lambda b: (0, 0)),
            pl.BlockSpec((1, C), lambda b: (0, 0)),
        ],
        out_specs=[pl.BlockSpec((tb, C), lambda b: (b, 0)),
                   pl.BlockSpec((tb, nb, C), lambda b: (b, 0, 0))],
        out_shape=[jax.ShapeDtypeStruct((B, C), BF16), jax.ShapeDtypeStruct((B, nb, C), F32)],
        compiler_params=_params("parallel"),
        name="conv_sample",
    )(proj, proj, buf, w_dw, vec(b_dw), vec(ln_g), vec(ln_b))


def _ret_log_gamma():
    return np.log1p(-np.exp2(-5.0 - np.arange(RET_HEADS, dtype=np.float64)))


def _group_norm_gate(o, gate, gn_g, gn_b):
    mu = jnp.mean(o, axis=-1, keepdims=True)
    oc = o - mu
    var = jnp.mean(oc * oc, axis=-1, keepdims=True)
    return (oc * lax.rsqrt(var + LN_EPS) * gn_g + gn_b) * _silu(gate)


def _rope_256(x, cos, sin):
    half = RET_HEAD_DIM // 2
    x1, x2 = x[..., :half], x[..., half:]
    return jnp.concatenate([x1 * cos - x2 * sin, x2 * cos + x1 * sin], axis=-1)


def _ret_prompt_body(q_ref, k_ref, v_ref, g_ref, cos_ref, sin_ref, intra_ref, qd_ref, kd_ref, cd_ref,
                     gng_ref, gnb_ref, o_ref, s_ref):
    c = pl.program_id(2)

    @pl.when(c == 0)
    def _():
        s_ref[...] = jnp.zeros_like(s_ref)

    cw = RET_CHUNK
    for j in range(q_ref.shape[1] // cw):
        rows = slice(j * cw, (j + 1) * cw)
        cos, sin = cos_ref[rows, :], sin_ref[rows, :]
        q = _rope_256(q_ref[0, rows, :], cos, sin)
        k = _rope_256(k_ref[0, rows, :], cos, sin) * (RET_HEAD_DIM ** -0.5)
        vb = v_ref[0, rows, :].astype(BF16)
        s_prev = s_ref[0, 0]
        scores = lax.dot_general(q.astype(BF16), k.astype(BF16), (((1,), (1,)), ((), ())),
                                 preferred_element_type=F32) * intra_ref[0]
        o = jnp.dot(scores.astype(BF16), vb, preferred_element_type=F32)
        o = o + jnp.dot((q * qd_ref[0]).astype(BF16), s_prev.astype(BF16), preferred_element_type=F32)
        kdt = (k * kd_ref[0]).T.astype(BF16)
        kv = jnp.dot(kdt, vb, preferred_element_type=F32)
        s_ref[0, 0] = cd_ref[0] * s_prev + kv
        o_ref[0, rows, :] = _group_norm_gate(o, g_ref[0, rows, :], gng_ref[...], gnb_ref[...]).astype(o_ref.dtype)


def retention_prompt(proj3, cos, sin, gn_g, gn_b):
    B, L, _ = proj3.shape
    H, d, c = RET_HEADS, RET_HEAD_DIM, RET_CHUNK
    lg = _ret_log_gamma()
    idx = np.arange(c, dtype=np.float64)
    rel = idx[:, None] - idx[None, :]
    intra = np.where(rel[None] >= 0, np.exp(lg[:, None, None] * np.maximum(rel, 0.0)[None]), 0.0)
    ones = np.ones((1, 1, d))
    qd = np.exp(lg[:, None] * (idx + 1.0)[None, :])[:, :, None] * ones
    kd = np.exp(lg[:, None] * (c - 1.0 - idx)[None, :])[:, :, None] * ones
    cd = np.exp(lg * c)[:, None, None] * ones
    col0 = 2 * CONV_CH // d
    tl = min(RET_STEP_ROWS, L)
    assert L % tl == 0 and tl % c == 0
    head_spec = lambda off: pl.BlockSpec((1, tl, d), lambda b, h, n: (b, n, col0 + off * H + h))
    tab_spec = pl.BlockSpec((1, c, d), lambda b, h, n: (h, 0, 0))
    return pl.pallas_call(
        _ret_prompt_body,
        grid=(B, H, L // tl),
        in_specs=[
            head_spec(0), head_spec(1), head_spec(2), head_spec(3),
            pl.BlockSpec((tl, d // 2), lambda b, h, n: (n, 0)),
            pl.BlockSpec((tl, d // 2), lambda b, h, n: (n, 0)),
            pl.BlockSpec((1, c, c), lambda b, h, n: (h, 0, 0)),
            tab_spec, tab_spec,
            pl.BlockSpec((1, 1, d), lambda b, h, n: (h, 0, 0)),
            pl.BlockSpec((1, d), lambda b, h, n: (0, h)),
            pl.BlockSpec((1, d), lambda b, h, n: (0, h)),
        ],
        out_specs=[
            pl.BlockSpec((1, tl, d), lambda b, h, n: (b, n, h)),
            pl.BlockSpec((1, 1, d, d), lambda b, h, n: (b, h, 0, 0)),
        ],
        out_shape=[jax.ShapeDtypeStruct((B, L, H * d), BF16),
                   jax.ShapeDtypeStruct((B, H, d, d), F32)],
        compiler_params=_params("parallel", "parallel", "arbitrary"),
        name="retention_prompt",
    )(proj3, proj3, proj3, proj3, cos, sin,
      jnp.asarray(intra, F32), jnp.asarray(qd, F32), jnp.asarray(kd, F32), jnp.asarray(cd, F32),
      gn_g.reshape(1, H * d), gn_b.reshape(1, H * d))


def _ret_sample_body(qc_ref, kc_ref, v_ref, g_ref, s_ref, cosc_ref, sinc_ref, gam_ref, gng_ref, gnb_ref,
                     o_ref, so_ref):
    half = RET_HEAD_DIM // 2
    cos, sin = cosc_ref[...], sinc_ref[...]

    def rope_col(x):
        x1, x2 = x[:half], x[half:]
        return jnp.concatenate([x1 * cos - x2 * sin, x2 * cos + x1 * sin], axis=0)

    for h in range(RET_HEADS):
        lo, hi = h * RET_HEAD_DIM, (h + 1) * RET_HEAD_DIM
        q = rope_col(qc_ref[0, h])
        k = rope_col(kc_ref[0, h]) * (RET_HEAD_DIM ** -0.5)
        v = v_ref[0, :, lo:hi]
        s_new = gam_ref[h] * s_ref[0, h] + k * v
        so_ref[0, h] = s_new
        o = jnp.sum(q * s_new, axis=0, keepdims=True)
        o_ref[0, :, lo:hi] = _group_norm_gate(o, g_ref[0, :, lo:hi], gng_ref[:, lo:hi],
                                              gnb_ref[:, lo:hi]).astype(o_ref.dtype)


def retention_sample(q_col, k_col, v_row, g_row, state, cos_col, sin_col, gn_g, gn_b):
    B = state.shape[0]
    H, d = RET_HEADS, RET_HEAD_DIM
    gam = jnp.asarray(np.exp(_ret_log_gamma()), F32)
    col_spec = pl.BlockSpec((1, H, d, 1), lambda b: (b, 0, 0, 0))
    row_spec = pl.BlockSpec((1, 1, H * d), lambda b: (b, 0, 0))
    return pl.pallas_call(
        _ret_sample_body,
        grid=(B,),
        in_specs=[
            col_spec, col_spec, row_spec, row_spec,
            pl.BlockSpec((1, H, d, d), lambda b: (b, 0, 0, 0)),
            pl.BlockSpec((d // 2, 1), lambda b: (0, 0)),
            pl.BlockSpec((d // 2, 1), lambda b: (0, 0)),
            pl.BlockSpec(memory_space=pltpu.SMEM),
            pl.BlockSpec((1, H * d), lambda b: (0, 0)),
            pl.BlockSpec((1, H * d), lambda b: (0, 0)),
        ],
        out_specs=[row_spec, pl.BlockSpec((1, H, d, d), lambda b: (b, 0, 0, 0))],
        out_shape=[jax.ShapeDtypeStruct((B, 1, H * d), BF16), jax.ShapeDtypeStruct((B, H, d, d), F32)],
        compiler_params=_params("parallel"),
        name="retention_sample",
    )(q_col, k_col, v_row, g_row, state, cos_col, sin_col, gam, gn_g.reshape(1, H * d), gn_b.reshape(1, H * d))


def _pair_operand(x2, h, lane):
    col = x2[:, (h // 2) * LANES:(h // 2 + 1) * LANES]
    swapped = pltpu.roll(col, ATT_HEAD_DIM, 1)
    first, second = (col, swapped) if h % 2 == 0 else (swapped, col)
    return jnp.concatenate([jnp.where(lane < ATT_HEAD_DIM, first, 0.0),
                            jnp.where(lane >= ATT_HEAD_DIM, second, 0.0)], axis=0).astype(BF16)


def _swa_prompt_body(sink_ref, q_ref, kc_ref, kp_ref, vc_ref, vp_ref, o_ref):
    j = pl.program_id(1)
    W = WINDOW
    k2 = jnp.concatenate([kp_ref[0], kc_ref[0]], axis=0)
    v2 = jnp.concatenate([vp_ref[0], vc_ref[0]], axis=0)
    row = lax.broadcasted_iota(jnp.int32, (W, 2 * W), 0)
    col = lax.broadcasted_iota(jnp.int32, (W, 2 * W), 1)
    first_key = jnp.where(j > 0, 0, W)
    mask = (col >= jnp.maximum(row, first_key)) & (col <= W + row)
    lane_kv = lax.broadcasted_iota(jnp.int32, (2 * W, LANES), 1)
    lane_o = lax.broadcasted_iota(jnp.int32, (W, LANES), 1)
    scale = ATT_HEAD_DIM ** -0.5
    for h in range(N_KV_HEADS):
        kk = _pair_operand(k2, h, lane_kv)
        vv = _pair_operand(v2, h, lane_kv)
        for p in range(GROUP // 2):
            c0 = h * GROUP * ATT_HEAD_DIM + p * LANES
            qp = q_ref[0, :, c0:c0 + LANES].astype(BF16)
            s = lax.dot_general(qp, kk, (((1,), (1,)), ((), ())), preferred_element_type=F32) * scale
            probs, inv = [], []
            for t in range(2):
                sink = sink_ref[h * GROUP + 2 * p + t]
                st = jnp.where(mask, s[:, t * 2 * W:(t + 1) * 2 * W], -jnp.inf)
                m = jnp.maximum(jnp.max(st, axis=-1, keepdims=True), sink)
                e = jnp.exp(st - m)
                denom = jnp.sum(e, axis=-1, keepdims=True) + jnp.exp(sink - m)
                probs.append(e.astype(BF16))
                inv.append(1.0 / denom)
            o2 = jnp.dot(jnp.concatenate(probs, axis=1), vv, preferred_element_type=F32)
            o2 = o2 * jnp.where(lane_o < ATT_HEAD_DIM, inv[0], inv[1])
            o_ref[0, :, c0:c0 + LANES] = o2.astype(o_ref.dtype)


def swa_prompt(qkv3, sinks):
    B, L, _ = qkv3.shape
    W = WINDOW
    kcol = Q_WIDTH // KV_WIDTH
    return pl.pallas_call(
        _swa_prompt_body,
        grid_spec=pltpu.PrefetchScalarGridSpec(
            num_scalar_prefetch=0,
            grid=(B, L // W),
            in_specs=[
                pl.BlockSpec(memory_space=pltpu.SMEM),
                pl.BlockSpec((1, W, Q_WIDTH), lambda b, j: (b, j, 0)),
                pl.BlockSpec((1, W, KV_WIDTH), lambda b, j: (b, j, kcol)),
                pl.BlockSpec((1, W, KV_WIDTH), lambda b, j: (b, jnp.maximum(j - 1, 0), kcol)),
                pl.BlockSpec((1, W, KV_WIDTH), lambda b, j: (b, j, kcol + 1)),
                pl.BlockSpec((1, W, KV_WIDTH), lambda b, j: (b, jnp.maximum(j - 1, 0), kcol + 1)),
            ],
            out_specs=pl.BlockSpec((1, W, Q_WIDTH), lambda b, j: (b, j, 0)),
        ),
        out_shape=jax.ShapeDtypeStruct((B, L, Q_WIDTH), BF16),
        compiler_params=_params("parallel", "parallel"),
        name="swa_prompt",
    )(sinks, qkv3, qkv3, qkv3, qkv3, qkv3)


def _swa_sample_body(q_ref, kn_ref, vn_ref, knf_ref, vnf_ref, kb_ref, vb_ref, sink_ref, o_ref, ko_ref, vo_ref):
    W = WINDOW
    scale = ATT_HEAD_DIM ** -0.5
    kb = kb_ref[0]
    vb = vb_ref[0]
    for h in range(N_KV_HEADS):
        lo, hi = h * ATT_HEAD_DIM, (h + 1) * ATT_HEAD_DIM
        qh = q_ref[0, h * GROUP:(h + 1) * GROUP, :]
        kn = kn_ref[0, h:h + 1, :]
        vn = vn_ref[0, h:h + 1, :]
        sink = sink_ref[h * GROUP:(h + 1) * GROUP, :]
        s_buf = lax.dot_general(qh.astype(BF16), kb[:, lo:hi].astype(BF16), (((1,), (1,)), ((), ())),
                                preferred_element_type=F32) * scale
        s_new = jnp.sum(qh * kn, axis=-1, keepdims=True) * scale
        m = jnp.maximum(jnp.maximum(jnp.max(s_buf, axis=-1, keepdims=True), s_new), sink)
        e_buf = jnp.exp(s_buf - m)
        e_new = jnp.exp(s_new - m)
        denom = jnp.sum(e_buf, axis=-1, keepdims=True) + e_new + jnp.exp(sink - m)
        o = jnp.dot(e_buf.astype(BF16), vb[:, lo:hi].astype(BF16), preferred_element_type=F32)
        o = o + e_new * vn
        o_ref[0, h * GROUP:(h + 1) * GROUP, :] = (o / denom).astype(o_ref.dtype)
    ko_ref[0, 0:W - 1, :] = kb[1:W, :]
    ko_ref[0, W - 1:W, :] = knf_ref[0]
    vo_ref[0, 0:W - 1, :] = vb[1:W, :]
    vo_ref[0, W - 1:W, :] = vnf_ref[0]


def swa_sample(q3, kn3, vn3, k_buf, v_buf, sinks):
    B = q3.shape[0]
    W, d = WINDOW, ATT_HEAD_DIM
    head_spec = lambda n: pl.BlockSpec((1, n, d), lambda b: (b, 0, 0))
    flat_spec = pl.BlockSpec((1, 1, KV_WIDTH), lambda b: (b, 0, 0))
    buf_spec = pl.BlockSpec((1, W, KV_WIDTH), lambda b: (b, 0, 0))
    return pl.pallas_call(
        _swa_sample_body,
        grid=(B,),
        in_specs=[head_spec(N_HEADS), head_spec(N_KV_HEADS), head_spec(N_KV_HEADS), flat_spec, flat_spec,
                  buf_spec, buf_spec, pl.BlockSpec((N_HEADS, 1), lambda b: (0, 0))],
        out_specs=[head_spec(N_HEADS), buf_spec, buf_spec],
        out_shape=[jax.ShapeDtypeStruct((B, N_HEADS, d), BF16),
                   jax.ShapeDtypeStruct((B, W, KV_WIDTH), F32),
                   jax.ShapeDtypeStruct((B, W, KV_WIDTH), F32)],
        compiler_params=_params("parallel"),
        name="swa_sample",
    )(q3, kn3, vn3, kn3.reshape(B, 1, KV_WIDTH), vn3.reshape(B, 1, KV_WIDTH), k_buf, v_buf,
      sinks.reshape(N_HEADS, 1))


def _shared_up_body(x_ref, wg_ref, wu_ref, h_ref):
    x = x_ref[...]
    g = jnp.dot(x, wg_ref[0].astype(BF16), preferred_element_type=F32)
    u = jnp.dot(x, wu_ref[0].astype(BF16), preferred_element_type=F32)
    h_ref[...] = (_silu(g) * u).astype(h_ref.dtype)


def _shared_down_body(h_ref, wd_ref, y_ref):
    y_ref[...] = jnp.dot(h_ref[...], wd_ref[0].astype(BF16), preferred_element_type=F32)


def shared_ffn(xb, w_gate, w_up, w_down, layer, tm=512):
    M, D = xb.shape
    FF = w_gate.shape[-1]
    tm = min(tm, M)
    wsel = lambda i: (layer, 0, 0)
    h = pl.pallas_call(
        _shared_up_body,
        grid=(M // tm,),
        in_specs=[pl.BlockSpec((tm, D), lambda i: (i, 0)),
                  pl.BlockSpec((1, D, FF), wsel), pl.BlockSpec((1, D, FF), wsel)],
        out_specs=pl.BlockSpec((tm, FF), lambda i: (i, 0)),
        out_shape=jax.ShapeDtypeStruct((M, FF), BF16),
        compiler_params=_params("parallel"),
        name="shared_up",
    )(xb, w_gate, w_up)
    return pl.pallas_call(
        _shared_down_body,
        grid=(M // tm,),
        in_specs=[pl.BlockSpec((tm, FF), lambda i: (i, 0)), pl.BlockSpec((1, FF, D), wsel)],
        out_specs=pl.BlockSpec((tm, D), lambda i: (i, 0)),
        out_shape=jax.ShapeDtypeStruct((M, D), F32),
        compiler_params=_params("parallel"),
        name="shared_down",
    )(h, w_down)


def _expert_changed(be_ref, i):
    return (i == 0) | (be_ref[i] != be_ref[jnp.maximum(i - 1, 0)])


GATHER_SLOTS = 3
DMA_QUEUES = 2
WEIGHT_DMA_PARTS = 8


def _stream_expert_weights(be_ref, nx_ref, i, w_hbm, stage, wsem, w_bf, layer):
    rows = stage.shape[1] // WEIGHT_DMA_PARTS

    def start(e, j):
        for c in range(WEIGHT_DMA_PARTS):
            part = pl.ds(c * rows, rows)
            pltpu.make_async_copy(w_hbm[j].at[layer, e, part], stage.at[j, part],
                                  wsem.at[j]).start(priority=c % DMA_QUEUES)

    def wait(e, j):
        pltpu.make_async_copy(w_hbm[j].at[layer, e], stage.at[j], wsem.at[j]).wait()

    @pl.when(i == 0)
    def _():
        for j in range(len(w_hbm)):
            start(be_ref[0], j)

    @pl.when(_expert_changed(be_ref, i))
    def _():
        e, nxt = be_ref[i], nx_ref[i]
        for j in range(len(w_hbm)):
            wait(e, j)
            w_bf[j][...] = stage[j].astype(BF16)

            @pl.when(nxt != e)
            def _():
                start(nxt, j)


def _moe_up_body(be_ref, nx_ref, na_ref, tok_ref, x_hbm, wg_hbm, wu_hbm, h_ref, xbuf, sem, x_lhs, stage, wsem,
                 wg_bf, wu_bf, *, tm, layer):
    i = pl.program_id(0)
    na = na_ref[0]
    last_blk = pl.num_programs(0) - 1
    half = wg_bf.shape[0] // 2

    def issue(blk, slot):
        base = jnp.minimum(blk, last_blk) * tm
        for r in range(tm):
            src = pl.multiple_of(tok_ref[base + r], TOK_ROWS)
            pltpu.make_async_copy(x_hbm.at[pl.ds(src, TOK_ROWS)],
                                  xbuf.at[slot, pl.ds(r * TOK_ROWS, TOK_ROWS)],
                                  sem.at[slot]).start(priority=r % DMA_QUEUES)

    def wait(slot):
        pltpu.make_async_copy(x_hbm.at[pl.ds(0, tm * TOK_ROWS)], xbuf.at[slot], sem.at[slot]).wait()

    @pl.when(i == 0)
    def _():
        issue(0, 0)
        issue(1, 1)

    @pl.when(i >= na)
    def _():
        h_ref[...] = jnp.zeros_like(h_ref)

    @pl.when(i < na)
    def _():
        slot = i % GATHER_SLOTS
        _stream_expert_weights(be_ref, nx_ref, i, (wg_hbm, wu_hbm), stage, wsem, (wg_bf, wu_bf), layer)
        wait(slot)
        for s in range(TOK_ROWS):
            lo, hi = _unpack_pair(_load_token_rows(xbuf.at[slot], s, tm))
            x_lhs[:, s * LANES:(s + 1) * LANES] = lo.astype(BF16)
            x_lhs[:, half + s * LANES:half + (s + 1) * LANES] = hi.astype(BF16)
        issue(i + 2, (i + 2) % GATHER_SLOTS)
        x = x_lhs[...]
        g = jnp.dot(x, wg_bf[...], preferred_element_type=F32)
        u = jnp.dot(x, wu_bf[...], preferred_element_type=F32)
        h_ref[...] = (_silu(g) * u).astype(h_ref.dtype)

        @pl.when(i + 1 == na)
        def _():
            wait((i + 1) % GATHER_SLOTS)
            wait((i + 2) % GATHER_SLOTS)


def _moe_down_body(be_ref, nx_ref, na_ref, dst_ref, h_ref, wd_hbm, y_hbm, ybuf, sem, stage, wsem, wd_bf, *, tm,
                   layer):
    i = pl.program_id(0)
    na = na_ref[0]

    def wait(slot):
        pltpu.make_async_copy(ybuf.at[slot], y_hbm.at[pl.ds(0, tm * TOK_ROWS)], sem.at[slot]).wait()

    def scatter(blk, slot):
        base = blk * tm
        for r in range(tm):
            dst = pl.multiple_of(dst_ref[base + r], TOK_ROWS)
            pltpu.make_async_copy(ybuf.at[slot, pl.ds(r * TOK_ROWS, TOK_ROWS)],
                                  y_hbm.at[pl.ds(dst, TOK_ROWS)], sem.at[slot]).start(priority=r % DMA_QUEUES)

    def compute(slot):
        y = jnp.dot(h_ref[...], wd_bf[...], preferred_element_type=F32)
        _store_token_rows(ybuf.at[slot], y, tm)

    @pl.when(i == 0)
    def _():
        plane = y_hbm.shape[0] // TOP_K
        spare = 2 * tm // TOP_K * TOK_ROWS
        ybuf[0] = jnp.zeros(ybuf.shape[1:], ybuf.dtype)
        for k in range(TOP_K):
            fill = pltpu.make_async_copy(ybuf.at[0, pl.ds(0, spare)],
                                         y_hbm.at[pl.ds((k + 1) * plane - spare, spare)], sem.at[0])
            fill.start()
            fill.wait()

    @pl.when(i < na)
    def _():
        _stream_expert_weights(be_ref, nx_ref, i, (wd_hbm,), stage, wsem, (wd_bf,), layer)

        @pl.when(i == 0)
        def _():
            compute(0)

        @pl.when(i > 0)
        def _():
            slot = i % 2

            @pl.when(i >= 2)
            def _():
                wait(slot)

            scatter(i - 1, 1 - slot)
            compute(slot)

    @pl.when(i == pl.num_programs(0) - 1)
    def _():
        last = na - 1

        @pl.when(na >= 2)
        def _():
            wait(na % 2)

        scatter(last, last % 2)
        wait(last % 2)


def routed_ffn(x_packed, w_gate, w_up, w_down, layer, plan, tm):
    slot_token, slot_dest, block_e, next_e, n_active = plan
    T = x_packed.shape[0] // TOK_ROWS
    NB = block_e.shape[0]
    D, FF = w_gate.shape[-2], w_gate.shape[-1]
    t_pad = T + 2 * tm // TOP_K
    buf = lambda n: pltpu.VMEM((n, tm * TOK_ROWS, LANES), jnp.uint32)
    row = lambda i, be, nx, na, ix: (jnp.minimum(i, na[0] - 1), 0)
    hbm = pl.BlockSpec(memory_space=pl.ANY)
    params = pltpu.CompilerParams(dimension_semantics=("arbitrary",), vmem_limit_bytes=V7X_VMEM_LIMIT)
    h = pl.pallas_call(
        functools.partial(_moe_up_body, tm=tm, layer=layer),
        grid_spec=pltpu.PrefetchScalarGridSpec(
            num_scalar_prefetch=4, grid=(NB,),
            in_specs=[hbm, hbm, hbm],
            out_specs=pl.BlockSpec((tm, FF), lambda i, be, nx, nv, ix: (i, 0)),
            scratch_shapes=[buf(GATHER_SLOTS), pltpu.SemaphoreType.DMA((GATHER_SLOTS,)),
                            pltpu.VMEM((tm, D), BF16),
                            pltpu.VMEM((2, D, FF), F32), pltpu.SemaphoreType.DMA((2,)),
                            pltpu.VMEM((D, FF), BF16), pltpu.VMEM((D, FF), BF16)]),
        out_shape=jax.ShapeDtypeStruct((NB * tm, FF), BF16),
        compiler_params=params,
        name="moe_up",
    )(block_e, next_e, n_active, slot_token * TOK_ROWS, x_packed, w_gate, w_up)
    y = pl.pallas_call(
        functools.partial(_moe_down_body, tm=tm, layer=layer),
        grid_spec=pltpu.PrefetchScalarGridSpec(
            num_scalar_prefetch=4, grid=(NB,),
            in_specs=[pl.BlockSpec((tm, FF), row), hbm],
            out_specs=hbm,
            scratch_shapes=[buf(2), pltpu.SemaphoreType.DMA((2,)),
                            pltpu.VMEM((1, FF, D), F32), pltpu.SemaphoreType.DMA((1,)),
                            pltpu.VMEM((FF, D), BF16)]),
        out_shape=jax.ShapeDtypeStruct((TOP_K * t_pad * TOK_ROWS, LANES), jnp.uint32),
        compiler_params=params,
        name="moe_down",
    )(block_e, next_e, n_active, slot_dest * TOK_ROWS, h, w_down)
    return y.reshape(TOP_K, t_pad * TOK_ROWS, LANES)


SLOT_MAP_CHUNK = 2048
SLOT_MAP_UNROLL = 32


def _slot_map_body(dest_ref, src_ref, *, clear_chunk):
    phase, i = pl.program_id(0), pl.program_id(1)

    @pl.when(phase == 0)
    def _():
        def clear(j, carry):
            src_ref[i * clear_chunk + j] = -1
            return carry

        lax.fori_loop(0, clear_chunk, clear, 0, unroll=SLOT_MAP_UNROLL)

    @pl.when(phase == 1)
    def _():
        def place(a, carry):
            src_ref[dest_ref[a]] = i * SLOT_MAP_CHUNK + a
            return carry

        lax.fori_loop(0, SLOT_MAP_CHUNK, place, 0, unroll=SLOT_MAP_UNROLL)


def _slot_map(dest, n_slots):
    A = dest.shape[0]
    n_chunks = -(-A // SLOT_MAP_CHUNK)
    clear_chunk = -(-(n_slots + 1) // (SLOT_MAP_UNROLL * n_chunks)) * SLOT_MAP_UNROLL
    dest = jnp.concatenate([dest, jnp.full((n_chunks * SLOT_MAP_CHUNK - A,), n_slots, jnp.int32)])
    src = pl.pallas_call(
        functools.partial(_slot_map_body, clear_chunk=clear_chunk),
        grid=(2, n_chunks),
        in_specs=[pl.BlockSpec((SLOT_MAP_CHUNK,), lambda ph, i: (i * ph,), memory_space=pltpu.SMEM)],
        out_specs=pl.BlockSpec(memory_space=pltpu.SMEM),
        out_shape=jax.ShapeDtypeStruct((n_chunks * clear_chunk,), jnp.int32),
        compiler_params=pltpu.CompilerParams(dimension_semantics=("arbitrary", "arbitrary")),
        name="slot_map",
    )(dest)
    return src[:n_slots]


def _dispatch_plan(idx, tm):
    T = idx.shape[0]
    A = T * TOP_K
    NB = -(-A // tm) + N_EXPERTS
    spare_t = 2 * tm // TOP_K
    t_pad = T + spare_t
    idx = idx.astype(jnp.int32)
    onehot = idx[:, :, None] == jnp.arange(N_EXPERTS, dtype=jnp.int32)[None, None, :]
    sel = onehot.any(axis=1).astype(jnp.int32)
    incl = jnp.cumsum(sel, axis=0)
    counts = incl[-1]
    padded = (counts + tm - 1) // tm * tm
    pad_end = jnp.cumsum(padded)
    pad_start = pad_end - padded
    n_active = (pad_end[-1] // tm).astype(jnp.int32)
    blk = jnp.arange(NB, dtype=jnp.int32)
    block_e = jnp.minimum(jnp.searchsorted(pad_end, jnp.minimum(blk, n_active - 1) * tm, side='right'),
                          N_EXPERTS - 1).astype(jnp.int32)
    e_ids = jnp.arange(N_EXPERTS, dtype=jnp.int32)
    later_used = lax.cummin(jnp.where(counts > 0, e_ids, N_EXPERTS), reverse=True)
    next_used = jnp.concatenate([later_used[1:], jnp.full((1,), N_EXPERTS, jnp.int32)])
    next_e = jnp.where(next_used[block_e] < N_EXPERTS, next_used[block_e], block_e).astype(jnp.int32)
    slot_of = (incl - sel + pad_start[None, :])[:, None, :]
    dest = jnp.sum(jnp.where(onehot, slot_of, 0), axis=-1).reshape(A)
    slot_src = _slot_map(dest, NB * tm)
    slot = jnp.arange(NB * tm, dtype=jnp.int32)
    valid = slot_src >= 0
    slot_token = jnp.where(valid, slot_src // TOP_K, 0)
    s2 = (slot // tm % 2) * tm + slot % tm
    spare = (s2 // spare_t) * t_pad + T + s2 % spare_t
    slot_dest = jnp.where(valid, (slot_src % TOP_K) * t_pad + slot_src // TOP_K, spare)
    return slot_token, slot_dest, block_e, next_e, n_active.reshape(1)


def moe_block(x_parts, xb_parts, xp_parts, p, layer):
    w_t = jnp.swapaxes(p['router_w'], 1, 2)
    bias_col = p['router_bias'][:, :, None]
    routed = [route(x, w_t, bias_col, layer) for x in x_parts]
    idx = jnp.concatenate([r[0] for r in routed], axis=1).T
    wts = jnp.concatenate([r[1] for r in routed], axis=1).T
    plan = _dispatch_plan(idx, MOE_TM)
    y = routed_ffn(jnp.concatenate(xp_parts, axis=0), p['exp_w_gate'], p['exp_w_up'], p['exp_w_down'],
                   layer, plan, MOE_TM)
    shared = [shared_ffn(xb, p['shared_w_gate'], p['shared_w_up'], p['shared_w_down'], layer)
              for xb in xb_parts]
    return y, wts, shared


def _rope_tables(pos, half):
    inv_freq = ROPE_THETA ** (-jnp.arange(half, dtype=F32) / half)
    ang = pos.astype(F32)[:, None] * inv_freq[None, :]
    return jnp.cos(ang), jnp.sin(ang)


def _even_prompt(x_bf, B, L, p, i):
    proj = matmul(x_bf, p['w_in_even'][i].astype(BF16), name="in_proj_even")
    proj3 = proj.reshape(B, L, -1)
    cos, sin = _rope_tables(jnp.arange(L, dtype=jnp.int32), RET_HEAD_DIM // 2)
    conv_out, conv_new = conv_prompt(proj3, p['conv_w'][i], p['conv_b'][i], p['conv_ln_g'][i], p['conv_ln_b'][i])
    ret_out, s_new = retention_prompt(proj3, cos, sin, p['ret_gn_g'][i], p['ret_gn_b'][i])
    mixed = jnp.concatenate([conv_out, ret_out], axis=-1).reshape(B * L, -1)
    return matmul(mixed, p['w_out_even'][i].astype(BF16), name="out_proj_even"), conv_new, s_new


def _even_sample(x_bf, conv_buf, ret_state, p, i):
    B = x_bf.shape[0]
    H, d = RET_HEADS, RET_HEAD_DIM
    proj = matmul(x_bf, p['w_in_even'][i].astype(BF16), name="in_proj_even")
    conv_out, conv_new = conv_sample(proj, conv_buf, p['conv_w'][i], p['conv_b'][i],
                                     p['conv_ln_g'][i], p['conv_ln_b'][i])
    o2 = 2 * CONV_CH
    q_col = proj[:, o2:o2 + RET_WIDTH].reshape(B, H, d, 1)
    k_col = proj[:, o2 + RET_WIDTH:o2 + 2 * RET_WIDTH].reshape(B, H, d, 1)
    v_row = proj[:, o2 + 2 * RET_WIDTH:o2 + 3 * RET_WIDTH].reshape(B, 1, RET_WIDTH)
    g_row = proj[:, o2 + 3 * RET_WIDTH:].reshape(B, 1, RET_WIDTH)
    cos, sin = _rope_tables(jnp.full((1,), PAST_LEN, jnp.int32), d // 2)
    ret_out, s_new = retention_sample(q_col, k_col, v_row, g_row, ret_state, cos.reshape(d // 2, 1),
                                      sin.reshape(d // 2, 1), p['ret_gn_g'][i], p['ret_gn_b'][i])
    mixed = jnp.concatenate([conv_out, ret_out.reshape(B, RET_WIDTH)], axis=-1)
    return matmul(mixed, p['w_out_even'][i].astype(BF16), name="out_proj_even"), conv_new, s_new


def _att_rope_tables(pos, rows):
    cos, sin = _rope_tables(pos, ATT_HEAD_DIM // 2)
    cos = jnp.tile(cos, (rows // cos.shape[0], LANES // cos.shape[1]))
    sin = jnp.tile(sin, (rows // sin.shape[0], LANES // sin.shape[1]))
    return cos, sin


def _odd_prompt(x_bf, B, L, p, i):
    cos, sin = _att_rope_tables(jnp.arange(L, dtype=jnp.int32), L)
    qkv = matmul(x_bf, p['w_qkv_odd'][i].astype(BF16), bias=p['b_qkv_odd'][i],
                 rope=(cos, sin, Q_WIDTH + KV_WIDTH), name="qkv_proj_odd")
    qkv3 = qkv.reshape(B, L, -1)
    o = swa_prompt(qkv3, p['sinks'][i])
    out = matmul(o.reshape(B * L, Q_WIDTH), p['w_out_odd'][i].astype(BF16), bias=p['b_out_odd'][i],
                 name="out_proj_odd")
    k_new = qkv3[:, L - WINDOW:, Q_WIDTH:Q_WIDTH + KV_WIDTH].reshape(B, WINDOW, N_KV_HEADS, ATT_HEAD_DIM)
    v_new = qkv3[:, L - WINDOW:, Q_WIDTH + KV_WIDTH:].reshape(B, WINDOW, N_KV_HEADS, ATT_HEAD_DIM)
    return out, k_new, v_new


def _odd_sample(x_bf, k_buf, v_buf, p, i):
    B = x_bf.shape[0]
    cos, sin = _att_rope_tables(jnp.full((1,), PAST_LEN, jnp.int32), B)
    qkv = matmul(x_bf, p['w_qkv_odd'][i].astype(BF16), bias=p['b_qkv_odd'][i],
                 rope=(cos, sin, Q_WIDTH + KV_WIDTH), name="qkv_proj_odd")
    q3 = qkv[:, :Q_WIDTH].reshape(B, N_HEADS, ATT_HEAD_DIM)
    kn3 = qkv[:, Q_WIDTH:Q_WIDTH + KV_WIDTH].reshape(B, N_KV_HEADS, ATT_HEAD_DIM)
    vn3 = qkv[:, Q_WIDTH + KV_WIDTH:].reshape(B, N_KV_HEADS, ATT_HEAD_DIM)
    o, k_new, v_new = swa_sample(q3, kn3, vn3, k_buf.reshape(B, WINDOW, KV_WIDTH),
                                 v_buf.reshape(B, WINDOW, KV_WIDTH), p['sinks'][i])
    out = matmul(o.reshape(B, Q_WIDTH), p['w_out_odd'][i].astype(BF16), bias=p['b_out_odd'][i],
                 name="out_proj_odd")
    shape = (B, WINDOW, N_KV_HEADS, ATT_HEAD_DIM)
    return out, k_new.reshape(shape), v_new.reshape(shape)


def kernel(x_prompt, x_sample, state_conv, state_ret, state_win_k, state_win_v,
           w_in_even, conv_w, conv_b, conv_ln_g, conv_ln_b, ret_gn_g, ret_gn_b, w_out_even,
           w_qkv_odd, b_qkv_odd, sinks, w_out_odd, b_out_odd, ln_g, ln_b,
           router_w, router_bias, exp_w_gate, exp_w_up, exp_w_down,
           shared_w_gate, shared_w_up, shared_w_down):
    p = dict(w_in_even=w_in_even, conv_w=conv_w, conv_b=conv_b, conv_ln_g=conv_ln_g, conv_ln_b=conv_ln_b,
             ret_gn_g=ret_gn_g, ret_gn_b=ret_gn_b, w_out_even=w_out_even,
             w_qkv_odd=w_qkv_odd, b_qkv_odd=b_qkv_odd, sinks=sinks, w_out_odd=w_out_odd, b_out_odd=b_out_odd,
             ln_g=ln_g, ln_b=ln_b, router_w=router_w, router_bias=router_bias,
             exp_w_gate=exp_w_gate, exp_w_up=exp_w_up, exp_w_down=exp_w_down,
             shared_w_gate=shared_w_gate, shared_w_up=shared_w_up, shared_w_down=shared_w_down)
    B, L, D = x_prompt.shape
    Bs = x_sample.shape[0]
    Tp = B * L
    xp = x_prompt.reshape(Tp, D)
    xs = x_sample.reshape(Bs, D)
    xp_bf, xs_bf = xp.astype(BF16), xs.astype(BF16)
    conv_p, conv_s, ret_p, ret_s, wk_p, wk_s, wv_p, wv_s = [], [], [], [], [], [], [], []
    for layer in range(DEPTH):
        i = layer // 2
        if layer % 2 == 0:
            hp, c_new, r_new = _even_prompt(xp_bf, B, L, p, i)
            conv_p.append(c_new)
            ret_p.append(r_new)
            hs, c_new, r_new = _even_sample(xs_bf, state_conv[i], state_ret[i], p, i)
            conv_s.append(c_new)
            ret_s.append(r_new)
        else:
            hp, k_new, v_new = _odd_prompt(xp_bf, B, L, p, i)
            wk_p.append(k_new)
            wv_p.append(v_new)
            hs, k_new, v_new = _odd_sample(xs_bf, state_win_k[i], state_win_v[i], p, i)
            wk_s.append(k_new)
            wv_s.append(v_new)
        xp, xp_bf, xp_pk = deepnorm(xp, [(hp, 0)], ln_g[layer, 0], ln_b[layer, 0], name="ln_mixer_prompt")
        xs, xs_bf, xs_pk = deepnorm(xs, [(hs, 0)], ln_g[layer, 0], ln_b[layer, 0], name="ln_mixer_sample")
        y, wts, shared = moe_block([xp, xs], [xp_bf, xs_bf], [xp_pk, xs_pk], p, layer)
        xp, xp_bf, _ = deepnorm(xp, [(shared[0], 0)], ln_g[layer, 1], ln_b[layer, 1], combine=(y, wts, 0),
                                tm=64, name="ln_moe_prompt")
        xs, xs_bf, _ = deepnorm(xs, [(shared[1], 0)], ln_g[layer, 1], ln_b[layer, 1], combine=(y, wts, Tp),
                                tm=32, name="ln_moe_sample")
    return (xp.reshape(B, L, D), xs.reshape(Bs, 1, D),
            jnp.stack(conv_p), jnp.stack(conv_s), jnp.stack(ret_p), jnp.stack(ret_s),
            jnp.stack(wk_p), jnp.stack(wk_s), jnp.stack(wv_p), jnp.stack(wv_s))
```

```python
import functools

import jax
import jax.numpy as jnp
import numpy as np
from jax import lax
from jax.experimental import pallas as pl
from jax.experimental.pallas import tpu as pltpu

F32 = jnp.float32
BF16 = jnp.bfloat16

D_MODEL = 4096
DEPTH = 2
PAST_LEN = 16384
CONV_CH = D_MODEL // 2
CONV_WIDTH = 31
CONV_HALO = 32
RET_HEADS = 8
RET_HEAD_DIM = 256
RET_WIDTH = RET_HEADS * RET_HEAD_DIM
RET_CHUNK = 128
RET_STEP_ROWS = 512
ATT_HEAD_DIM = 64
N_HEADS = 64
N_KV_HEADS = 8
GROUP = N_HEADS // N_KV_HEADS
WINDOW = 128
Q_WIDTH = N_HEADS * ATT_HEAD_DIM
KV_WIDTH = N_KV_HEADS * ATT_HEAD_DIM
ROPE_THETA = 10000.0
N_EXPERTS = 128
TOP_K = 8
N_GROUPS = 8
TOPK_GROUPS = 4
EXPERTS_PER_GROUP = N_EXPERTS // N_GROUPS
EXPERT_FF = 512
ROUTED_SCALE = 2.5
DN_ALPHA = (2.0 * DEPTH) ** 0.25
LN_EPS = 1e-5

LANES = 128
V7X_VMEM_LIMIT = 52 * 1024 * 1024
MOE_TM = 256


def _params(*sem):
    return pltpu.CompilerParams(dimension_semantics=sem, vmem_limit_bytes=V7X_VMEM_LIMIT)


def _silu(x):
    return x * jax.nn.sigmoid(x)


def _rot_half_64(x):
    n = x.shape[-1]
    lane = lax.broadcasted_iota(jnp.int32, x.shape, x.ndim - 1)
    fwd = pltpu.roll(x, n - ATT_HEAD_DIM // 2, x.ndim - 1)
    bwd = pltpu.roll(x, ATT_HEAD_DIM // 2, x.ndim - 1)
    return jnp.where(lane % ATT_HEAD_DIM < ATT_HEAD_DIM // 2, -fwd, bwd)


def _mm_body(*refs, has_bias, rope_tiles):
    x_ref, w_ref = refs[0], refs[1]
    pos = 2
    b_ref = None
    if has_bias:
        b_ref = refs[pos]
        pos += 1
    if rope_tiles:
        cos_ref, sin_ref = refs[pos], refs[pos + 1]
        pos += 2
    o_ref = refs[pos]
    acc = jnp.dot(x_ref[...].astype(BF16), w_ref[...].astype(BF16), preferred_element_type=F32)
    if has_bias:
        acc = acc + b_ref[...]
    if rope_tiles:
        reps = acc.shape[1] // LANES
        cos = jnp.concatenate([cos_ref[...]] * reps, axis=1)
        sin = jnp.concatenate([sin_ref[...]] * reps, axis=1)
        @pl.when(pl.program_id(1) < rope_tiles)
        def _():
            o_ref[...] = (acc * cos + _rot_half_64(acc) * sin).astype(o_ref.dtype)

        @pl.when(pl.program_id(1) >= rope_tiles)
        def _():
            o_ref[...] = acc.astype(o_ref.dtype)
    else:
        o_ref[...] = acc.astype(o_ref.dtype)


def matmul(x, w, bias=None, rope=None, out_dtype=F32, tm=1024, tn=512, name="matmul"):
    M, K = x.shape
    N = w.shape[1]
    tm = min(tm, M)
    tn = min(tn, N)
    assert M % tm == 0 and N % tn == 0
    in_specs = [pl.BlockSpec((tm, K), lambda i, j: (i, 0)),
                pl.BlockSpec((K, tn), lambda i, j: (0, j))]
    args = [x, w]
    if bias is not None:
        in_specs.append(pl.BlockSpec((1, tn), lambda i, j: (0, j)))
        args.append(bias.reshape(1, N))
    rope_tiles = 0
    if rope is not None:
        cos, sin, n_cols = rope
        assert n_cols % tn == 0 and cos.shape[0] % tm == 0
        rope_tiles = n_cols // tn
        nrb = cos.shape[0] // tm
        in_specs += [pl.BlockSpec((tm, LANES), lambda i, j: (i % nrb, 0))] * 2
        args += [cos, sin]
    return pl.pallas_call(
        functools.partial(_mm_body, has_bias=bias is not None, rope_tiles=rope_tiles),
        grid=(M // tm, N // tn),
        in_specs=in_specs,
        out_specs=pl.BlockSpec((tm, tn), lambda i, j: (i, j)),
        out_shape=jax.ShapeDtypeStruct((M, N), out_dtype),
        compiler_params=_params("parallel", "arbitrary"),
        name=name,
    )(*args)


def _split_bf16(x):
    hi = x.astype(BF16)
    lo = (x - hi.astype(F32)).astype(BF16)
    return hi, lo


def _first_argmax(x, row, n):
    m = jnp.max(x, axis=0, keepdims=True)
    i = jnp.min(jnp.where(x == m, row, n), axis=0, keepdims=True)
    return m, i


def _router_body(x_ref, wt_ref, bias_ref, idx_ref, wts_ref):
    nt = (((1,), (1,)), ((), ()))
    xh, xl = _split_bf16(x_ref[...])
    wh, wl = _split_bf16(wt_ref[0])
    logits = lax.dot_general(wh, xh, nt, preferred_element_type=F32)
    logits = logits + lax.dot_general(wh, xl, nt, preferred_element_type=F32)
    logits = logits + lax.dot_general(wl, xh, nt, preferred_element_type=F32)
    scores = jax.nn.sigmoid(logits)
    biased = scores + bias_ref[0]
    tm = scores.shape[1]
    neg = -jnp.inf
    row_g = lax.broadcasted_iota(jnp.int32, (EXPERTS_PER_GROUP, tm), 0)
    grp = []
    for g in range(N_GROUPS):
        xg = biased[g * EXPERTS_PER_GROUP:(g + 1) * EXPERTS_PER_GROUP]
        m1, i1 = _first_argmax(xg, row_g, EXPERTS_PER_GROUP)
        m2 = jnp.max(jnp.where(row_g == i1, neg, xg), axis=0, keepdims=True)
        grp.append(m1 + m2)
    gs = jnp.concatenate(grp, axis=0)
    row_n = lax.broadcasted_iota(jnp.int32, (N_GROUPS, tm), 0)
    chosen = jnp.zeros((N_GROUPS, tm), F32)
    for _ in range(TOPK_GROUPS):
        _, i = _first_argmax(gs, row_n, N_GROUPS)
        hit = row_n == i
        chosen = jnp.where(hit, 1.0, chosen)
        gs = jnp.where(hit, neg, gs)
    emask = jnp.concatenate([jnp.broadcast_to(chosen[g:g + 1], (EXPERTS_PER_GROUP, tm))
                             for g in range(N_GROUPS)], axis=0)
    cand = jnp.where(emask > 0.0, biased, neg)
    row_e = lax.broadcasted_iota(jnp.int32, (N_EXPERTS, tm), 0)
    ids, ws = [], []
    for _ in range(TOP_K):
        _, i = _first_argmax(cand, row_e, N_EXPERTS)
        hit = row_e == i
        ids.append(i)
        ws.append(jnp.sum(jnp.where(hit, scores, 0.0), axis=0, keepdims=True))
        cand = jnp.where(hit, neg, cand)
    w = jnp.concatenate(ws, axis=0)
    idx_ref[...] = jnp.concatenate(ids, axis=0)
    wts_ref[...] = w / jnp.sum(w, axis=0, keepdims=True) * ROUTED_SCALE


def route(x, w_t, bias_col, layer, tm=512):
    M, K = x.shape
    E = w_t.shape[1]
    tm = min(tm, M)
    return pl.pallas_call(
        _router_body,
        grid=(M // tm,),
        in_specs=[pl.BlockSpec((tm, K), lambda i: (i, 0)),
                  pl.BlockSpec((1, E, K), lambda i: (layer, 0, 0)),
                  pl.BlockSpec((1, E, 1), lambda i: (layer, 0, 0))],
        out_specs=[pl.BlockSpec((TOP_K, tm), lambda i: (0, i))] * 2,
        out_shape=[jax.ShapeDtypeStruct((TOP_K, M), jnp.int32), jax.ShapeDtypeStruct((TOP_K, M), F32)],
        compiler_params=_params("parallel"),
        name="router",
    )(x, w_t, bias_col)


def _layer_norm_rows(z, g, b):
    mu = jnp.mean(z, axis=-1, keepdims=True)
    zc = z - mu
    var = jnp.mean(zc * zc, axis=-1, keepdims=True)
    return zc * lax.rsqrt(var + LN_EPS) * g + b


def _unpack_pair(w):
    lo = lax.bitcast_convert_type(w << 16, F32)
    hi = lax.bitcast_convert_type(w & jnp.uint32(0xFFFF0000), F32)
    return lo, hi


def _pack_pair(lo, hi):
    lo_bits = lax.bitcast_convert_type(lo.astype(BF16).astype(F32), jnp.uint32) >> 16
    hi_bits = lax.bitcast_convert_type(hi.astype(BF16).astype(F32), jnp.uint32) & jnp.uint32(0xFFFF0000)
    return hi_bits | lo_bits


TOK_ROWS = D_MODEL // 2 // LANES


def _load_token_rows(ref, s, tm):
    return ref[pl.ds(s, tm, stride=TOK_ROWS), :]


def _store_token_rows(ref, y, tm):
    half = y.shape[1] // 2
    for s in range(TOK_ROWS):
        words = _pack_pair(y[:, s * LANES:(s + 1) * LANES], y[:, half + s * LANES:half + (s + 1) * LANES])
        ref[pl.ds(s, tm, stride=TOK_ROWS), :] = words


def _ln_body(*refs, n_h, combine):
    x_ref = refs[0]
    h_refs = refs[1:1 + n_h]
    pos = 1 + n_h
    if combine:
        y_ref, w_ref = refs[pos], refs[pos + 1]
        pos += 2
    g_ref, b_ref, o_ref, ob_ref, op_ref = refs[pos:]
    tm, d = x_ref.shape
    half = d // 2
    z = DN_ALPHA * x_ref[...]
    for h_ref in h_refs:
        z = z + h_ref[...]
    if combine:
        wk = [jnp.broadcast_to(w_ref[:, k:k + 1], (tm, LANES)) for k in range(TOP_K)]
        cols_lo, cols_hi = [], []
        for s in range(TOK_ROWS):
            acc_lo = z[:, s * LANES:(s + 1) * LANES]
            acc_hi = z[:, half + s * LANES:half + (s + 1) * LANES]
            for k in range(TOP_K):
                lo, hi = _unpack_pair(_load_token_rows(y_ref.at[k], s, tm))
                acc_lo = acc_lo + wk[k] * lo
                acc_hi = acc_hi + wk[k] * hi
            cols_lo.append(acc_lo)
            cols_hi.append(acc_hi)
        z = jnp.concatenate(cols_lo + cols_hi, axis=1)
    y = _layer_norm_rows(z, g_ref[...], b_ref[...])
    o_ref[...] = y
    ob_ref[...] = y.astype(BF16)
    _store_token_rows(op_ref, y, tm)


def deepnorm(x, hs, g, b, combine=None, tm=256, name="deepnorm"):
    M, D = x.shape
    tm = min(tm, M)
    assert M % tm == 0
    row_spec = lambda width, off: pl.BlockSpec((tm, width), functools.partial(lambda i, o: (i + o, 0), o=off // tm))
    in_specs = [row_spec(D, 0)]
    args = [x]
    for h, off in hs:
        assert off % tm == 0
        in_specs.append(row_spec(D, off))
        args.append(h)
    if combine is not None:
        y, wts, off = combine
        assert off % tm == 0
        y_spec = pl.BlockSpec((TOP_K, tm * TOK_ROWS, LANES),
                              functools.partial(lambda i, o: (0, i + o, 0), o=off // tm))
        in_specs += [y_spec, row_spec(TOP_K, off)]
        args += [y, wts]
    in_specs += [pl.BlockSpec((1, D), lambda i: (0, 0))] * 2
    args += [g.reshape(1, D), b.reshape(1, D)]
    return pl.pallas_call(
        functools.partial(_ln_body, n_h=len(hs), combine=combine is not None),
        grid=(M // tm,),
        in_specs=in_specs,
        out_specs=[row_spec(D, 0), row_spec(D, 0), pl.BlockSpec((tm * TOK_ROWS, LANES), lambda i: (i, 0))],
        out_shape=[jax.ShapeDtypeStruct((M, D), F32), jax.ShapeDtypeStruct((M, D), BF16),
                   jax.ShapeDtypeStruct((M * TOK_ROWS, LANES), jnp.uint32)],
        compiler_params=_params("parallel"),
        name=name,
    )(*args)


def _conv_prompt_body(a_ref, g_ref, pa_ref, pg_ref, w_ref, bdw_ref, lng_ref, lnb_ref,
                      o_ref, st_ref, full_ref, y_ref, *, tl):
    i = pl.program_id(1)
    u = a_ref[0] * jax.nn.sigmoid(g_ref[0])
    up = pa_ref[0] * jax.nn.sigmoid(pg_ref[0])
    up = jnp.where(i > 0, up, 0.0)
    full_ref[0:CONV_HALO, :] = up
    full_ref[CONV_HALO:CONV_HALO + tl, :] = u
    base = CONV_HALO - (CONV_WIDTH - 1)
    n_col = u.shape[1] // LANES

    def col_step(c, carry):
        off = pl.multiple_of(c * LANES, LANES)
        acc = jnp.zeros((tl, LANES), F32)
        for j in range(CONV_WIDTH):
            acc = acc + w_ref[j:j + 1, pl.ds(off, LANES)] * full_ref[base + j:base + j + tl, pl.ds(off, LANES)]
        y_ref[:, pl.ds(off, LANES)] = acc
        return carry

    lax.fori_loop(0, n_col, col_step, 0)
    y = _layer_norm_rows(y_ref[...] + bdw_ref[...], lng_ref[...], lnb_ref[...])
    o_ref[0] = _silu(y).astype(o_ref.dtype)

    @pl.when(i == pl.num_programs(1) - 1)
    def _():
        st_ref[0] = full_ref[CONV_HALO + tl - (CONV_WIDTH - 1):CONV_HALO + tl, :]


def conv_prompt(proj3, w_dw, b_dw, ln_g, ln_b, tl=128):
    B, L, _ = proj3.shape
    C = CONV_CH
    nh = tl // CONV_HALO
    vec = lambda v: v.reshape(1, C)
    return pl.pallas_call(
        functools.partial(_conv_prompt_body, tl=tl),
        grid=(B, L // tl),
        in_specs=[
            pl.BlockSpec((1, tl, C), lambda b, i: (b, i, 0)),
            pl.BlockSpec((1, tl, C), lambda b, i: (b, i, 1)),
            pl.BlockSpec((1, CONV_HALO, C), lambda b, i: (b, jnp.maximum(i * nh - 1, 0), 0)),
            pl.BlockSpec((1, CONV_HALO, C), lambda b, i: (b, jnp.maximum(i * nh - 1, 0), 1)),
            pl.BlockSpec((CONV_WIDTH, C), lambda b, i: (0, 0)),
            pl.BlockSpec((1, C), lambda b, i: (0, 0)),
            pl.BlockSpec((1, C), lambda b, i: (0, 0)),
            pl.BlockSpec((1, C), lambda b, i: (0, 0)),
        ],
        out_specs=[
            pl.BlockSpec((1, tl, C), lambda b, i: (b, i, 0)),
            pl.BlockSpec((1, CONV_WIDTH - 1, C), lambda b, i: (b, 0, 0)),
        ],
        out_shape=[jax.ShapeDtypeStruct((B, L, C), BF16),
                   jax.ShapeDtypeStruct((B, CONV_WIDTH - 1, C), F32)],
        scratch_shapes=[pltpu.VMEM((CONV_HALO + tl, C), F32), pltpu.VMEM((tl, C), F32)],
        compiler_params=_params("parallel", "arbitrary"),
        name="conv_prompt",
    )(proj3, proj3, proj3, proj3, w_dw, vec(b_dw), vec(ln_g), vec(ln_b))


def _conv_sample_body(a_ref, g_ref, buf_ref, w_ref, bdw_ref, lng_ref, lnb_ref, o_ref, st_ref):
    u = a_ref[...] * jax.nn.sigmoid(g_ref[...])
    nb = CONV_WIDTH - 1
    acc = w_ref[nb:nb + 1, :] * u
    for j in range(nb):
        acc = acc + w_ref[j:j + 1, :] * buf_ref[:, j, :]
    y = _layer_norm_rows(acc + bdw_ref[...], lng_ref[...], lnb_ref[...])
    o_ref[...] = _silu(y).astype(o_ref.dtype)
    st_ref[:, 0:nb - 1, :] = buf_ref[:, 1:nb, :]
    st_ref[:, nb - 1, :] = u


def conv_sample(proj, buf, w_dw, b_dw, ln_g, ln_b, tb=8):
    B = proj.shape[0]
    C = CONV_CH
    nb = CONV_WIDTH - 1
    vec = lambda v: v.reshape(1, C)
    return pl.pallas_call(
        _conv_sample_body,
        grid=(B // tb,),
        in_specs=[
            pl.BlockSpec((tb, C), lambda b: (b, 0)),
            pl.BlockSpec((tb, C), lambda b: (b, 1)),
            pl.BlockSpec((tb, nb, C), lambda b: (b, 0, 0)),
            pl.BlockSpec((CONV_WIDTH, C), lambda b: (0, 0)),
            pl.BlockSpec((1, C), lambda b: (0, 0)),
            pl.BlockSpec((1, C), lambda b: (0, 0)),
            pl.BlockSpec((1, C), lambda b: (0, 0)),
        ],
        out_specs=[pl.BlockSpec((tb, C), lambda b: (b, 0)),
                   pl.BlockSpec((tb, nb, C), lambda b: (b, 0, 0))],
        out_shape=[jax.ShapeDtypeStruct((B, C), BF16), jax.ShapeDtypeStruct((B, nb, C), F32)],
        compiler_params=_params("parallel"),
        name="conv_sample",
    )(proj, proj, buf, w_dw, vec(b_dw), vec(ln_g), vec(ln_b))


def _ret_log_gamma():
    return np.log1p(-np.exp2(-5.0 - np.arange(RET_HEADS, dtype=np.float64)))


def _group_norm_gate(o, gate, gn_g, gn_b):
    mu = jnp.mean(o, axis=-1, keepdims=True)
    oc = o - mu
    var = jnp.mean(oc * oc, axis=-1, keepdims=True)
    return (oc * lax.rsqrt(var + LN_EPS) * gn_g + gn_b) * _silu(gate)


def _rope_256(x, cos, sin):
    half = RET_HEAD_DIM // 2
    x1, x2 = x[..., :half], x[..., half:]
    return jnp.concatenate([x1 * cos - x2 * sin, x2 * cos + x1 * sin], axis=-1)


def _ret_prompt_body(q_ref, k_ref, v_ref, g_ref, cos_ref, sin_ref, intra_ref, qd_ref, kd_ref, cd_ref,
                     gng_ref, gnb_ref, o_ref, s_ref):
    c = pl.program_id(2)

    @pl.when(c == 0)
    def _():
        s_ref[...] = jnp.zeros_like(s_ref)

    cw = RET_CHUNK
    for j in range(q_ref.shape[1] // cw):
        rows = slice(j * cw, (j + 1) * cw)
        cos, sin = cos_ref[rows, :], sin_ref[rows, :]
        q = _rope_256(q_ref[0, rows, :], cos, sin)
        k = _rope_256(k_ref[0, rows, :], cos, sin) * (RET_HEAD_DIM ** -0.5)
        vb = v_ref[0, rows, :].astype(BF16)
        s_prev = s_ref[0, 0]
        scores = lax.dot_general(q.astype(BF16), k.astype(BF16), (((1,), (1,)), ((), ())),
                                 preferred_element_type=F32) * intra_ref[0]
        o = jnp.dot(scores.astype(BF16), vb, preferred_element_type=F32)
        o = o + jnp.dot((q * qd_ref[0]).astype(BF16), s_prev.astype(BF16), preferred_element_type=F32)
        kdt = (k * kd_ref[0]).T.astype(BF16)
        kv = jnp.dot(kdt, vb, preferred_element_type=F32)
        s_ref[0, 0] = cd_ref[0] * s_prev + kv
        o_ref[0, rows, :] = _group_norm_gate(o, g_ref[0, rows, :], gng_ref[...], gnb_ref[...]).astype(o_ref.dtype)


def retention_prompt(proj3, cos, sin, gn_g, gn_b):
    B, L, _ = proj3.shape
    H, d, c = RET_HEADS, RET_HEAD_DIM, RET_CHUNK
    lg = _ret_log_gamma()
    idx = np.arange(c, dtype=np.float64)
    rel = idx[:, None] - idx[None, :]
    intra = np.where(rel[None] >= 0, np.exp(lg[:, None, None] * np.maximum(rel, 0.0)[None]), 0.0)
    ones = np.ones((1, 1, d))
    qd = np.exp(lg[:, None] * (idx + 1.0)[None, :])[:, :, None] * ones
    kd = np.exp(lg[:, None] * (c - 1.0 - idx)[None, :])[:, :, None] * ones
    cd = np.exp(lg * c)[:, None, None] * ones
    col0 = 2 * CONV_CH // d
    tl = min(RET_STEP_ROWS, L)
    assert L % tl == 0 and tl % c == 0
    head_spec = lambda off: pl.BlockSpec((1, tl, d), lambda b, h, n: (b, n, col0 + off * H + h))
    tab_spec = pl.BlockSpec((1, c, d), lambda b, h, n: (h, 0, 0))
    return pl.pallas_call(
        _ret_prompt_body,
        grid=(B, H, L // tl),
        in_specs=[
            head_spec(0), head_spec(1), head_spec(2), head_spec(3),
            pl.BlockSpec((tl, d // 2), lambda b, h, n: (n, 0)),
            pl.BlockSpec((tl, d // 2), lambda b, h, n: (n, 0)),
            pl.BlockSpec((1, c, c), lambda b, h, n: (h, 0, 0)),
            tab_spec, tab_spec,
            pl.BlockSpec((1, 1, d), lambda b, h, n: (h, 0, 0)),
            pl.BlockSpec((1, d), lambda b, h, n: (0, h)),
            pl.BlockSpec((1, d), lambda b, h, n: (0, h)),
        ],
        out_specs=[
            pl.BlockSpec((1, tl, d), lambda b, h, n: (b, n, h)),
            pl.BlockSpec((1, 1, d, d), lambda b, h, n: (b, h, 0, 0)),
        ],
        out_shape=[jax.ShapeDtypeStruct((B, L, H * d), BF16),
                   jax.ShapeDtypeStruct((B, H, d, d), F32)],
        compiler_params=_params("parallel", "parallel", "arbitrary"),
        name="retention_prompt",
    )(proj3, proj3, proj3, proj3, cos, sin,
      jnp.asarray(intra, F32), jnp.asarray(qd, F32), jnp.asarray(kd, F32), jnp.asarray(cd, F32),
      gn_g.reshape(1, H * d), gn_b.reshape(1, H * d))


def _ret_sample_body(qc_ref, kc_ref, v_ref, g_ref, s_ref, cosc_ref, sinc_ref, gam_ref, gng_ref, gnb_ref,
                     o_ref, so_ref):
    half = RET_HEAD_DIM // 2
    cos, sin = cosc_ref[...], sinc_ref[...]

    def rope_col(x):
        x1, x2 = x[:half], x[half:]
        return jnp.concatenate([x1 * cos - x2 * sin, x2 * cos + x1 * sin], axis=0)

    for h in range(RET_HEADS):
        lo, hi = h * RET_HEAD_DIM, (h + 1) * RET_HEAD_DIM
        q = rope_col(qc_ref[0, h])
        k = rope_col(kc_ref[0, h]) * (RET_HEAD_DIM ** -0.5)
        v = v_ref[0, :, lo:hi]
        s_new = gam_ref[h] * s_ref[0, h] + k * v
        so_ref[0, h] = s_new
        o = jnp.sum(q * s_new, axis=0, keepdims=True)
        o_ref[0, :, lo:hi] = _group_norm_gate(o, g_ref[0, :, lo:hi], gng_ref[:, lo:hi],
                                              gnb_ref[:, lo:hi]).astype(o_ref.dtype)


def retention_sample(q_col, k_col, v_row, g_row, state, cos_col, sin_col, gn_g, gn_b):
    B = state.shape[0]
    H, d = RET_HEADS, RET_HEAD_DIM
    gam = jnp.asarray(np.exp(_ret_log_gamma()), F32)
    col_spec = pl.BlockSpec((1, H, d, 1), lambda b: (b, 0, 0, 0))
    row_spec = pl.BlockSpec((1, 1, H * d), lambda b: (b, 0, 0))
    return pl.pallas_call(
        _ret_sample_body,
        grid=(B,),
        in_specs=[
            col_spec, col_spec, row_spec, row_spec,
            pl.BlockSpec((1, H, d, d), lambda b: (b, 0, 0, 0)),
            pl.BlockSpec((d // 2, 1), lambda b: (0, 0)),
            pl.BlockSpec((d // 2, 1), lambda b: (0, 0)),
            pl.BlockSpec(memory_space=pltpu.SMEM),
            pl.BlockSpec((1, H * d), lambda b: (0, 0)),
            pl.BlockSpec((1, H * d), lambda b: (0, 0)),
        ],
        out_specs=[row_spec, pl.BlockSpec((1, H, d, d), lambda b: (b, 0, 0, 0))],
        out_shape=[jax.ShapeDtypeStruct((B, 1, H * d), BF16), jax.ShapeDtypeStruct((B, H, d, d), F32)],
        compiler_params=_params("parallel"),
        name="retention_sample",
    )(q_col, k_col, v_row, g_row, state, cos_col, sin_col, gam, gn_g.reshape(1, H * d), gn_b.reshape(1, H * d))


def _pair_operand(x2, h, lane):
    col = x2[:, (h // 2) * LANES:(h // 2 + 1) * LANES]
    swapped = pltpu.roll(col, ATT_HEAD_DIM, 1)
    first, second = (col, swapped) if h % 2 == 0 else (swapped, col)
    return jnp.concatenate([jnp.where(lane < ATT_HEAD_DIM, first, 0.0),
                            jnp.where(lane >= ATT_HEAD_DIM, second, 0.0)], axis=0).astype(BF16)


def _swa_prompt_body(sink_ref, q_ref, kc_ref, kp_ref, vc_ref, vp_ref, o_ref):
    j = pl.program_id(1)
    W = WINDOW
    k2 = jnp.concatenate([kp_ref[0], kc_ref[0]], axis=0)
    v2 = jnp.concatenate([vp_ref[0], vc_ref[0]], axis=0)
    row = lax.broadcasted_iota(jnp.int32, (W, 2 * W), 0)
    col = lax.broadcasted_iota(jnp.int32, (W, 2 * W), 1)
    first_key = jnp.where(j > 0, 0, W)
    mask = (col >= jnp.maximum(row, first_key)) & (col <= W + row)
    lane_kv = lax.broadcasted_iota(jnp.int32, (2 * W, LANES), 1)
    lane_o = lax.broadcasted_iota(jnp.int32, (W, LANES), 1)
    scale = ATT_HEAD_DIM ** -0.5
    for h in range(N_KV_HEADS):
        kk = _pair_operand(k2, h, lane_kv)
        vv = _pair_operand(v2, h, lane_kv)
        for p in range(GROUP // 2):
            c0 = h * GROUP * ATT_HEAD_DIM + p * LANES
            qp = q_ref[0, :, c0:c0 + LANES].astype(BF16)
            s = lax.dot_general(qp, kk, (((1,), (1,)), ((), ())), preferred_element_type=F32) * scale
            probs, inv = [], []
            for t in range(2):
                sink = sink_ref[h * GROUP + 2 * p + t]
                st = jnp.where(mask, s[:, t * 2 * W:(t + 1) * 2 * W], -jnp.inf)
                m = jnp.maximum(jnp.max(st, axis=-1, keepdims=True), sink)
                e = jnp.exp(st - m)
                denom = jnp.sum(e, axis=-1, keepdims=True) + jnp.exp(sink - m)
                probs.append(e.astype(BF16))
                inv.append(1.0 / denom)
            o2 = jnp.dot(jnp.concatenate(probs, axis=1), vv, preferred_element_type=F32)
            o2 = o2 * jnp.where(lane_o < ATT_HEAD_DIM, inv[0], inv[1])
            o_ref[0, :, c0:c0 + LANES] = o2.astype(o_ref.dtype)


def swa_prompt(qkv3, sinks):
    B, L, _ = qkv3.shape
    W = WINDOW
    kcol = Q_WIDTH // KV_WIDTH
    return pl.pallas_call(
        _swa_prompt_body,
        grid_spec=pltpu.PrefetchScalarGridSpec(
            num_scalar_prefetch=0,
            grid=(B, L // W),
            in_specs=[
                pl.BlockSpec(memory_space=pltpu.SMEM),
                pl.BlockSpec((1, W, Q_WIDTH), lambda b, j: (b, j, 0)),
                pl.BlockSpec((1, W, KV_WIDTH), lambda b, j: (b, j, kcol)),
                pl.BlockSpec((1, W, KV_WIDTH), lambda b, j: (b, jnp.maximum(j - 1, 0), kcol)),
                pl.BlockSpec((1, W, KV_WIDTH), lambda b, j: (b, j, kcol + 1)),
                pl.BlockSpec((1, W, KV_WIDTH), lambda b, j: (b, jnp.maximum(j - 1, 0), kcol + 1)),
            ],
            out_specs=pl.BlockSpec((1, W, Q_WIDTH), lambda b, j: (b, j, 0)),
        ),
        out_shape=jax.ShapeDtypeStruct((B, L, Q_WIDTH), BF16),
        compiler_params=_params("parallel", "parallel"),
        name="swa_prompt",
    )(sinks, qkv3, qkv3, qkv3, qkv3, qkv3)


def _swa_sample_body(q_ref, kn_ref, vn_ref, knf_ref, vnf_ref, kb_ref, vb_ref, sink_ref, o_ref, ko_ref, vo_ref):
    W = WINDOW
    scale = ATT_HEAD_DIM ** -0.5
    kb = kb_ref[0]
    vb = vb_ref[0]
    for h in range(N_KV_HEADS):
        lo, hi = h * ATT_HEAD_DIM, (h + 1) * ATT_HEAD_DIM
        qh = q_ref[0, h * GROUP:(h + 1) * GROUP, :]
        kn = kn_ref[0, h:h + 1, :]
        vn = vn_ref[0, h:h + 1, :]
        sink = sink_ref[h * GROUP:(h + 1) * GROUP, :]
        s_buf = lax.dot_general(qh.astype(BF16), kb[:, lo:hi].astype(BF16), (((1,), (1,)), ((), ())),
                                preferred_element_type=F32) * scale
        s_new = jnp.sum(qh * kn, axis=-1, keepdims=True) * scale
        m = jnp.maximum(jnp.maximum(jnp.max(s_buf, axis=-1, keepdims=True), s_new), sink)
        e_buf = jnp.exp(s_buf - m)
        e_new = jnp.exp(s_new - m)
        denom = jnp.sum(e_buf, axis=-1, keepdims=True) + e_new + jnp.exp(sink - m)
        o = jnp.dot(e_buf.astype(BF16), vb[:, lo:hi].astype(BF16), preferred_element_type=F32)
        o = o + e_new * vn
        o_ref[0, h * GROUP:(h + 1) * GROUP, :] = (o / denom).astype(o_ref.dtype)
    ko_ref[0, 0:W - 1, :] = kb[1:W, :]
    ko_ref[0, W - 1:W, :] = knf_ref[0]
    vo_ref[0, 0:W - 1, :] = vb[1:W, :]
    vo_ref[0, W - 1:W, :] = vnf_ref[0]


def swa_sample(q3, kn3, vn3, k_buf, v_buf, sinks):
    B = q3.shape[0]
    W, d = WINDOW, ATT_HEAD_DIM
    head_spec = lambda n: pl.BlockSpec((1, n, d), lambda b: (b, 0, 0))
    flat_spec = pl.BlockSpec((1, 1, KV_WIDTH), lambda b: (b, 0, 0))
    buf_spec = pl.BlockSpec((1, W, KV_WIDTH), lambda b: (b, 0, 0))
    return pl.pallas_call(
        _swa_sample_body,
        grid=(B,),
        in_specs=[head_spec(N_HEADS), head_spec(N_KV_HEADS), head_spec(N_KV_HEADS), flat_spec, flat_spec,
                  buf_spec, buf_spec, pl.BlockSpec((N_HEADS, 1), lambda b: (0, 0))],
        out_specs=[head_spec(N_HEADS), buf_spec, buf_spec],
        out_shape=[jax.ShapeDtypeStruct((B, N_HEADS, d), BF16),
                   jax.ShapeDtypeStruct((B, W, KV_WIDTH), F32),
                   jax.ShapeDtypeStruct((B, W, KV_WIDTH), F32)],
        compiler_params=_params("parallel"),
        name="swa_sample",
    )(q3, kn3, vn3, kn3.reshape(B, 1, KV_WIDTH), vn3.reshape(B, 1, KV_WIDTH), k_buf, v_buf,
      sinks.reshape(N_HEADS, 1))


def _shared_up_body(x_ref, wg_ref, wu_ref, h_ref):
    x = x_ref[...]
    g = jnp.dot(x, wg_ref[0].astype(BF16), preferred_element_type=F32)
    u = jnp.dot(x, wu_ref[0].astype(BF16), preferred_element_type=F32)
    h_ref[...] = (_silu(g) * u).astype(h_ref.dtype)


def _shared_down_body(h_ref, wd_ref, y_ref):
    y_ref[...] = jnp.dot(h_ref[...], wd_ref[0].astype(BF16), preferred_element_type=F32)


def shared_ffn(xb, w_gate, w_up, w_down, layer, tm=512):
    M, D = xb.shape
    FF = w_gate.shape[-1]
    tm = min(tm, M)
    wsel = lambda i: (layer, 0, 0)
    h = pl.pallas_call(
        _shared_up_body,
        grid=(M // tm,),
        in_specs=[pl.BlockSpec((tm, D), lambda i: (i, 0)),
                  pl.BlockSpec((1, D, FF), wsel), pl.BlockSpec((1, D, FF), wsel)],
        out_specs=pl.BlockSpec((tm, FF), lambda i: (i, 0)),
        out_shape=jax.ShapeDtypeStruct((M, FF), BF16),
        compiler_params=_params("parallel"),
        name="shared_up",
    )(xb, w_gate, w_up)
    return pl.pallas_call(
        _shared_down_body,
        grid=(M // tm,),
        in_specs=[pl.BlockSpec((tm, FF), lambda i: (i, 0)), pl.BlockSpec((1, FF, D), wsel)],
        out_specs=pl.BlockSpec((tm, D), lambda i: (i, 0)),
        out_shape=jax.ShapeDtypeStruct((M, D), F32),
        compiler_params=_params("parallel"),
        name="shared_down",
    )(h, w_down)


def _expert_changed(be_ref, i):
    return (i == 0) | (be_ref[i] != be_ref[jnp.maximum(i - 1, 0)])


GATHER_SLOTS = 3
DMA_QUEUES = 2
TOKEN_DMA_PRIORITY = 0
WEIGHT_DMA_PRIORITY = 1


def _stream_expert_weights(be_ref, nx_ref, i, w_hbm, stage, wsem, w_bf, layer):
    def copy(e, j):
        return pltpu.make_async_copy(w_hbm[j].at[layer, e], stage.at[j], wsem.at[j])

    @pl.when(i == 0)
    def _():
        for j in range(len(w_hbm)):
            copy(be_ref[0], j).start(priority=WEIGHT_DMA_PRIORITY)

    @pl.when(_expert_changed(be_ref, i))
    def _():
        e, nxt = be_ref[i], nx_ref[i]
        for j in range(len(w_hbm)):
            copy(e, j).wait()
            w_bf[j][...] = stage[j].astype(BF16)

            @pl.when(nxt != e)
            def _():
                copy(nxt, j).start(priority=WEIGHT_DMA_PRIORITY)


def _moe_up_body(be_ref, nx_ref, na_ref, tok_ref, x_hbm, wg_hbm, wu_hbm, h_ref, xbuf, sem, x_lhs, stage, wsem,
                 wg_bf, wu_bf, *, tm, layer):
    i = pl.program_id(0)
    na = na_ref[0]
    last_blk = pl.num_programs(0) - 1
    half = wg_bf.shape[0] // 2

    def issue(blk, slot):
        base = jnp.minimum(blk, last_blk) * tm
        for r in range(tm):
            src = pl.multiple_of(tok_ref[base + r], TOK_ROWS)
            pltpu.make_async_copy(x_hbm.at[pl.ds(src, TOK_ROWS)],
                                  xbuf.at[slot, pl.ds(r * TOK_ROWS, TOK_ROWS)],
                                  sem.at[slot]).start(priority=TOKEN_DMA_PRIORITY)

    def wait(slot):
        pltpu.make_async_copy(x_hbm.at[pl.ds(0, tm * TOK_ROWS)], xbuf.at[slot], sem.at[slot]).wait()

    @pl.when(i == 0)
    def _():
        issue(0, 0)
        issue(1, 1)

    @pl.when(i >= na)
    def _():
        h_ref[...] = jnp.zeros_like(h_ref)

    @pl.when(i < na)
    def _():
        slot = i % GATHER_SLOTS
        _stream_expert_weights(be_ref, nx_ref, i, (wg_hbm, wu_hbm), stage, wsem, (wg_bf, wu_bf), layer)
        wait(slot)
        for s in range(TOK_ROWS):
            lo, hi = _unpack_pair(_load_token_rows(xbuf.at[slot], s, tm))
            x_lhs[:, s * LANES:(s + 1) * LANES] = lo.astype(BF16)
            x_lhs[:, half + s * LANES:half + (s + 1) * LANES] = hi.astype(BF16)
        issue(i + 2, (i + 2) % GATHER_SLOTS)
        x = x_lhs[...]
        g = jnp.dot(x, wg_bf[...], preferred_element_type=F32)
        u = jnp.dot(x, wu_bf[...], preferred_element_type=F32)
        h_ref[...] = (_silu(g) * u).astype(h_ref.dtype)

        @pl.when(i + 1 == na)
        def _():
            wait((i + 1) % GATHER_SLOTS)
            wait((i + 2) % GATHER_SLOTS)


def _moe_down_body(be_ref, nx_ref, na_ref, dst_ref, h_ref, wd_hbm, y_hbm, ybuf, sem, stage, wsem, wd_bf, *, tm,
                   layer):
    i = pl.program_id(0)
    na = na_ref[0]

    def wait(slot):
        pltpu.make_async_copy(ybuf.at[slot], y_hbm.at[pl.ds(0, tm * TOK_ROWS)], sem.at[slot]).wait()

    def scatter(blk, slot):
        base = blk * tm
        for r in range(tm):
            dst = pl.multiple_of(dst_ref[base + r], TOK_ROWS)
            pltpu.make_async_copy(ybuf.at[slot, pl.ds(r * TOK_ROWS, TOK_ROWS)],
                                  y_hbm.at[pl.ds(dst, TOK_ROWS)], sem.at[slot]).start(priority=r % DMA_QUEUES)

    def compute(slot):
        y = jnp.dot(h_ref[...], wd_bf[...], preferred_element_type=F32)
        _store_token_rows(ybuf.at[slot], y, tm)

    @pl.when(i == 0)
    def _():
        plane = y_hbm.shape[0] // TOP_K
        spare = 2 * tm // TOP_K * TOK_ROWS
        ybuf[0] = jnp.zeros(ybuf.shape[1:], ybuf.dtype)
        for k in range(TOP_K):
            fill = pltpu.make_async_copy(ybuf.at[0, pl.ds(0, spare)],
                                         y_hbm.at[pl.ds((k + 1) * plane - spare, spare)], sem.at[0])
            fill.start()
            fill.wait()

    @pl.when(i < na)
    def _():
        _stream_expert_weights(be_ref, nx_ref, i, (wd_hbm,), stage, wsem, (wd_bf,), layer)

        @pl.when(i == 0)
        def _():
            compute(0)

        @pl.when(i > 0)
        def _():
            slot = i % 2

            @pl.when(i >= 2)
            def _():
                wait(slot)

            scatter(i - 1, 1 - slot)
            compute(slot)

    @pl.when(i == pl.num_programs(0) - 1)
    def _():
        last = na - 1

        @pl.when(na >= 2)
        def _():
            wait(na % 2)

        scatter(last, last % 2)
        wait(last % 2)


def routed_ffn(x_packed, w_gate, w_up, w_down, layer, plan, tm):
    slot_token, slot_dest, block_e, next_e, n_active = plan
    T = x_packed.shape[0] // TOK_ROWS
    NB = block_e.shape[0]
    D, FF = w_gate.shape[-2], w_gate.shape[-1]
    t_pad = T + 2 * tm // TOP_K
    buf = lambda n: pltpu.VMEM((n, tm * TOK_ROWS, LANES), jnp.uint32)
    row = lambda i, be, nx, na, ix: (jnp.minimum(i, na[0] - 1), 0)
    hbm = pl.BlockSpec(memory_space=pl.ANY)
    params = pltpu.CompilerParams(dimension_semantics=("arbitrary",), vmem_limit_bytes=V7X_VMEM_LIMIT)
    h = pl.pallas_call(
        functools.partial(_moe_up_body, tm=tm, layer=layer),
        grid_spec=pltpu.PrefetchScalarGridSpec(
            num_scalar_prefetch=4, grid=(NB,),
            in_specs=[hbm, hbm, hbm],
            out_specs=pl.BlockSpec((tm, FF), lambda i, be, nx, na, ix: (i, 0)),
            scratch_shapes=[buf(GATHER_SLOTS), pltpu.SemaphoreType.DMA((GATHER_SLOTS,)),
                            pltpu.VMEM((tm, D), BF16),
                            pltpu.VMEM((2, D, FF), F32), pltpu.SemaphoreType.DMA((2,)),
                            pltpu.VMEM((D, FF), BF16), pltpu.VMEM((D, FF), BF16)]),
        out_shape=jax.ShapeDtypeStruct((NB * tm, FF), BF16),
        compiler_params=params,
        name="moe_up",
    )(block_e, next_e, n_active, slot_token * TOK_ROWS, x_packed, w_gate, w_up)
    y = pl.pallas_call(
        functools.partial(_moe_down_body, tm=tm, layer=layer),
        grid_spec=pltpu.PrefetchScalarGridSpec(
            num_scalar_prefetch=4, grid=(NB,),
            in_specs=[pl.BlockSpec((tm, FF), row), hbm],
            out_specs=hbm,
            scratch_shapes=[buf(2), pltpu.SemaphoreType.DMA((2,)),
                            pltpu.VMEM((1, FF, D), F32), pltpu.SemaphoreType.DMA((1,)),
                            pltpu.VMEM((FF, D), BF16)]),
        out_shape=jax.ShapeDtypeStruct((TOP_K * t_pad * TOK_ROWS, LANES), jnp.uint32),
        compiler_params=params,
        name="moe_down",
    )(block_e, next_e, n_active, slot_dest * TOK_ROWS, h, w_down)
    return y.reshape(TOP_K, t_pad * TOK_ROWS, LANES)


SLOT_MAP_CHUNK = 2048
SLOT_MAP_UNROLL = 32


def _slot_map_body(dest_ref, src_ref, *, clear_chunk):
    phase, i = pl.program_id(0), pl.program_id(1)

    @pl.when(phase == 0)
    def _():
        def clear(j, carry):
            src_ref[i * clear_chunk + j] = -1
            return carry

        lax.fori_loop(0, clear_chunk, clear, 0, unroll=SLOT_MAP_UNROLL)

    @pl.when(phase == 1)
    def _():
        def place(a, carry):
            src_ref[dest_ref[a]] = i * SLOT_MAP_CHUNK + a
            return carry

        lax.fori_loop(0, SLOT_MAP_CHUNK, place, 0, unroll=SLOT_MAP_UNROLL)


def _slot_map(dest, n_slots):
    A = dest.shape[0]
    n_chunks = -(-A // SLOT_MAP_CHUNK)
    clear_chunk = -(-(n_slots + 1) // (SLOT_MAP_UNROLL * n_chunks)) * SLOT_MAP_UNROLL
    dest = jnp.concatenate([dest, jnp.full((n_chunks * SLOT_MAP_CHUNK - A,), n_slots, jnp.int32)])
    src = pl.pallas_call(
        functools.partial(_slot_map_body, clear_chunk=clear_chunk),
        grid=(2, n_chunks),
        in_specs=[pl.BlockSpec((SLOT_MAP_CHUNK,), lambda ph, i: (i * ph,), memory_space=pltpu.SMEM)],
        out_specs=pl.BlockSpec(memory_space=pltpu.SMEM),
        out_shape=jax.ShapeDtypeStruct((n_chunks * clear_chunk,), jnp.int32),
        compiler_params=pltpu.CompilerParams(dimension_semantics=("arbitrary", "arbitrary")),
        name="slot_map",
    )(dest)
    return src[:n_slots]


def _dispatch_plan(idx, tm):
    T = idx.shape[0]
    A = T * TOP_K
    NB = -(-A // tm) + N_EXPERTS
    spare_t = 2 * tm // TOP_K
    t_pad = T + spare_t
    idx = idx.astype(jnp.int32)
    onehot = idx[:, :, None] == jnp.arange(N_EXPERTS, dtype=jnp.int32)[None, None, :]
    sel = onehot.any(axis=1).astype(jnp.int32)
    incl = jnp.cumsum(sel, axis=0)
    counts = incl[-1]
    padded = (counts + tm - 1) // tm * tm
    pad_end = jnp.cumsum(padded)
    pad_start = pad_end - padded
    n_active = (pad_end[-1] // tm).astype(jnp.int32)
    blk = jnp.arange(NB, dtype=jnp.int32)
    block_e = jnp.minimum(jnp.searchsorted(pad_end, jnp.minimum(blk, n_active - 1) * tm, side='right'),
                          N_EXPERTS - 1).astype(jnp.int32)
    e_ids = jnp.arange(N_EXPERTS, dtype=jnp.int32)
    later_used = lax.cummin(jnp.where(counts > 0, e_ids, N_EXPERTS), reverse=True)
    next_used = jnp.concatenate([later_used[1:], jnp.full((1,), N_EXPERTS, jnp.int32)])
    next_e = jnp.where(next_used[block_e] < N_EXPERTS, next_used[block_e], block_e).astype(jnp.int32)
    slot_of = (incl - sel + pad_start[None, :])[:, None, :]
    dest = jnp.sum(jnp.where(onehot, slot_of, 0), axis=-1).reshape(A)
    slot_src = _slot_map(dest, NB * tm)
    slot = jnp.arange(NB * tm, dtype=jnp.int32)
    valid = slot_src >= 0
    slot_token = jnp.where(valid, slot_src // TOP_K, 0)
    s2 = (slot // tm % 2) * tm + slot % tm
    spare = (s2 // spare_t) * t_pad + T + s2 % spare_t
    slot_dest = jnp.where(valid, (slot_src % TOP_K) * t_pad + slot_src // TOP_K, spare)
    return slot_token, slot_dest, block_e, next_e, n_active.reshape(1)


def moe_block(x_parts, xb_parts, xp_parts, p, layer):
    w_t = jnp.swapaxes(p['router_w'], 1, 2)
    bias_col = p['router_bias'][:, :, None]
    routed = [route(x, w_t, bias_col, layer) for x in x_parts]
    idx = jnp.concatenate([r[0] for r in routed], axis=1).T
    wts = jnp.concatenate([r[1] for r in routed], axis=1).T
    plan = _dispatch_plan(idx, MOE_TM)
    y = routed_ffn(jnp.concatenate(xp_parts, axis=0), p['exp_w_gate'], p['exp_w_up'], p['exp_w_down'],
                   layer, plan, MOE_TM)
    shared = [shared_ffn(xb, p['shared_w_gate'], p['shared_w_up'], p['shared_w_down'], layer)
              for xb in xb_parts]
    return y, wts, shared


def _rope_tables(pos, half):
    inv_freq = ROPE_THETA ** (-jnp.arange(half, dtype=F32) / half)
    ang = pos.astype(F32)[:, None] * inv_freq[None, :]
    return jnp.cos(ang), jnp.sin(ang)


def _even_prompt(x_bf, B, L, p, i):
    proj = matmul(x_bf, p['w_in_even'][i].astype(BF16), name="in_proj_even")
    proj3 = proj.reshape(B, L, -1)
    cos, sin = _rope_tables(jnp.arange(L, dtype=jnp.int32), RET_HEAD_DIM // 2)
    conv_out, conv_new = conv_prompt(proj3, p['conv_w'][i], p['conv_b'][i], p['conv_ln_g'][i], p['conv_ln_b'][i])
    ret_out, s_new = retention_prompt(proj3, cos, sin, p['ret_gn_g'][i], p['ret_gn_b'][i])
    mixed = jnp.concatenate([conv_out, ret_out], axis=-1).reshape(B * L, -1)
    return matmul(mixed, p['w_out_even'][i].astype(BF16), name="out_proj_even"), conv_new, s_new


def _even_sample(x_bf, conv_buf, ret_state, p, i):
    B = x_bf.shape[0]
    H, d = RET_HEADS, RET_HEAD_DIM
    proj = matmul(x_bf, p['w_in_even'][i].astype(BF16), name="in_proj_even")
    conv_out, conv_new = conv_sample(proj, conv_buf, p['conv_w'][i], p['conv_b'][i],
                                     p['conv_ln_g'][i], p['conv_ln_b'][i])
    o2 = 2 * CONV_CH
    q_col = proj[:, o2:o2 + RET_WIDTH].reshape(B, H, d, 1)
    k_col = proj[:, o2 + RET_WIDTH:o2 + 2 * RET_WIDTH].reshape(B, H, d, 1)
    v_row = proj[:, o2 + 2 * RET_WIDTH:o2 + 3 * RET_WIDTH].reshape(B, 1, RET_WIDTH)
    g_row = proj[:, o2 + 3 * RET_WIDTH:].reshape(B, 1, RET_WIDTH)
    cos, sin = _rope_tables(jnp.full((1,), PAST_LEN, jnp.int32), d // 2)
    ret_out, s_new = retention_sample(q_col, k_col, v_row, g_row, ret_state, cos.reshape(d // 2, 1),
                                      sin.reshape(d // 2, 1), p['ret_gn_g'][i], p['ret_gn_b'][i])
    mixed = jnp.concatenate([conv_out, ret_out.reshape(B, RET_WIDTH)], axis=-1)
    return matmul(mixed, p['w_out_even'][i].astype(BF16), name="out_proj_even"), conv_new, s_new


def _att_rope_tables(pos, rows):
    cos, sin = _rope_tables(pos, ATT_HEAD_DIM // 2)
    cos = jnp.tile(cos, (rows // cos.shape[0], LANES // cos.shape[1]))
    sin = jnp.tile(sin, (rows // sin.shape[0], LANES // sin.shape[1]))
    return cos, sin


def _odd_prompt(x_bf, B, L, p, i):
    cos, sin = _att_rope_tables(jnp.arange(L, dtype=jnp.int32), L)
    qkv = matmul(x_bf, p['w_qkv_odd'][i].astype(BF16), bias=p['b_qkv_odd'][i],
                 rope=(cos, sin, Q_WIDTH + KV_WIDTH), name="qkv_proj_odd")
    qkv3 = qkv.reshape(B, L, -1)
    o = swa_prompt(qkv3, p['sinks'][i])
    out = matmul(o.reshape(B * L, Q_WIDTH), p['w_out_odd'][i].astype(BF16), bias=p['b_out_odd'][i],
                 name="out_proj_odd")
    k_new = qkv3[:, L - WINDOW:, Q_WIDTH:Q_WIDTH + KV_WIDTH].reshape(B, WINDOW, N_KV_HEADS, ATT_HEAD_DIM)
    v_new = qkv3[:, L - WINDOW:, Q_WIDTH + KV_WIDTH:].reshape(B, WINDOW, N_KV_HEADS, ATT_HEAD_DIM)
    return out, k_new, v_new


def _odd_sample(x_bf, k_buf, v_buf, p, i):
    B = x_bf.shape[0]
    cos, sin = _att_rope_tables(jnp.full((1,), PAST_LEN, jnp.int32), B)
    qkv = matmul(x_bf, p['w_qkv_odd'][i].astype(BF16), bias=p['b_qkv_odd'][i],
                 rope=(cos, sin, Q_WIDTH + KV_WIDTH), name="qkv_proj_odd")
    q3 = qkv[:, :Q_WIDTH].reshape(B, N_HEADS, ATT_HEAD_DIM)
    kn3 = qkv[:, Q_WIDTH:Q_WIDTH + KV_WIDTH].reshape(B, N_KV_HEADS, ATT_HEAD_DIM)
    vn3 = qkv[:, Q_WIDTH + KV_WIDTH:].reshape(B, N_KV_HEADS, ATT_HEAD_DIM)
    o, k_new, v_new = swa_sample(q3, kn3, vn3, k_buf.reshape(B, WINDOW, KV_WIDTH),
                                 v_buf.reshape(B, WINDOW, KV_WIDTH), p['sinks'][i])
    out = matmul(o.reshape(B, Q_WIDTH), p['w_out_odd'][i].astype(BF16), bias=p['b_out_odd'][i],
                 name="out_proj_odd")
    shape = (B, WINDOW, N_KV_HEADS, ATT_HEAD_DIM)
    return out, k_new.reshape(shape), v_new.reshape(shape)


def kernel(x_prompt, x_sample, state_conv, state_ret, state_win_k, state_win_v,
           w_in_even, conv_w, conv_b, conv_ln_g, conv_ln_b, ret_gn_g, ret_gn_b, w_out_even,
           w_qkv_odd, b_qkv_odd, sinks, w_out_odd, b_out_odd, ln_g, ln_b,
           router_w, router_bias, exp_w_gate, exp_w_up, exp_w_down,
           shared_w_gate, shared_w_up, shared_w_down):
    p = dict(w_in_even=w_in_even, conv_w=conv_w, conv_b=conv_b, conv_ln_g=conv_ln_g, conv_ln_b=conv_ln_b,
             ret_gn_g=ret_gn_g, ret_gn_b=ret_gn_b, w_out_even=w_out_even,
             w_qkv_odd=w_qkv_odd, b_qkv_odd=b_qkv_odd, sinks=sinks, w_out_odd=w_out_odd, b_out_odd=b_out_odd,
             ln_g=ln_g, ln_b=ln_b, router_w=router_w, router_bias=router_bias,
             exp_w_gate=exp_w_gate, exp_w_up=exp_w_up, exp_w_down=exp_w_down,
             shared_w_gate=shared_w_gate, shared_w_up=shared_w_up, shared_w_down=shared_w_down)
    B, L, D = x_prompt.shape
    Bs = x_sample.shape[0]
    Tp = B * L
    xp = x_prompt.reshape(Tp, D)
    xs = x_sample.reshape(Bs, D)
    xp_bf, xs_bf = xp.astype(BF16), xs.astype(BF16)
    conv_p, conv_s, ret_p, ret_s, wk_p, wk_s, wv_p, wv_s = [], [], [], [], [], [], [], []
    for layer in range(DEPTH):
        i = layer // 2
        if layer % 2 == 0:
            hp, c_new, r_new = _even_prompt(xp_bf, B, L, p, i)
            conv_p.append(c_new)
            ret_p.append(r_new)
            hs, c_new, r_new = _even_sample(xs_bf, state_conv[i], state_ret[i], p, i)
            conv_s.append(c_new)
            ret_s.append(r_new)
        else:
            hp, k_new, v_new = _odd_prompt(xp_bf, B, L, p, i)
            wk_p.append(k_new)
            wv_p.append(v_new)
            hs, k_new, v_new = _odd_sample(xs_bf, state_win_k[i], state_win_v[i], p, i)
            wk_s.append(k_new)
            wv_s.append(v_new)
        xp, xp_bf, xp_pk = deepnorm(xp, [(hp, 0)], ln_g[layer, 0], ln_b[layer, 0], name="ln_mixer_prompt")
        xs, xs_bf, xs_pk = deepnorm(xs, [(hs, 0)], ln_g[layer, 0], ln_b[layer, 0], name="ln_mixer_sample")
        y, wts, shared = moe_block([xp, xs], [xp_bf, xs_bf], [xp_pk, xs_pk], p, layer)
        xp, xp_bf, _ = deepnorm(xp, [(shared[0], 0)], ln_g[layer, 1], ln_b[layer, 1], combine=(y, wts, 0),
                                tm=64, name="ln_moe_prompt")
        xs, xs_bf, _ = deepnorm(xs, [(shared[1], 0)], ln_g[layer, 1], ln_b[layer, 1], combine=(y, wts, Tp),
                                tm=32, name="ln_moe_sample")
    return (xp.reshape(B, L, D), xs.reshape(Bs, 1, D),
            jnp.stack(conv_p), jnp.stack(conv_s), jnp.stack(ret_p), jnp.stack(ret_s),
            jnp.stack(wk_p), jnp.stack(wk_s), jnp.stack(wv_p), jnp.stack(wv_s))
```
